```python
import jax, jax.numpy as jnp
from jax import lax
import numpy as np

D_MODEL = 2048
BATCH = 4
SEQ = 2048
DEPTH = 1
DEC_BATCH = 8
DEC_SEQ = 4096
PAST_LEN = 128

ATT_GROUPS = ((128, 1), (512, 4), (2048, 16))
ATT_HEADS_PER_GROUP = 4
ATT_HEAD_DIM = 128
ATT_HEADS = ATT_HEADS_PER_GROUP * len(ATT_GROUPS)
ATT_WIDTH = ATT_HEADS * ATT_HEAD_DIM
ATT_OUT = ATT_HEADS_PER_GROUP * ATT_HEAD_DIM
ROPE_THETA = 10000.0
RW_HEAD = 64
RW_WIDTH = D_MODEL // 2
RW_HEADS = RW_WIDTH // RW_HEAD
DECAY_LORA = 64
ICLR_LORA = 64
GATE_LORA = 160
RW_COLS = 3 * RW_WIDTH + 2 * DECAY_LORA + 2 * ICLR_LORA + GATE_LORA
N_IN = 3 * ATT_WIDTH + RW_COLS + 2 * D_MODEL
D_FF = ((8 * D_MODEL // 3 + 127) // 128) * 128
RMS_EPS = 1e-6
GN_EPS = 64e-5
NEG_INF = -1e30

kernel_name = 'hybrid_dilated_attn_rwkv7_encoder'


def rmsnorm(x, g):
    xf = x.astype(jnp.float32)
    y = xf * lax.rsqrt(jnp.mean(xf * xf, axis=-1, keepdims=True) + RMS_EPS)
    return (y * g.astype(jnp.float32)).astype(x.dtype)


def shift_prev(x):
    return jnp.pad(x, ((0, 0), (1, 0), (0, 0)))[:, :-1]


def shift_next(x):
    return jnp.pad(x, ((0, 0), (0, 1), (0, 0)))[:, 1:]


def rotary(x, pos):
    half = x.shape[-1] // 2
    inv_freq = 1.0 / (ROPE_THETA ** (jnp.arange(half, dtype=jnp.float32) / half))
    ang = pos.astype(jnp.float32)[:, None] * inv_freq[None, :]
    cos = jnp.cos(ang)[None, :, None, :]
    sin = jnp.sin(ang)[None, :, None, :]
    xf = x.astype(jnp.float32)
    x1, x2 = xf[..., :half], xf[..., half:]
    return jnp.concatenate([x1 * cos - x2 * sin, x2 * cos + x1 * sin], axis=-1).astype(x.dtype)


def dilated_window_attention(q, k, v, window, dilation):
    B, T, H, Dh = q.shape
    span = window // (2 * dilation)
    blk = span
    L = T // dilation
    nb = -(-L // blk)
    Lp = nb * blk

    def classes(t):
        t = t.reshape(B, L, dilation, H, Dh).transpose(0, 2, 1, 3, 4)
        return jnp.pad(t, ((0, 0), (0, 0), (0, Lp - L), (0, 0), (0, 0)))

    def band(t):
        t = jnp.pad(classes(t), ((0, 0), (0, 0), (blk, blk), (0, 0), (0, 0)))
        t = t.reshape(B, dilation, nb + 2, blk, H, Dh)
        return jnp.concatenate([t[:, :, :-2], t[:, :, 1:-1], t[:, :, 2:]], axis=3)

    qb = classes(q).reshape(B, dilation, nb, blk, H, Dh)
    kb = band(k)
    vb = band(v)
    qi = (jnp.arange(nb) * blk)[:, None] + jnp.arange(blk)[None, :]
    ki = (jnp.arange(nb) * blk)[:, None] + jnp.arange(-blk, 2 * blk)[None, :]
    mask = ((jnp.abs(qi[:, :, None] - ki[:, None, :]) <= span)
            & (ki[:, None, :] >= 0) & (ki[:, None, :] < L))
    s = jnp.einsum('bgnqhd,bgnkhd->bgnhqk', qb, kb, preferred_element_type=jnp.float32) * (Dh ** -0.5)
    s = jnp.where(mask[None, None, :, None], s, NEG_INF)
    m = jnp.max(s, axis=-1, keepdims=True)
    p = jnp.exp(s - m)
    l = jnp.sum(p, axis=-1, keepdims=True)
    o = jnp.einsum('bgnhqk,bgnkhd->bgnqhd', p / l, vb.astype(jnp.float32))
    lse = (m + jnp.log(l))[..., 0].transpose(0, 1, 2, 4, 3)
    o = o.reshape(B, dilation, Lp, H, Dh)[:, :, :L].transpose(0, 2, 1, 3, 4).reshape(B, T, H, Dh)
    lse = lse.reshape(B, dilation, Lp, H)[:, :, :L].transpose(0, 2, 1, 3).reshape(B, T, H)
    return o, lse


def dilated_attention_branch(q, k, v):
    B, T, _ = q.shape
    pos = jnp.arange(T)
    q = rotary(q.reshape(B, T, ATT_HEADS, ATT_HEAD_DIM), pos)
    k = rotary(k.reshape(B, T, ATT_HEADS, ATT_HEAD_DIM), pos)
    v = v.reshape(B, T, ATT_HEADS, ATT_HEAD_DIM)
    outs, lses = [], []
    for gi, (window, dilation) in enumerate(ATT_GROUPS):
        hs = slice(gi * ATT_HEADS_PER_GROUP, (gi + 1) * ATT_HEADS_PER_GROUP)
        o, lse = dilated_window_attention(q[:, :, hs], k[:, :, hs], v[:, :, hs], window, dilation)
        outs.append(o)
        lses.append(lse)
    wts = jax.nn.softmax(jnp.stack(lses), axis=0)[..., None]
    out = jnp.sum(wts * jnp.stack(outs), axis=0)
    return out.reshape(B, T, ATT_OUT).astype(q.dtype)


def rwkv7_bidir_branch(z, shift_mu, decay_base, decay_up, iclr_base, iclr_up, gate_up,
                       k_k, k_a, r_k, lnx_w, lnx_b):
    B, T, _ = z.shape
    f32 = jnp.float32
    z = z + shift_mu[0] * (shift_prev(z) - z) + shift_mu[1] * (shift_next(z) - z)
    c1, c2, c3 = RW_WIDTH, 2 * RW_WIDTH, 3 * RW_WIDTH
    c4 = c3 + 2 * DECAY_LORA
    c5 = c4 + 2 * ICLR_LORA
    r = z[..., :c1].astype(f32)
    k = z[..., c1:c2].astype(f32)
    v = z[..., c2:c3].astype(f32)
    wl = z[..., c3:c4].reshape(B, T, 2, DECAY_LORA).astype(f32)
    al = z[..., c4:c5].reshape(B, T, 2, ICLR_LORA).astype(f32)
    gl = z[..., c5:].astype(f32)
    w_raw = decay_base.astype(f32) + jnp.einsum('btzr,zrc->btzc', jnp.tanh(wl), decay_up.astype(f32))
    decay = jnp.exp(-jnp.exp(-jax.nn.softplus(-w_raw) - 0.5))
    a = jax.nn.sigmoid(iclr_base.astype(f32) + jnp.einsum('btzr,zrc->btzc', al, iclr_up.astype(f32)))
    g = jax.nn.sigmoid(gl) @ gate_up.astype(f32)

    def heads(t):
        return t.reshape(t.shape[:-1] + (RW_HEADS, RW_HEAD))

    kk = heads(k * k_k.astype(f32))
    kk = kk / jnp.maximum(jnp.sqrt(jnp.sum(kk * kk, axis=-1, keepdims=True)), 1e-12)
    k_dir = heads(k[:, :, None, :] * (1.0 + (a - 1.0) * k_a.astype(f32)))

    def shared(t):
        t = jnp.swapaxes(t, 0, 1)
        return jnp.stack([t, t[::-1]], axis=1)

    def per_dir(t):
        t = t.transpose(1, 2, 0, 3, 4)
        return jnp.stack([t[:, 0], t[::-1, 1]], axis=1)

    def step(S, inp):
        r_t, w_t, k_t, v_t, kk_t, a_t = inp
        sa = jnp.einsum('zbhij,zbhj->zbhi', S, -kk_t)
        S = (S * w_t[..., None, :] + sa[..., :, None] * (kk_t * a_t)[..., None, :]
             + v_t[..., :, None] * k_t[..., None, :])
        y = jnp.einsum('zbhij,zbhj->zbhi', S, r_t)
        return S, y

    S0 = jnp.zeros((2, B, RW_HEADS, RW_HEAD, RW_HEAD), f32)
    xs = (shared(heads(r)), per_dir(heads(decay)), per_dir(k_dir), shared(heads(v)), shared(kk),
          per_dir(heads(a)))
    _, ys = lax.scan(step, S0, xs)
    y = jnp.swapaxes(ys[:, 0] + ys[::-1, 1], 0, 1)
    mu = jnp.mean(y, axis=-1, keepdims=True)
    var = jnp.mean(jnp.square(y - mu), axis=-1, keepdims=True)
    y = ((y - mu) * lax.rsqrt(var + GN_EPS)).reshape(B, T, RW_WIDTH) * lnx_w.astype(f32) + lnx_b.astype(f32)
    bonus = jnp.sum(jnp.sum(heads(r)[:, :, None] * k_dir * r_k.astype(f32), axis=-1, keepdims=True), axis=2)
    y = y + (bonus * heads(v)).reshape(B, T, RW_WIDTH)
    return (y * g).astype(z.dtype)


def centred_dwconv3(u, w, b):
    return w[0] * shift_prev(u) + w[1] * u + w[2] * shift_next(u) + b


def encoder_layer(x, c, ln_mix_pre, ln_mix_post, ln_ffn_pre, ln_ffn_post, w_ada, b_ada, w_in,
                  shift_mu, decay_base, decay_up, iclr_base, iclr_up, gate_up, k_k, k_a, r_k,
                  lnx_w, lnx_b, w_att_branch, w_rwkv_branch, w_out,
                  w_ffn_gate, w_ffn_up, ffn_conv_w, ffn_conv_b, w_ffn_down):
    D = D_MODEL
    ada = (jax.nn.silu(c) @ w_ada + b_ada)[:, None, :]
    sh1, sc1, gt1, sh2, sc2, gt2 = jnp.split(ada, 6, axis=-1)
    h = rmsnorm(x, ln_mix_pre) * (1 + sc1) + sh1
    proj = h @ w_in
    o1, o2, o3 = ATT_WIDTH, 2 * ATT_WIDTH, 3 * ATT_WIDTH
    o4 = o3 + RW_COLS
    att = dilated_attention_branch(proj[..., :o1], proj[..., o1:o2], proj[..., o2:o3])
    rw = rwkv7_bidir_branch(proj[..., o3:o4], shift_mu, decay_base, decay_up, iclr_base, iclr_up,
                            gate_up, k_k, k_a, r_k, lnx_w, lnx_b)
    gate_att = jax.nn.sigmoid(proj[..., o4:o4 + D])
    gate_rw = jax.nn.sigmoid(proj[..., o4 + D:])
    mix = (gate_att * (att @ w_att_branch) + gate_rw * (rw @ w_rwkv_branch)) @ w_out
    x = x + gt1 * rmsnorm(mix, ln_mix_post)
    h = rmsnorm(x, ln_ffn_pre) * (1 + sc2) + sh2
    u = centred_dwconv3(h @ w_ffn_gate, ffn_conv_w, ffn_conv_b)
    f = (jax.nn.gelu(u, approximate=True) * (h @ w_ffn_up)) @ w_ffn_down
    return x + gt2 * rmsnorm(f, ln_ffn_post)


def setup_inputs(seed: int = 0) -> dict:
    key = jax.random.key(seed)
    ks = jax.random.split(key, 32)
    f32 = jnp.float32
    L, D = DEPTH, D_MODEL

    def nrm(k, shape, scale):
        return jax.random.normal(k, shape, f32) * scale

    def gain(k, shape, centre=1.0):
        return centre + 0.05 * jax.random.normal(k, shape, f32)

    return {
        'x_prompt': nrm(ks[0], (BATCH, SEQ, D), 1.0),
        'x_sample': nrm(ks[1], (DEC_BATCH, DEC_SEQ, D), 1.0),
        'c_prompt': nrm(ks[2], (BATCH, D), 1.0),
        'c_sample': nrm(ks[3], (DEC_BATCH, D), 1.0),
        'ln_mix_pre': gain(ks[4], (L, D)),
        'ln_mix_post': gain(ks[5], (L, D)),
        'ln_ffn_pre': gain(ks[6], (L, D)),
        'ln_ffn_post': gain(ks[7], (L, D)),
        'w_ada': nrm(ks[8], (L, D, 6 * D), 0.5 * D ** -0.5),
        'b_ada': nrm(ks[9], (L, 6 * D), 0.02),
        'w_in': nrm(ks[10], (L, D, N_IN), D ** -0.5),
        'shift_mu': jax.random.uniform(ks[11], (L, 2, RW_COLS), f32, 0.0, 0.5),
        'decay_base': jax.random.uniform(ks[12], (L, 2, RW_WIDTH), f32, -6.0, 0.0),
        'decay_up': nrm(ks[13], (L, 2, DECAY_LORA, RW_WIDTH), 0.5 * DECAY_LORA ** -0.5),
        'iclr_base': nrm(ks[14], (L, 2, RW_WIDTH), 0.3),
        'iclr_up': nrm(ks[15], (L, 2, ICLR_LORA, RW_WIDTH), 0.5 * ICLR_LORA ** -0.5),
        'gate_up': nrm(ks[16], (L, GATE_LORA, RW_WIDTH), GATE_LORA ** -0.5),
        'k_k': gain(ks[17], (L, RW_WIDTH), 0.85),
        'k_a': gain(ks[18], (L, RW_WIDTH), 1.0),
        'r_k': nrm(ks[19], (L, RW_HEADS, RW_HEAD), 0.1),
        'lnx_w': gain(ks[20], (L, RW_WIDTH)),
        'lnx_b': nrm(ks[21], (L, RW_WIDTH), 0.02),
        'w_att_branch': nrm(ks[22], (L, ATT_OUT, D), ATT_OUT ** -0.5),
        'w_rwkv_branch': nrm(ks[23], (L, RW_WIDTH, D), RW_WIDTH ** -0.5),
        'w_out': nrm(ks[24], (L, D, D), D ** -0.5),
        'w_ffn_gate': nrm(ks[25], (L, D, D_FF), D ** -0.5),
        'w_ffn_up': nrm(ks[26], (L, D, D_FF), D ** -0.5),
        'ffn_conv_w': nrm(ks[27], (L, 3, D_FF), 3 ** -0.5),
        'ffn_conv_b': nrm(ks[28], (L, D_FF), 0.02),
        'w_ffn_down': nrm(ks[29], (L, D_FF, D), D_FF ** -0.5),
    }


def reference(x_prompt, x_sample, c_prompt, c_sample, ln_mix_pre, ln_mix_post, ln_ffn_pre, ln_ffn_post,
              w_ada, b_ada, w_in, shift_mu, decay_base, decay_up, iclr_base, iclr_up, gate_up,
              k_k, k_a, r_k, lnx_w, lnx_b, w_att_branch, w_rwkv_branch, w_out,
              w_ffn_gate, w_ffn_up, ffn_conv_w, ffn_conv_b, w_ffn_down):
    def trunk(x, c):
        for l in range(DEPTH):
            x = encoder_layer(x, c, ln_mix_pre[l], ln_mix_post[l], ln_ffn_pre[l], ln_ffn_post[l],
                              w_ada[l], b_ada[l], w_in[l], shift_mu[l], decay_base[l], decay_up[l],
                              iclr_base[l], iclr_up[l], gate_up[l], k_k[l], k_a[l], r_k[l],
                              lnx_w[l], lnx_b[l], w_att_branch[l], w_rwkv_branch[l], w_out[l],
                              w_ffn_gate[l], w_ffn_up[l], ffn_conv_w[l], ffn_conv_b[l], w_ffn_down[l])
        return x

    y_prompt = trunk(x_prompt, c_prompt)
    y_sample = trunk(x_sample, c_sample)
    return (y_prompt, y_sample)
```

```python
import functools

import jax
import jax.numpy as jnp
from jax import lax
from jax.experimental import pallas as pl
from jax.experimental.pallas import tpu as pltpu

F32 = jnp.float32
BF16 = jnp.bfloat16

D_MODEL = 2048
ATT_GROUPS = ((128, 1), (512, 4), (2048, 16))
ATT_HEADS_PER_GROUP = 4
ATT_HEAD_DIM = 128
ATT_HEADS = ATT_HEADS_PER_GROUP * len(ATT_GROUPS)
ATT_WIDTH = ATT_HEADS * ATT_HEAD_DIM
ATT_OUT = ATT_HEADS_PER_GROUP * ATT_HEAD_DIM
ROPE_THETA = 10000.0
RW_HEAD = 64
RW_WIDTH = D_MODEL // 2
RW_HEADS = RW_WIDTH // RW_HEAD
DECAY_LORA = 64
ICLR_LORA = 64
GATE_LORA = 160
RW_COLS = 3 * RW_WIDTH + 2 * DECAY_LORA + 2 * ICLR_LORA + GATE_LORA
N_IN = 3 * ATT_WIDTH + RW_COLS + 2 * D_MODEL
D_FF = ((8 * D_MODEL // 3 + 127) // 128) * 128
RMS_EPS = 1e-6
GN_EPS = 64e-5
NEG_INF = -1e30

LANES = 128
SUBLANES = 8
VMEM_BYTES_V7X = 64 * 1024 * 1024
VMEM_LIMIT = VMEM_BYTES_V7X - 8 * 1024 * 1024

RW_OFF = 3 * ATT_WIDTH
RW_PAD = (-RW_COLS) % LANES
GATE_OFF = RW_OFF + RW_COLS + RW_PAD
N_IN_PAD = GATE_OFF + 2 * D_MODEL
FF_TILE = 512
D_FF_PAD = -(-D_FF // FF_TILE) * FF_TILE
RW_CHUNK = 64
ATT_SPAN = 64


def _cparams(sem):
    return pltpu.CompilerParams(dimension_semantics=sem, vmem_limit_bytes=VMEM_LIMIT)


def _rms(x, gain):
    return x * lax.rsqrt(jnp.mean(x * x, axis=-1, keepdims=True) + RMS_EPS) * gain


def _ada_kernel(c_ref, w_ref, b_ref, o_ref):
    c = c_ref[...]
    s = c * jax.nn.sigmoid(c)
    o_ref[...] = jnp.dot(s.astype(BF16), w_ref[...].astype(BF16),
                         preferred_element_type=F32) + b_ref[...]


def _ada(c, w, b, tn=1024):
    m, k = c.shape
    n = w.shape[1]
    return pl.pallas_call(
        _ada_kernel,
        grid=(n // tn,),
        in_specs=[pl.BlockSpec((m, k), lambda j: (0, 0)),
                  pl.BlockSpec((k, tn), lambda j: (0, j)),
                  pl.BlockSpec((1, tn), lambda j: (0, j))],
        out_specs=pl.BlockSpec((m, tn), lambda j: (0, j)),
        out_shape=jax.ShapeDtypeStruct((m, n), F32),
        compiler_params=_cparams(("parallel",)),
    )(c, w, b)


def _inproj_kernel(x_ref, g_ref, sc_ref, sh_ref, w_ref, o_ref, h_ref):
    @pl.when(pl.program_id(1) == 0)
    def _():
        h = _rms(x_ref[...], g_ref[...]) * (1.0 + sc_ref[0]) + sh_ref[0]
        h_ref[...] = h.astype(BF16)

    o_ref[...] = jnp.dot(h_ref[...], w_ref[...], preferred_element_type=F32)


def _inproj(x, gain, sc, sh, w, seq, tm=512, tn=1536):
    m, k = x.shape
    n = w.shape[1]
    bpr = seq // tm
    return pl.pallas_call(
        _inproj_kernel,
        grid=(m // tm, n // tn),
        in_specs=[pl.BlockSpec((tm, k), lambda i, j: (i, 0)),
                  pl.BlockSpec((1, k), lambda i, j: (0, 0)),
                  pl.BlockSpec((1, 1, k), lambda i, j: (i // bpr, 0, 0)),
                  pl.BlockSpec((1, 1, k), lambda i, j: (i // bpr, 0, 0)),
                  pl.BlockSpec((k, tn), lambda i, j: (0, j))],
        out_specs=pl.BlockSpec((tm, tn), lambda i, j: (i, j)),
        out_shape=jax.ShapeDtypeStruct((m, n), F32),
        scratch_shapes=[pltpu.VMEM((tm, k), BF16)],
        compiler_params=_cparams(("parallel", "arbitrary")),
    )(x, gain, sc, sh, w)


def _attn_kernel(q_ref, k_ref, v_ref, cos_ref, sin_ref, o_ref, qr_ref, kr_ref, og_ref, lse_ref, *, seq):
    gid = pl.program_id(2)
    rows = 512
    half = ATT_HEAD_DIM // 2

    def rot(i, _):
        sl = pl.ds(pl.multiple_of(i * rows, rows), rows)
        cos = cos_ref[sl, :]
        sin = sin_ref[sl, :]
        q = q_ref[0, sl, :]
        k = k_ref[0, sl, :]
        qr_ref[sl, :] = (q * cos + pltpu.roll(q, half, 1) * sin) * (ATT_HEAD_DIM ** -0.5)
        kr_ref[sl, :] = k * cos + pltpu.roll(k, half, 1) * sin
        return 0

    lax.fori_loop(0, seq // rows, rot, 0)

    def group(gi, dil):
        cls_len = seq // dil
        bq = min(128, cls_len)
        bk = min(bq + 2 * ATT_SPAN, cls_len)
        nqb = cls_len // bq
        qpos0 = lax.broadcasted_iota(jnp.int32, (bq, bk), 0)
        kpos0 = lax.broadcasted_iota(jnp.int32, (bq, bk), 1)

        def body(it, _):
            cls = it // nqb
            p0 = (it % nqb) * bq
            ks = jnp.clip(p0 - ATT_SPAN, 0, cls_len - bk)
            if dil == 1:
                rq = pl.ds(p0, bq)
                rk = pl.ds(ks, bk)
            else:
                rq = pl.ds(cls + dil * p0, bq, stride=dil)
                rk = pl.ds(cls + dil * ks, bk, stride=dil)
            qb = qr_ref[rq, :].astype(BF16)
            kb = kr_ref[rk, :].astype(BF16)
            vb = v_ref[0, rk, :].astype(BF16)
            s = lax.dot_general(qb, kb, (((1,), (1,)), ((), ())), preferred_element_type=F32)
            band = jnp.abs((qpos0 + p0) - (kpos0 + ks)) <= ATT_SPAN
            s = jnp.where(band, s, NEG_INF)
            m = jnp.max(s, axis=-1, keepdims=True)
            p = jnp.exp(s - m)
            l = jnp.sum(p, axis=-1, keepdims=True)
            o = jnp.dot(p.astype(BF16), vb, preferred_element_type=F32) / l
            og_ref[gi, rq, :] = o
            lse_ref[gi, rq, :] = jnp.broadcast_to(m + jnp.log(l), (bq, ATT_HEAD_DIM))
            return 0

        lax.fori_loop(0, dil * nqb, body, 0)

    for gi, (_, dil) in enumerate(ATT_GROUPS):
        @pl.when(gid == gi)
        def _(gi=gi, dil=dil):
            group(gi, dil)

    @pl.when(gid == len(ATT_GROUPS) - 1)
    def _():
        def comb(i, _):
            sl = pl.ds(pl.multiple_of(i * rows, rows), rows)
            l0 = lse_ref[0, sl, :]
            l1 = lse_ref[1, sl, :]
            l2 = lse_ref[2, sl, :]
            mx = jnp.maximum(jnp.maximum(l0, l1), l2)
            w0 = jnp.exp(l0 - mx)
            w1 = jnp.exp(l1 - mx)
            w2 = jnp.exp(l2 - mx)
            num = w0 * og_ref[0, sl, :] + w1 * og_ref[1, sl, :] + w2 * og_ref[2, sl, :]
            o_ref[0, sl, :] = (num / (w0 + w1 + w2)).astype(o_ref.dtype)
            return 0

        lax.fori_loop(0, seq // rows, comb, 0)


def _attention(proj3, cos, sin):
    bsz, seq, _ = proj3.shape
    hpg = ATT_HEADS_PER_GROUP
    ng = len(ATT_GROUPS)
    blk = (1, seq, ATT_HEAD_DIM)
    return pl.pallas_call(
        functools.partial(_attn_kernel, seq=seq),
        grid=(bsz, hpg, ng),
        in_specs=[pl.BlockSpec(blk, lambda b, s, g: (b, 0, g * hpg + s)),
                  pl.BlockSpec(blk, lambda b, s, g: (b, 0, ATT_HEADS + g * hpg + s)),
                  pl.BlockSpec(blk, lambda b, s, g: (b, 0, 2 * ATT_HEADS + g * hpg + s)),
                  pl.BlockSpec((seq, ATT_HEAD_DIM), lambda b, s, g: (0, 0)),
                  pl.BlockSpec((seq, ATT_HEAD_DIM), lambda b, s, g: (0, 0))],
        out_specs=pl.BlockSpec(blk, lambda b, s, g: (b, 0, s)),
        out_shape=jax.ShapeDtypeStruct((bsz, seq, ATT_OUT), BF16),
        scratch_shapes=[pltpu.VMEM((seq, ATT_HEAD_DIM), F32),
                        pltpu.VMEM((seq, ATT_HEAD_DIM), F32),
                        pltpu.VMEM((ng, seq, ATT_HEAD_DIM), F32),
                        pltpu.VMEM((ng, seq, ATT_HEAD_DIM), F32)],
        compiler_params=_cparams(("parallel", "parallel", "arbitrary")),
    )(proj3, proj3, proj3, cos, sin)


def _rwkv_kernel(r_ref, v_ref, kk_ref, lw_ref, kd_ref, a_ref, y_ref, s_ref):
    cn = RW_CHUNK
    c2 = 2 * cn
    rev = pl.program_id(1) == 1
    sign = 1 - 2 * pl.program_id(1)

    @pl.when(pl.program_id(2) == 0)
    def _():
        s_ref[...] = jnp.zeros_like(s_ref)

    ti = lax.broadcasted_iota(jnp.int32, (cn, cn), 0)
    si = lax.broadcasted_iota(jnp.int32, (cn, cn), 1)
    tri = ((si - ti) * sign <= 0).astype(F32)
    lw_all = lw_ref[0, 0]
    g_all = jnp.dot(tri, lw_all, precision=lax.Precision.HIGHEST, preferred_element_type=F32)

    t2 = lax.broadcasted_iota(jnp.int32, (c2, c2), 0)
    s2 = lax.broadcasted_iota(jnp.int32, (c2, c2), 1)
    eye = (t2 == s2).astype(F32)
    order = (s2 % cn - t2 % cn) * sign
    strict = order < 0
    incl = order <= 0
    lane = lax.broadcasted_iota(jnp.int32, (1, LANES), 1)
    first = lane < RW_HEAD
    same_head = ((lax.broadcasted_iota(jnp.int32, (LANES, LANES), 0) // RW_HEAD)
                 == (lax.broadcasted_iota(jnp.int32, (LANES, LANES), 1) // RW_HEAD))

    def stack(x):
        return jnp.concatenate([jnp.where(first, x, 0.0), jnp.where(first, 0.0, x)], axis=0)

    def mm(a, b):
        return jnp.dot(a.astype(BF16), b.astype(BF16), preferred_element_type=F32)

    def mm_nt(a, b):
        return lax.dot_general(a.astype(BF16), b.astype(BF16), (((1,), (1,)), ((), ())),
                               preferred_element_type=F32)

    for hp in range(RW_WIDTH // LANES):
        ls = slice(hp * LANES, (hp + 1) * LANES)
        r = r_ref[0, :, ls]
        v = v_ref[0, :, ls]
        kk = kk_ref[0, :, ls]
        kd = kd_ref[0, 0, :, ls]
        a = a_ref[0, 0, :, ls]
        lw = lw_all[:, ls]
        g = g_all[:, ls]
        g_end = jnp.where(rev, g[0:1], g[cn - 1:cn])
        b = kk * a
        e_neg = jnp.exp(-g)
        e_end = jnp.exp(g_end - g)
        x = jnp.concatenate([stack(kk * jnp.exp(g - lw)), stack(r * jnp.exp(g))], axis=0)
        y = jnp.concatenate([stack(b * e_neg), stack(kd * e_neg)], axis=0)
        amat = mm_nt(x, y)
        tmat = jnp.where(strict, amat[:c2, :c2], 0.0)
        mkk = jnp.where(strict, amat[:c2, c2:], 0.0)
        arb = jnp.where(incl, amat[c2:, :c2], 0.0)
        ark = jnp.where(incl, amat[c2:, c2:], 0.0)
        ninv = eye - tmat
        pw = tmat
        for _ in range(cn.bit_length() - 2):
            pw = mm(pw, pw)
            ninv = ninv + mm(ninv, pw)
        s0 = s_ref[hp]
        xs = mm_nt(x, s0)
        mv = mm(jnp.concatenate([mkk, ark], axis=0), stack(v))
        us = mm(ninv, -xs[:c2] - mv[:c2])
        ys = xs[c2:] + mv[c2:] + mm(arb, us)
        y_ref[0, 0, :, ls] = ys[:cn] + ys[cn:]
        uv = jnp.concatenate([us[:cn] + us[cn:], v], axis=0)
        bk = jnp.concatenate([b * e_end, kd * e_end], axis=0)
        upd = mm(uv.T, bk)
        s_ref[hp] = jnp.where(same_head, s0 * jnp.exp(g_end) + upd, 0.0)


def _rwkv_scan(r, v, kk, lw, kd, a):
    bsz, seq, width = r.shape
    cn = RW_CHUNK
    nc = seq // cn

    def shared(b, z, c):
        return (b, c + z * (nc - 1 - 2 * c), 0)

    def per_dir(b, z, c):
        return (z, b, c + z * (nc - 1 - 2 * c), 0)

    sblk = pl.BlockSpec((1, cn, width), shared)
    dblk = pl.BlockSpec((1, 1, cn, width), per_dir)
    return pl.pallas_call(
        _rwkv_kernel,
        grid=(bsz, 2, nc),
        in_specs=[sblk, sblk, sblk, dblk, dblk, dblk],
        out_specs=dblk,
        out_shape=jax.ShapeDtypeStruct((2, bsz, seq, width), F32),
        scratch_shapes=[pltpu.VMEM((width // LANES, LANES, LANES), F32)],
        compiler_params=_cparams(("parallel", "parallel", "arbitrary")),
    )(r, v, kk, lw, kd, a)


def _mix_kernel(att_ref, rw_ref, ga_ref, gr_ref, x_ref, gt_ref, gain_ref, wa_ref, wr_ref, wo_ref, o_ref):
    pa = jnp.dot(att_ref[...], wa_ref[...], preferred_element_type=F32)
    pr = jnp.dot(rw_ref[...], wr_ref[...], preferred_element_type=F32)
    merged = jax.nn.sigmoid(ga_ref[...]) * pa + jax.nn.sigmoid(gr_ref[...]) * pr
    mix = jnp.dot(merged.astype(BF16), wo_ref[...], preferred_element_type=F32)
    o_ref[...] = x_ref[...] + gt_ref[0] * _rms(mix, gain_ref[...])


def _mix(att, rw, proj, x, gt, gain, wa, wr, wo, seq, tm=256):
    m, d = x.shape
    bpr = seq // tm
    gblk = GATE_OFF // d
    const = lambda i: (0, 0)
    return pl.pallas_call(
        _mix_kernel,
        grid=(m // tm,),
        in_specs=[pl.BlockSpec((tm, att.shape[1]), lambda i: (i, 0)),
                  pl.BlockSpec((tm, rw.shape[1]), lambda i: (i, 0)),
                  pl.BlockSpec((tm, d), lambda i: (i, gblk)),
                  pl.BlockSpec((tm, d), lambda i: (i, gblk + 1)),
                  pl.BlockSpec((tm, d), lambda i: (i, 0)),
                  pl.BlockSpec((1, 1, d), lambda i: (i // bpr, 0, 0)),
                  pl.BlockSpec((1, d), const),
                  pl.BlockSpec(wa.shape, const, pipeline_mode=pl.Buffered(1)),
                  pl.BlockSpec(wr.shape, const, pipeline_mode=pl.Buffered(1)),
                  pl.BlockSpec(wo.shape, const, pipeline_mode=pl.Buffered(1))],
        out_specs=pl.BlockSpec((tm, d), lambda i: (i, 0)),
        out_shape=jax.ShapeDtypeStruct((m, d), F32),
        compiler_params=_cparams(("parallel",)),
    )(att, rw, proj, proj, x, gt, gain, wa, wr, wo)


def _ffn_kernel(x_ref, xp_ref, xn_ref, gpre_ref, sc_ref, sh_ref, wg_ref, wu_ref, cw_ref, cb_ref, wd_ref,
                gt_ref, gpost_ref, o_ref, h_ref, acc_ref, *, tm, seq):
    i = pl.program_id(0)
    j = pl.program_id(1)
    halo = SUBLANES
    bpr = seq // tm

    @pl.when(j == 0)
    def _():
        gain = gpre_ref[...]
        sc = 1.0 + sc_ref[0]
        sh = sh_ref[0]
        has_prev = (i % bpr != 0).astype(F32)
        has_next = (i % bpr != bpr - 1).astype(F32)
        h_ref[0:halo, :] = ((_rms(xp_ref[...], gain) * sc + sh) * has_prev).astype(BF16)
        h_ref[halo:halo + tm, :] = (_rms(x_ref[...], gain) * sc + sh).astype(BF16)
        h_ref[halo + tm:, :] = ((_rms(xn_ref[...], gain) * sc + sh) * has_next).astype(BF16)
        acc_ref[...] = jnp.zeros_like(acc_ref)

    ext = tm + 2 * halo
    gate = jnp.dot(h_ref[...], wg_ref[...], preferred_element_type=F32)
    prev = pltpu.roll(gate, 1, 0)[halo:halo + tm]
    nxt = pltpu.roll(gate, ext - 1, 0)[halo:halo + tm]
    u = cw_ref[0:1, :] * prev + cw_ref[1:2, :] * gate[halo:halo + tm] + cw_ref[2:3, :] * nxt + cb_ref[...]
    up = jnp.dot(h_ref[halo:halo + tm, :], wu_ref[...], preferred_element_type=F32)
    act = jax.nn.gelu(u, approximate=True) * up
    acc_ref[...] += jnp.dot(act.astype(BF16), wd_ref[...], preferred_element_type=F32)

    @pl.when(j == pl.num_programs(1) - 1)
    def _():
        o_ref[...] = x_ref[...] + gt_ref[0] * _rms(acc_ref[...], gpost_ref[...])


def _ffn(x, gpre, sc, sh, wg, wu, cw, cb, wd, gt, gpost, seq, tm=512, tf=FF_TILE):
    m, d = x.shape
    f = wg.shape[1]
    bpr = seq // tm
    hb = tm // SUBLANES
    nhb = m // SUBLANES
    const = lambda i, j: (0, 0)
    bidx = lambda i, j: (i // bpr, 0, 0)
    return pl.pallas_call(
        functools.partial(_ffn_kernel, tm=tm, seq=seq),
        grid=(m // tm, f // tf),
        in_specs=[pl.BlockSpec((tm, d), lambda i, j: (i, 0)),
                  pl.BlockSpec((SUBLANES, d), lambda i, j: (jnp.maximum(i * hb - 1, 0), 0)),
                  pl.BlockSpec((SUBLANES, d), lambda i, j: (jnp.minimum((i + 1) * hb, nhb - 1), 0)),
                  pl.BlockSpec((1, d), const),
                  pl.BlockSpec((1, 1, d), bidx),
                  pl.BlockSpec((1, 1, d), bidx),
                  pl.BlockSpec((d, tf), lambda i, j: (0, j)),
                  pl.BlockSpec((d, tf), lambda i, j: (0, j)),
                  pl.BlockSpec((SUBLANES, tf), lambda i, j: (0, j)),
                  pl.BlockSpec((1, tf), lambda i, j: (0, j)),
                  pl.BlockSpec((tf, d), lambda i, j: (j, 0)),
                  pl.BlockSpec((1, 1, d), bidx),
                  pl.BlockSpec((1, d), const)],
        out_specs=pl.BlockSpec((tm, d), lambda i, j: (i, 0)),
        out_shape=jax.ShapeDtypeStruct((m, d), F32),
        scratch_shapes=[pltpu.VMEM((tm + 2 * SUBLANES, d), BF16),
                        pltpu.VMEM((tm, d), F32)],
        compiler_params=_cparams(("parallel", "arbitrary")),
    )(x, x, x, gpre, sc, sh, wg, wu, cw, cb, wd, gt, gpost)


def _shift_prev(x):
    return jnp.pad(x, ((0, 0), (1, 0), (0, 0)))[:, :-1]


def _shift_next(x):
    return jnp.pad(x, ((0, 0), (0, 1), (0, 0)))[:, 1:]


def _rwkv_branch(z, shift_mu, decay_base, decay_up, iclr_base, iclr_up, gate_up, k_k, k_a, r_k, lnx_w, lnx_b):
    bsz, seq, _ = z.shape
    z = z + shift_mu[0] * (_shift_prev(z) - z) + shift_mu[1] * (_shift_next(z) - z)
    c1, c2, c3 = RW_WIDTH, 2 * RW_WIDTH, 3 * RW_WIDTH
    c4 = c3 + 2 * DECAY_LORA
    c5 = c4 + 2 * ICLR_LORA
    r = z[..., :c1]
    k = z[..., c1:c2]
    v = z[..., c2:c3]
    wl = z[..., c3:c4].reshape(bsz, seq, 2, DECAY_LORA)
    al = z[..., c4:c5].reshape(bsz, seq, 2, ICLR_LORA)
    gl = z[..., c5:]
    hi = lax.Precision.HIGHEST
    w_raw = decay_base[:, None, None, :] + jnp.einsum('btzr,zrc->zbtc', jnp.tanh(wl), decay_up, precision=hi)
    lw = -jnp.exp(-jax.nn.softplus(-w_raw) - 0.5)
    a = jax.nn.sigmoid(iclr_base[:, None, None, :] + jnp.einsum('btzr,zrc->zbtc', al, iclr_up, precision=hi))
    g = jnp.dot(jax.nn.sigmoid(gl), gate_up, precision=hi)

    def heads(t):
        return t.reshape(t.shape[:-1] + (RW_HEADS, RW_HEAD))

    kk = heads(k * k_k)
    kk = (kk / jnp.maximum(jnp.sqrt(jnp.sum(kk * kk, axis=-1, keepdims=True)), 1e-12)).reshape(k.shape)
    kd = k[None] * (1.0 + (a - 1.0) * k_a)
    ys = _rwkv_scan(r, v, kk, lw, kd, a)
    y = heads(ys[0] + ys[1])
    mu = jnp.mean(y, axis=-1, keepdims=True)
    var = jnp.mean(jnp.square(y - mu), axis=-1, keepdims=True)
    y = ((y - mu) * lax.rsqrt(var + GN_EPS)).reshape(bsz, seq, RW_WIDTH) * lnx_w + lnx_b
    bonus = jnp.sum(jnp.sum(heads(r)[None] * heads(kd) * r_k, axis=-1, keepdims=True), axis=0)
    y = y + (bonus * heads(v)).reshape(bsz, seq, RW_WIDTH)
    return (y * g).astype(BF16)


def _rope_tables(seq):
    half = ATT_HEAD_DIM // 2
    inv_freq = 1.0 / (ROPE_THETA ** (jnp.arange(half, dtype=F32) / half))
    ang = jnp.arange(seq, dtype=F32)[:, None] * inv_freq[None, :]
    cos = jnp.cos(ang)
    sin = jnp.sin(ang)
    return jnp.concatenate([cos, cos], axis=-1), jnp.concatenate([-sin, sin], axis=-1)


def _trunk(x, ada, p):
    bsz, seq, d = x.shape
    m = bsz * seq
    sh1, sc1, gt1, sh2, sc2, gt2 = [t[:, None, :] for t in jnp.split(ada, 6, axis=-1)]
    x2 = x.reshape(m, d)
    proj = _inproj(x2, p['ln_mix_pre'], sc1, sh1, p['w_in'], seq)
    proj3 = proj.reshape(bsz, seq, N_IN_PAD)
    cos, sin = _rope_tables(seq)
    att = _attention(proj3, cos, sin)
    rw = _rwkv_branch(proj3[..., RW_OFF:RW_OFF + RW_COLS], p['shift_mu'], p['decay_base'], p['decay_up'],
                      p['iclr_base'], p['iclr_up'], p['gate_up'], p['k_k'], p['k_a'], p['r_k'],
                      p['lnx_w'], p['lnx_b'])
    x1 = _mix(att.reshape(m, ATT_OUT), rw.reshape(m, RW_WIDTH), proj, x2, gt1, p['ln_mix_post'],
              p['w_att_branch'], p['w_rwkv_branch'], p['w_out'], seq)
    y = _ffn(x1, p['ln_ffn_pre'], sc2, sh2, p['w_ffn_gate'], p['w_ffn_up'], p['ffn_conv_w'], p['ffn_conv_b'],
             p['w_ffn_down'], gt2, p['ln_ffn_post'], seq)
    return y.reshape(bsz, seq, d)


def kernel(x_prompt, x_sample, c_prompt, c_sample, ln_mix_pre, ln_mix_post, ln_ffn_pre, ln_ffn_post,
           w_ada, b_ada, w_in, shift_mu, decay_base, decay_up, iclr_base, iclr_up, gate_up,
           k_k, k_a, r_k, lnx_w, lnx_b, w_att_branch, w_rwkv_branch, w_out,
           w_ffn_gate, w_ffn_up, ffn_conv_w, ffn_conv_b, w_ffn_down):
    depth = w_in.shape[0]
    nb_p = c_prompt.shape[0]
    nb_s = c_sample.shape[0]
    c_all = jnp.concatenate([c_prompt, c_sample], axis=0)
    c_all = jnp.pad(c_all, ((0, (-c_all.shape[0]) % SUBLANES), (0, 0)))
    fpad = D_FF_PAD - D_FF
    xp, xs = x_prompt, x_sample
    for l in range(depth):
        rw_end = RW_OFF + RW_COLS
        w_in_l = jnp.concatenate([w_in[l][:, :rw_end], jnp.zeros((D_MODEL, RW_PAD), F32), w_in[l][:, rw_end:]],
                                 axis=1).astype(BF16)
        p = dict(
            ln_mix_pre=ln_mix_pre[l][None], ln_mix_post=ln_mix_post[l][None],
            ln_ffn_pre=ln_ffn_pre[l][None], ln_ffn_post=ln_ffn_post[l][None],
            w_in=w_in_l, shift_mu=shift_mu[l], decay_base=decay_base[l], decay_up=decay_up[l],
            iclr_base=iclr_base[l], iclr_up=iclr_up[l], gate_up=gate_up[l], k_k=k_k[l], k_a=k_a[l],
            r_k=r_k[l], lnx_w=lnx_w[l], lnx_b=lnx_b[l],
            w_att_branch=w_att_branch[l].astype(BF16), w_rwkv_branch=w_rwkv_branch[l].astype(BF16),
            w_out=w_out[l].astype(BF16),
            w_ffn_gate=jnp.pad(w_ffn_gate[l], ((0, 0), (0, fpad))).astype(BF16),
            w_ffn_up=jnp.pad(w_ffn_up[l], ((0, 0), (0, fpad))).astype(BF16),
            ffn_conv_w=jnp.pad(ffn_conv_w[l], ((0, SUBLANES - 3), (0, fpad))),
            ffn_conv_b=jnp.pad(ffn_conv_b[l], ((0, fpad),))[None],
            w_ffn_down=jnp.pad(w_ffn_down[l], ((0, fpad), (0, 0))).astype(BF16),
        )
        ada = _ada(c_all, w_ada[l], b_ada[l][None])
        xp = _trunk(xp, ada[:nb_p], p)
        xs = _trunk(xs, ada[nb_p:nb_p + nb_s], p)
    return (xp, xs)
```

```python
import functools

import jax
import jax.numpy as jnp
from jax import lax
from jax.experimental import pallas as pl
from jax.experimental.pallas import tpu as pltpu

F32 = jnp.float32
BF16 = jnp.bfloat16

D_MODEL = 2048
ATT_GROUPS = ((128, 1), (512, 4), (2048, 16))
ATT_HEADS_PER_GROUP = 4
ATT_HEAD_DIM = 128
ATT_HEADS = ATT_HEADS_PER_GROUP * len(ATT_GROUPS)
ATT_WIDTH = ATT_HEADS * ATT_HEAD_DIM
ATT_OUT = ATT_HEADS_PER_GROUP * ATT_HEAD_DIM
ROPE_THETA = 10000.0
RW_HEAD = 64
RW_WIDTH = D_MODEL // 2
RW_HEADS = RW_WIDTH // RW_HEAD
DECAY_LORA = 64
ICLR_LORA = 64
GATE_LORA = 160
RW_COLS = 3 * RW_WIDTH + 2 * DECAY_LORA + 2 * ICLR_LORA + GATE_LORA
N_IN = 3 * ATT_WIDTH + RW_COLS + 2 * D_MODEL
D_FF = ((8 * D_MODEL // 3 + 127) // 128) * 128
RMS_EPS = 1e-6
GN_EPS = 64e-5
NEG_INF = -1e30

LANES = 128
SUBLANES = 8
MXU_DIM_V7X = 256
VMEM_BYTES_V7X = 64 * 1024 * 1024
VMEM_LIMIT = VMEM_BYTES_V7X - 8 * 1024 * 1024

GATE_OFF = 0
RW_OFF = GATE_OFF + 2 * D_MODEL
LAT_OFF = RW_OFF + 3 * RW_WIDTH
LAT_COLS = RW_COLS - 3 * RW_WIDTH
LAT_PAD = 512
ATT_OFF = LAT_OFF + LAT_PAD
N_IN_PAD = ATT_OFF + 3 * ATT_WIDTH
RW_COLS_PAD = 3 * RW_WIDTH + LAT_PAD
FF_TILE = 512
D_FF_PAD = -(-D_FF // FF_TILE) * FF_TILE
RW_CHUNK = 64
ATT_SPAN = 64


def _cparams(sem):
    return pltpu.CompilerParams(dimension_semantics=sem, vmem_limit_bytes=VMEM_LIMIT)


def _rms(x, gain):
    return x * lax.rsqrt(jnp.mean(x * x, axis=-1, keepdims=True) + RMS_EPS) * gain


def _ada_kernel(c_ref, w_ref, b_ref, o_ref):
    c = c_ref[...]
    s = c * jax.nn.sigmoid(c)
    o_ref[...] = jnp.dot(s.astype(BF16), w_ref[...].astype(BF16),
                         preferred_element_type=F32) + b_ref[...]


def _ada(c, w, b, tn=1024):
    m, k = c.shape
    n = w.shape[1]
    return pl.pallas_call(
        _ada_kernel,
        name="ada",
        grid=(n // tn,),
        in_specs=[pl.BlockSpec((m, k), lambda j: (0, 0)),
                  pl.BlockSpec((k, tn), lambda j: (0, j)),
                  pl.BlockSpec((1, tn), lambda j: (0, j))],
        out_specs=pl.BlockSpec((m, tn), lambda j: (0, j)),
        out_shape=jax.ShapeDtypeStruct((m, n), F32),
        compiler_params=_cparams(("parallel",)),
    )(c, w, b)


def _inproj_kernel(x_ref, g_ref, sc_ref, sh_ref, w_ref, o_ref, h_ref):
    @pl.when(pl.program_id(1) == 0)
    def _():
        h = _rms(x_ref[...], g_ref[...]) * (1.0 + sc_ref[0]) + sh_ref[0]
        h_ref[...] = h.astype(BF16)

    o_ref[...] = jnp.dot(h_ref[...], w_ref[...], preferred_element_type=F32)


def _inproj(x, gain, sc, sh, w, seq, tm=512, tn=1536):
    m, k = x.shape
    n = w.shape[1]
    bpr = seq // tm
    return pl.pallas_call(
        _inproj_kernel,
        name="inproj",
        grid=(m // tm, n // tn),
        in_specs=[pl.BlockSpec((tm, k), lambda i, j: (i, 0)),
                  pl.BlockSpec((1, k), lambda i, j: (0, 0)),
                  pl.BlockSpec((1, 1, k), lambda i, j: (i // bpr, 0, 0)),
                  pl.BlockSpec((1, 1, k), lambda i, j: (i // bpr, 0, 0)),
                  pl.BlockSpec((k, tn), lambda i, j: (0, j))],
        out_specs=pl.BlockSpec((tm, tn), lambda i, j: (i, j)),
        out_shape=jax.ShapeDtypeStruct((m, n), F32),
        scratch_shapes=[pltpu.VMEM((tm, k), BF16)],
        compiler_params=_cparams(("parallel", "arbitrary")),
    )(x, gain, sc, sh, w)


def _attn_kernel(q_ref, k_ref, v_ref, cos_ref, sin_ref, o_ref, qr_ref, kr_ref, og_ref, lse_ref, *, seq):
    gid = pl.program_id(2)
    rows = 512
    half = ATT_HEAD_DIM // 2

    def rot(i, _):
        sl = pl.ds(pl.multiple_of(i * rows, rows), rows)
        cos = cos_ref[sl, :]
        sin = sin_ref[sl, :]
        q = q_ref[0, sl, :]
        k = k_ref[0, sl, :]
        qr_ref[sl, :] = (q * cos + pltpu.roll(q, half, 1) * sin) * (ATT_HEAD_DIM ** -0.5)
        kr_ref[sl, :] = k * cos + pltpu.roll(k, half, 1) * sin
        return 0

    lax.fori_loop(0, seq // rows, rot, 0)

    def group(gi, dil):
        cls_len = seq // dil
        bq = min(128, cls_len)
        bk = min(bq + 2 * ATT_SPAN, cls_len)
        nqb = cls_len // bq
        qpos0 = lax.broadcasted_iota(jnp.int32, (bq, bk), 0)
        kpos0 = lax.broadcasted_iota(jnp.int32, (bq, bk), 1)

        def body(it, _):
            cls = it // nqb
            p0 = (it % nqb) * bq
            ks = jnp.clip(p0 - ATT_SPAN, 0, cls_len - bk)
            if dil == 1:
                rq = pl.ds(p0, bq)
                rk = pl.ds(ks, bk)
            else:
                rq = pl.ds(cls + dil * p0, bq, stride=dil)
                rk = pl.ds(cls + dil * ks, bk, stride=dil)
            qb = qr_ref[rq, :].astype(BF16)
            kb = kr_ref[rk, :].astype(BF16)
            vb = v_ref[0, rk, :].astype(BF16)
            s = lax.dot_general(qb, kb, (((1,), (1,)), ((), ())), preferred_element_type=F32)
            band = jnp.abs((qpos0 + p0) - (kpos0 + ks)) <= ATT_SPAN
            s = jnp.where(band, s, NEG_INF)
            m = jnp.max(s, axis=-1, keepdims=True)
            p = jnp.exp(s - m)
            l = jnp.sum(p, axis=-1, keepdims=True)
            o = jnp.dot(p.astype(BF16), vb, preferred_element_type=F32) / l
            og_ref[gi, rq, :] = o
            lse_ref[gi, rq, :] = jnp.broadcast_to(m + jnp.log(l), (bq, ATT_HEAD_DIM))
            return 0

        lax.fori_loop(0, dil * nqb, body, 0, unroll=4)

    for gi, (_, dil) in enumerate(ATT_GROUPS):
        @pl.when(gid == gi)
        def _(gi=gi, dil=dil):
            group(gi, dil)

    @pl.when(gid == len(ATT_GROUPS) - 1)
    def _():
        def comb(i, _):
            sl = pl.ds(pl.multiple_of(i * rows, rows), rows)
            l0 = lse_ref[0, sl, :]
            l1 = lse_ref[1, sl, :]
            l2 = lse_ref[2, sl, :]
            mx = jnp.maximum(jnp.maximum(l0, l1), l2)
            w0 = jnp.exp(l0 - mx)
            w1 = jnp.exp(l1 - mx)
            w2 = jnp.exp(l2 - mx)
            num = w0 * og_ref[0, sl, :] + w1 * og_ref[1, sl, :] + w2 * og_ref[2, sl, :]
            o_ref[0, sl, :] = (num / (w0 + w1 + w2)).astype(o_ref.dtype)
            return 0

        lax.fori_loop(0, seq // rows, comb, 0)


def _attention(proj3, cos, sin):
    bsz, seq, _ = proj3.shape
    hpg = ATT_HEADS_PER_GROUP
    ng = len(ATT_GROUPS)
    blk = (1, seq, ATT_HEAD_DIM)
    q0 = ATT_OFF // ATT_HEAD_DIM
    return pl.pallas_call(
        functools.partial(_attn_kernel, seq=seq),
        name="attn",
        grid=(bsz, hpg, ng),
        in_specs=[pl.BlockSpec(blk, lambda b, s, g: (b, 0, q0 + g * hpg + s)),
                  pl.BlockSpec(blk, lambda b, s, g: (b, 0, q0 + ATT_HEADS + g * hpg + s)),
                  pl.BlockSpec(blk, lambda b, s, g: (b, 0, q0 + 2 * ATT_HEADS + g * hpg + s)),
                  pl.BlockSpec((seq, ATT_HEAD_DIM), lambda b, s, g: (0, 0)),
                  pl.BlockSpec((seq, ATT_HEAD_DIM), lambda b, s, g: (0, 0))],
        out_specs=pl.BlockSpec(blk, lambda b, s, g: (b, 0, s)),
        out_shape=jax.ShapeDtypeStruct((bsz, seq, ATT_OUT), BF16),
        scratch_shapes=[pltpu.VMEM((seq, ATT_HEAD_DIM), F32),
                        pltpu.VMEM((seq, ATT_HEAD_DIM), F32),
                        pltpu.VMEM((ng, seq, ATT_HEAD_DIM), F32),
                        pltpu.VMEM((ng, seq, ATT_HEAD_DIM), F32)],
        compiler_params=_cparams(("parallel", "parallel", "arbitrary")),
    )(proj3, proj3, proj3, cos, sin)


def _rwkv_kernel(r_ref, v_ref, kk_ref, lw_ref, kd_ref, b_ref, y_ref, s_ref):
    cn = RW_CHUNK
    c2 = 2 * cn
    rev = pl.program_id(1) == 1
    sign = 1 - 2 * pl.program_id(1)

    @pl.when(pl.program_id(2) == 0)
    def _():
        s_ref[...] = jnp.zeros_like(s_ref)

    ti = lax.broadcasted_iota(jnp.int32, (cn, cn), 0)
    si = lax.broadcasted_iota(jnp.int32, (cn, cn), 1)
    tri = ((si - ti) * sign <= 0).astype(F32)
    lw_all = lw_ref[0, 0]
    g_all = jnp.dot(tri, lw_all, precision=lax.Precision.HIGHEST, preferred_element_type=F32)

    t2 = lax.broadcasted_iota(jnp.int32, (c2, c2), 0)
    s2 = lax.broadcasted_iota(jnp.int32, (c2, c2), 1)
    eye = (t2 == s2).astype(F32)
    order = (s2 % cn - t2 % cn) * sign
    strict = order < 0
    incl = order <= 0
    lane = lax.broadcasted_iota(jnp.int32, (1, LANES), 1)
    first = lane < RW_HEAD
    same_head = ((lax.broadcasted_iota(jnp.int32, (LANES, LANES), 0) // RW_HEAD)
                 == (lax.broadcasted_iota(jnp.int32, (LANES, LANES), 1) // RW_HEAD))

    def stack(x):
        return jnp.concatenate([jnp.where(first, x, 0.0), jnp.where(first, 0.0, x)], axis=0)

    def mm(a, b):
        return jnp.dot(a.astype(BF16), b.astype(BF16), preferred_element_type=F32)

    def mm_nt(a, b):
        return lax.dot_general(a.astype(BF16), b.astype(BF16), (((1,), (1,)), ((), ())),
                               preferred_element_type=F32)

    pairs = range(RW_WIDTH // LANES)
    lanes = [slice(hp * LANES, (hp + 1) * LANES) for hp in pairs]
    xq, vs, bk, decay_end, amat = [], [], [], [], []
    for ls in lanes:
        r = r_ref[0, :, ls]
        v = v_ref[0, :, ls]
        kk = kk_ref[0, :, ls]
        kd = kd_ref[0, 0, :, ls]
        b = b_ref[0, 0, :, ls]
        g = g_all[:, ls]
        g_end = jnp.where(rev, g[0:1], g[cn - 1:cn])
        e_neg = jnp.exp(-g)
        e_end = jnp.exp(g_end - g)
        x = jnp.concatenate([stack(kk * jnp.exp(g - lw_all[:, ls])), stack(r * jnp.exp(g))],
                            axis=0).astype(BF16)
        y = jnp.concatenate([stack(b * e_neg), stack(kd * e_neg)], axis=0).astype(BF16)
        xq.append(x)
        vs.append(stack(v).astype(BF16))
        bk.append(jnp.concatenate([b * e_end, kd * e_end], axis=0).astype(BF16))
        decay_end.append(jnp.exp(g_end))
        amat.append(lax.dot_general(x, y, (((1,), (1,)), ((), ())), preferred_element_type=F32))
    tmat = [jnp.where(strict, am[:c2, :c2], 0.0) for am in amat]
    rest = [jnp.concatenate([jnp.where(strict, am[:c2, c2:], 0.0), jnp.where(incl, am[c2:, c2:], 0.0)],
                            axis=0).astype(BF16) for am in amat]
    arb = [jnp.where(incl, am[c2:, :c2], 0.0).astype(BF16) for am in amat]
    ninv = [eye - t for t in tmat]
    pw = [t.astype(BF16) for t in tmat]
    pw = [jnp.dot(p, p, preferred_element_type=F32).astype(BF16) for p in pw]
    nsq = cn.bit_length() - 2
    for it in range(nsq):
        if it < nsq - 1:
            both = [jnp.dot(p, jnp.concatenate([p, n.astype(BF16)], axis=1), preferred_element_type=F32)
                    for p, n in zip(pw, ninv)]
            pw = [bo[:, :c2].astype(BF16) for bo in both]
            ninv = [n + bo[:, c2:] for n, bo in zip(ninv, both)]
        else:
            ninv = [n + jnp.dot(p, n.astype(BF16), preferred_element_type=F32) for p, n in zip(pw, ninv)]
    mv = [jnp.dot(m, v, preferred_element_type=F32) for m, v in zip(rest, vs)]
    s0 = [s_ref[hp] for hp in pairs]
    xs = [mm_nt(x, s) for x, s in zip(xq, s0)]
    us = [mm(n, -x[:c2] - m[:c2]) for n, x, m in zip(ninv, xs, mv)]
    ys = [x[c2:] + m[c2:] + mm(ab, u) for x, m, ab, u in zip(xs, mv, arb, us)]
    for hp in pairs:
        y_ref[0, 0, :, lanes[hp]] = ys[hp][:cn] + ys[hp][cn:]
        uv = jnp.concatenate([us[hp][:cn] + us[hp][cn:], v_ref[0, :, lanes[hp]]], axis=0)
        upd = mm(uv.T, bk[hp])
        s_ref[hp] = jnp.where(same_head, s0[hp] * decay_end[hp] + upd, 0.0)


def _rwkv_scan(r, v, kk, lw, kd, b):
    bsz, seq, width = r.shape
    cn = RW_CHUNK
    nc = seq // cn

    def shared(b, z, c):
        return (b, c + z * (nc - 1 - 2 * c), 0)

    def per_dir(b, z, c):
        return (z, b, c + z * (nc - 1 - 2 * c), 0)

    sblk = pl.BlockSpec((1, cn, width), shared)
    dblk = pl.BlockSpec((1, 1, cn, width), per_dir)
    return pl.pallas_call(
        _rwkv_kernel,
        name="rwkv",
        grid=(bsz, 2, nc),
        in_specs=[sblk, sblk, sblk, dblk, dblk, dblk],
        out_specs=dblk,
        out_shape=jax.ShapeDtypeStruct((2, bsz, seq, width), F32),
        scratch_shapes=[pltpu.VMEM((width // LANES, LANES, LANES), F32)],
        compiler_params=_cparams(("parallel", "parallel", "arbitrary")),
    )(r, v, kk, lw, kd, b)


def _head_sum(x, ones, passes):
    gw = ones.shape[0]
    cols = []
    for c0 in range(0, x.shape[1], gw):
        xc = x[:, c0:c0 + gw]
        hi = xc.astype(BF16)
        out = jnp.dot(hi, ones, preferred_element_type=F32)
        if passes == 2:
            lo = (xc - hi.astype(F32)).astype(BF16)
            out = out + jnp.dot(lo, ones, preferred_element_type=F32)
        cols.append(out)
    return jnp.concatenate(cols, axis=1)


def _prep_kernel(r_ref, k_ref, v_ref, l_ref, rp_ref, kp_ref, vp_ref, lp_ref, rn_ref, kn_ref, vn_ref, ln_ref,
                 mu_ref, dbase_ref, dup_ref, ibase_ref, iup_ref, gup_ref, kkw_ref, ka_ref, rk_ref, ones_ref,
                 ro_ref, vo_ref, kko_ref, go_ref, bvo_ref, lwo_ref, kdo_ref, bo_ref, *, tm, seq):
    i = pl.program_id(0)
    bpr = seq // tm
    has_prev = (i % bpr != 0).astype(F32)
    has_next = (i % bpr != bpr - 1).astype(F32)
    rowi = lax.broadcasted_iota(jnp.int32, (tm, 1), 0)
    hi = lax.Precision.HIGHEST
    cw = RW_WIDTH

    def shifted(main_ref, prev_ref, next_ref, c0, c1):
        z = main_ref[...]
        zp = jnp.where(rowi == 0, prev_ref[SUBLANES - 1:SUBLANES, :] * has_prev, pltpu.roll(z, 1, 0))
        zn = jnp.where(rowi == tm - 1, next_ref[0:1, :] * has_next, pltpu.roll(z, tm - 1, 0))
        return z + mu_ref[0:1, c0:c1] * (zp - z) + mu_ref[1:2, c0:c1] * (zn - z)

    r = shifted(r_ref, rp_ref, rn_ref, 0, cw)
    k = shifted(k_ref, kp_ref, kn_ref, cw, 2 * cw)
    v = shifted(v_ref, vp_ref, vn_ref, 2 * cw, 3 * cw)
    lat = shifted(l_ref, lp_ref, ln_ref, 3 * cw, 3 * cw + LAT_PAD)
    nd, ni = 2 * DECAY_LORA, 2 * ICLR_LORA
    w_raw = dbase_ref[...] + jnp.dot(jnp.tanh(lat[:, :nd]), dup_ref[...], precision=hi,
                                     preferred_element_type=F32)
    neg = -w_raw
    softplus = jnp.maximum(neg, 0.0) + jnp.log(1.0 + jnp.exp(-jnp.abs(neg)))
    lw = -jnp.exp(-softplus - 0.5)
    a = jax.nn.sigmoid(ibase_ref[...] + jnp.dot(lat[:, nd:nd + ni].astype(BF16), iup_ref[...],
                                                preferred_element_type=F32))
    g = jnp.dot(jax.nn.sigmoid(lat[:, nd + ni:]).astype(BF16), gup_ref[...], preferred_element_type=F32)
    ones = ones_ref[...]
    kkr = k * kkw_ref[...]
    kk = kkr / jnp.maximum(jnp.sqrt(_head_sum(kkr * kkr, ones, 2)), 1e-12)
    ka = ka_ref[...]
    kd0 = k * (1.0 + (a[:, :cw] - 1.0) * ka)
    kd1 = k * (1.0 + (a[:, cw:] - 1.0) * ka)
    bonus = _head_sum(r * (kd0 + kd1) * rk_ref[...], ones, 2)
    ro_ref[...] = r
    vo_ref[...] = v
    kko_ref[...] = kk
    go_ref[...] = g
    bvo_ref[...] = bonus * v
    lwo_ref[0] = lw[:, :cw]
    lwo_ref[1] = lw[:, cw:]
    kdo_ref[0] = kd0
    kdo_ref[1] = kd1
    bo_ref[0] = kk * a[:, :cw]
    bo_ref[1] = kk * a[:, cw:]


def _prep(proj, mu, dbase, dup, ibase, iup, gup, kkw, ka, rk, ones, seq, tm=256):
    m = proj.shape[0]
    cw = RW_WIDTH
    bpr = seq // tm
    hb = tm // SUBLANES
    nhb = m // SUBLANES
    r0 = RW_OFF // cw
    l0 = LAT_OFF // LAT_PAD
    cols = [(cw, r0), (cw, r0 + 1), (cw, r0 + 2), (LAT_PAD, l0)]
    main = [pl.BlockSpec((tm, w), lambda i, c=c: (i, c)) for w, c in cols]
    prev = [pl.BlockSpec((SUBLANES, w), lambda i, c=c: (jnp.maximum(i * hb - 1, 0), c)) for w, c in cols]
    nxt = [pl.BlockSpec((SUBLANES, w), lambda i, c=c: (jnp.minimum((i + 1) * hb, nhb - 1), c)) for w, c in cols]
    params = [mu, dbase, dup, ibase, iup, gup, kkw, ka, rk, ones]
    pspecs = [pl.BlockSpec(t.shape, lambda i: (0, 0)) for t in params]
    oblk = pl.BlockSpec((tm, cw), lambda i: (i, 0))
    dblk = pl.BlockSpec((2, tm, cw), lambda i: (0, i, 0))
    one = jax.ShapeDtypeStruct((m, cw), F32)
    two = jax.ShapeDtypeStruct((2, m, cw), F32)
    return pl.pallas_call(
        functools.partial(_prep_kernel, tm=tm, seq=seq),
        name="rwkv_prep",
        grid=(m // tm,),
        in_specs=main + prev + nxt + pspecs,
        out_specs=[oblk] * 5 + [dblk] * 3,
        out_shape=[one] * 5 + [two] * 3,
        compiler_params=_cparams(("parallel",)),
    )(*([proj] * 12), *params)


def _mix_kernel(att_ref, y_ref, bv_ref, g_ref, lnw_ref, lnb_ref, ones_ref, ga_ref, gr_ref, x_ref, gt_ref,
                gain_ref, wa_ref, wr_ref, wo_ref, o_ref):
    ones = ones_ref[...]
    y = y_ref[0] + y_ref[1]
    mu = _head_sum(y, ones, 1) * (1.0 / RW_HEAD)
    yc = y - mu
    var = _head_sum(yc * yc, ones, 1) * (1.0 / RW_HEAD)
    rw = ((yc * lax.rsqrt(var + GN_EPS)) * lnw_ref[...] + lnb_ref[...] + bv_ref[...]) * g_ref[...]
    pa = jnp.dot(att_ref[...], wa_ref[...], preferred_element_type=F32)
    pr = jnp.dot(rw.astype(BF16), wr_ref[...], preferred_element_type=F32)
    merged = jax.nn.sigmoid(ga_ref[...]) * pa + jax.nn.sigmoid(gr_ref[...]) * pr
    mix = jnp.dot(merged.astype(BF16), wo_ref[...], preferred_element_type=F32)
    o_ref[...] = x_ref[...] + gt_ref[0] * _rms(mix, gain_ref[...])


def _mix(att, y, bv, g, lnw, lnb, ones, proj, x, gt, gain, wa, wr, wo, seq, tm=256):
    m, d = x.shape
    cw = RW_WIDTH
    bpr = seq // tm
    gblk = GATE_OFF // d
    const = lambda i: (0, 0)
    resident = lambda t: pl.BlockSpec(t.shape, const, pipeline_mode=pl.Buffered(1))
    return pl.pallas_call(
        _mix_kernel,
        name="mix",
        grid=(m // tm,),
        in_specs=[pl.BlockSpec((tm, att.shape[1]), lambda i: (i, 0)),
                  pl.BlockSpec((2, tm, cw), lambda i: (0, i, 0)),
                  pl.BlockSpec((tm, cw), lambda i: (i, 0)),
                  pl.BlockSpec((tm, cw), lambda i: (i, 0)),
                  pl.BlockSpec((1, cw), const),
                  pl.BlockSpec((1, cw), const),
                  resident(ones),
                  pl.BlockSpec((tm, d), lambda i: (i, gblk)),
                  pl.BlockSpec((tm, d), lambda i: (i, gblk + 1)),
                  pl.BlockSpec((tm, d), lambda i: (i, 0)),
                  pl.BlockSpec((1, 1, d), lambda i: (i // bpr, 0, 0)),
                  pl.BlockSpec((1, d), const),
                  resident(wa), resident(wr), resident(wo)],
        out_specs=pl.BlockSpec((tm, d), lambda i: (i, 0)),
        out_shape=jax.ShapeDtypeStruct((m, d), F32),
        compiler_params=_cparams(("parallel",)),
    )(att, y, bv, g, lnw, lnb, ones, proj, proj, x, gt, gain, wa, wr, wo)


def _ffn_kernel(x_ref, xp_ref, xn_ref, gpre_ref, sc_ref, sh_ref, wg_ref, wu_ref, cw_ref, cb_ref, wd_ref,
                gt_ref, gpost_ref, o_ref, h_ref, acc_ref, *, tm, seq):
    i = pl.program_id(0)
    j = pl.program_id(1)
    halo = SUBLANES
    bpr = seq // tm

    @pl.when(j == 0)
    def _():
        gain = gpre_ref[...]
        sc = 1.0 + sc_ref[0]
        sh = sh_ref[0]
        has_prev = (i % bpr != 0).astype(F32)
        has_next = (i % bpr != bpr - 1).astype(F32)
        h_ref[0:halo, :] = ((_rms(xp_ref[...], gain) * sc + sh) * has_prev).astype(BF16)
        h_ref[halo:halo + tm, :] = (_rms(x_ref[...], gain) * sc + sh).astype(BF16)
        h_ref[halo + tm:, :] = ((_rms(xn_ref[...], gain) * sc + sh) * has_next).astype(BF16)
        acc_ref[...] = jnp.zeros_like(acc_ref)

    ext = tm + 2 * halo
    gate = jnp.dot(h_ref[...], wg_ref[...], preferred_element_type=F32)
    prev = pltpu.roll(gate, 1, 0)[halo:halo + tm]
    nxt = pltpu.roll(gate, ext - 1, 0)[halo:halo + tm]
    u = cw_ref[0:1, :] * prev + cw_ref[1:2, :] * gate[halo:halo + tm] + cw_ref[2:3, :] * nxt + cb_ref[...]
    up = jnp.dot(h_ref[halo:halo + tm, :], wu_ref[...], preferred_element_type=F32)
    act = jax.nn.gelu(u, approximate=True) * up
    acc_ref[...] += jnp.dot(act.astype(BF16), wd_ref[...], preferred_element_type=F32)

    @pl.when(j == pl.num_programs(1) - 1)
    def _():
        o_ref[...] = x_ref[...] + gt_ref[0] * _rms(acc_ref[...], gpost_ref[...])


def _ffn(x, gpre, sc, sh, wg, wu, cw, cb, wd, gt, gpost, seq, tm=512, tf=FF_TILE):
    m, d = x.shape
    f = wg.shape[1]
    bpr = seq // tm
    hb = tm // SUBLANES
    nhb = m // SUBLANES
    const = lambda i, j: (0, 0)
    bidx = lambda i, j: (i // bpr, 0, 0)
    return pl.pallas_call(
        functools.partial(_ffn_kernel, tm=tm, seq=seq),
        name="ffn",
        grid=(m // tm, f // tf),
        in_specs=[pl.BlockSpec((tm, d), lambda i, j: (i, 0)),
                  pl.BlockSpec((SUBLANES, d), lambda i, j: (jnp.maximum(i * hb - 1, 0), 0)),
                  pl.BlockSpec((SUBLANES, d), lambda i, j: (jnp.minimum((i + 1) * hb, nhb - 1), 0)),
                  pl.BlockSpec((1, d), const),
                  pl.BlockSpec((1, 1, d), bidx),
                  pl.BlockSpec((1, 1, d), bidx),
                  pl.BlockSpec((d, tf), lambda i, j: (0, j)),
                  pl.BlockSpec((d, tf), lambda i, j: (0, j)),
                  pl.BlockSpec((SUBLANES, tf), lambda i, j: (0, j)),
                  pl.BlockSpec((1, tf), lambda i, j: (0, j)),
                  pl.BlockSpec((tf, d), lambda i, j: (j, 0)),
                  pl.BlockSpec((1, 1, d), bidx),
                  pl.BlockSpec((1, d), const)],
        out_specs=pl.BlockSpec((tm, d), lambda i, j: (i, 0)),
        out_shape=jax.ShapeDtypeStruct((m, d), F32),
        scratch_shapes=[pltpu.VMEM((tm + 2 * SUBLANES, d), BF16),
                        pltpu.VMEM((tm, d), F32)],
        compiler_params=_cparams(("parallel", "arbitrary")),
    )(x, x, x, gpre, sc, sh, wg, wu, cw, cb, wd, gt, gpost)


def _rope_tables(seq):
    half = ATT_HEAD_DIM // 2
    inv_freq = 1.0 / (ROPE_THETA ** (jnp.arange(half, dtype=F32) / half))
    ang = jnp.arange(seq, dtype=F32)[:, None] * inv_freq[None, :]
    cos = jnp.cos(ang)
    sin = jnp.sin(ang)
    return jnp.concatenate([cos, cos], axis=-1), jnp.concatenate([-sin, sin], axis=-1)


def _trunk(x, ada, p):
    bsz, seq, d = x.shape
    m = bsz * seq
    sh1, sc1, gt1, sh2, sc2, gt2 = [t[:, None, :] for t in jnp.split(ada, 6, axis=-1)]
    x2 = x.reshape(m, d)
    proj = _inproj(x2, p['ln_mix_pre'], sc1, sh1, p['w_in'], seq)
    proj3 = proj.reshape(bsz, seq, N_IN_PAD)
    cos, sin = _rope_tables(seq)
    att = _attention(proj3, cos, sin)
    r, v, kk, g, bv, lw, kd, b = _prep(proj, p['shift_mu'], p['decay_base'], p['decay_up'], p['iclr_base'],
                                       p['iclr_up'], p['gate_up'], p['k_k'], p['k_a'], p['r_k'], p['head_ones'], seq)
    per_seq = lambda t: t.reshape(t.shape[:-2] + (bsz, seq, RW_WIDTH))
    ys = _rwkv_scan(per_seq(r), per_seq(v), per_seq(kk), per_seq(lw), per_seq(kd), per_seq(b))
    x1 = _mix(att.reshape(m, ATT_OUT), ys.reshape(2, m, RW_WIDTH), bv, g, p['lnx_w'], p['lnx_b'], p['head_ones'],
              proj, x2, gt1, p['ln_mix_post'], p['w_att_branch'], p['w_rwkv_branch'], p['w_out'], seq)
    y = _ffn(x1, p['ln_ffn_pre'], sc2, sh2, p['w_ffn_gate'], p['w_ffn_up'], p['ffn_conv_w'], p['ffn_conv_b'],
             p['w_ffn_down'], gt2, p['ln_ffn_post'], seq)
    return y.reshape(bsz, seq, d)


def kernel(x_prompt, x_sample, c_prompt, c_sample, ln_mix_pre, ln_mix_post, ln_ffn_pre, ln_ffn_post,
           w_ada, b_ada, w_in, shift_mu, decay_base, decay_up, iclr_base, iclr_up, gate_up,
           k_k, k_a, r_k, lnx_w, lnx_b, w_att_branch, w_rwkv_branch, w_out,
           w_ffn_gate, w_ffn_up, ffn_conv_w, ffn_conv_b, w_ffn_down):
    depth = w_in.shape[0]
    nb_p = c_prompt.shape[0]
    nb_s = c_sample.shape[0]
    c_all = jnp.concatenate([c_prompt, c_sample], axis=0)
    c_all = jnp.pad(c_all, ((0, (-c_all.shape[0]) % SUBLANES), (0, 0)))
    fpad = D_FF_PAD - D_FF
    xp, xs = x_prompt, x_sample
    lane_head = jnp.arange(MXU_DIM_V7X) // RW_HEAD
    head_ones = (lane_head[:, None] == lane_head[None, :]).astype(BF16)
    zdl = jnp.zeros((DECAY_LORA, RW_WIDTH), F32)
    zil = jnp.zeros((ICLR_LORA, RW_WIDTH), F32)
    for l in range(depth):
        att_end = 3 * ATT_WIDTH
        rw_end = att_end + RW_COLS
        w_in_l = jnp.concatenate([w_in[l][:, rw_end:], w_in[l][:, att_end:rw_end],
                                  jnp.zeros((D_MODEL, LAT_PAD - LAT_COLS), F32), w_in[l][:, :att_end]],
                                 axis=1).astype(BF16)
        p = dict(
            ln_mix_pre=ln_mix_pre[l][None], ln_mix_post=ln_mix_post[l][None],
            ln_ffn_pre=ln_ffn_pre[l][None], ln_ffn_post=ln_ffn_post[l][None],
            w_in=w_in_l, head_ones=head_ones,
            shift_mu=jnp.pad(shift_mu[l], ((0, 0), (0, RW_COLS_PAD - RW_COLS))),
            decay_base=decay_base[l].reshape(1, 2 * RW_WIDTH),
            decay_up=jnp.block([[decay_up[l][0], zdl], [zdl, decay_up[l][1]]]),
            iclr_base=iclr_base[l].reshape(1, 2 * RW_WIDTH),
            iclr_up=jnp.block([[iclr_up[l][0], zil], [zil, iclr_up[l][1]]]).astype(BF16),
            gate_up=jnp.pad(gate_up[l], ((0, LAT_PAD - LAT_COLS), (0, 0))).astype(BF16),
            k_k=k_k[l][None], k_a=k_a[l][None], r_k=r_k[l].reshape(1, RW_WIDTH),
            lnx_w=lnx_w[l][None], lnx_b=lnx_b[l][None],
            w_att_branch=w_att_branch[l].astype(BF16), w_rwkv_branch=w_rwkv_branch[l].astype(BF16),
            w_out=w_out[l].astype(BF16),
            w_ffn_gate=jnp.pad(w_ffn_gate[l], ((0, 0), (0, fpad))).astype(BF16),
            w_ffn_up=jnp.pad(w_ffn_up[l], ((0, 0), (0, fpad))).astype(BF16),
            ffn_conv_w=jnp.pad(ffn_conv_w[l], ((0, SUBLANES - 3), (0, fpad))),
            ffn_conv_b=jnp.pad(ffn_conv_b[l], ((0, fpad),))[None],
            w_ffn_down=jnp.pad(w_ffn_down[l], ((0, fpad), (0, 0))).astype(BF16),
        )
        ada = _ada(c_all, w_ada[l], b_ada[l][None])
        xp = _trunk(xp, ada[:nb_p], p)
        xs = _trunk(xs, ada[nb_p:nb_p + nb_s], p)
    return (xp, xs)
```

```python
import functools

import jax
import jax.numpy as jnp
from jax import lax
from jax.experimental import pallas as pl
from jax.experimental.pallas import tpu as pltpu

F32 = jnp.float32
BF16 = jnp.bfloat16

D_MODEL = 2048
ATT_GROUPS = ((128, 1), (512, 4), (2048, 16))
ATT_HEADS_PER_GROUP = 4
ATT_HEAD_DIM = 128
ATT_HEADS = ATT_HEADS_PER_GROUP * len(ATT_GROUPS)
ATT_WIDTH = ATT_HEADS * ATT_HEAD_DIM
ATT_OUT = ATT_HEADS_PER_GROUP * ATT_HEAD_DIM
ROPE_THETA = 10000.0
RW_HEAD = 64
RW_WIDTH = D_MODEL // 2
RW_HEADS = RW_WIDTH // RW_HEAD
DECAY_LORA = 64
ICLR_LORA = 64
GATE_LORA = 160
RW_COLS = 3 * RW_WIDTH + 2 * DECAY_LORA + 2 * ICLR_LORA + GATE_LORA
N_IN = 3 * ATT_WIDTH + RW_COLS + 2 * D_MODEL
D_FF = ((8 * D_MODEL // 3 + 127) // 128) * 128
RMS_EPS = 1e-6
GN_EPS = 64e-5
NEG_INF = -1e30

LANES = 128
SUBLANES = 8
MXU_DIM_V7X = 256
VMEM_BYTES_V7X = 64 * 1024 * 1024
VMEM_LIMIT = VMEM_BYTES_V7X - 8 * 1024 * 1024

GATE_OFF = 0
RW_OFF = GATE_OFF + 2 * D_MODEL
LAT_OFF = RW_OFF + 3 * RW_WIDTH
LAT_COLS = RW_COLS - 3 * RW_WIDTH
LAT_PAD = 512
ATT_OFF = LAT_OFF + LAT_PAD
N_IN_PAD = ATT_OFF + 3 * ATT_WIDTH
RW_COLS_PAD = 3 * RW_WIDTH + LAT_PAD
FF_TILE = 512
D_FF_PAD = -(-D_FF // FF_TILE) * FF_TILE
RW_CHUNK = 64
ATT_SPAN = 64


def _cparams(sem):
    return pltpu.CompilerParams(dimension_semantics=sem, vmem_limit_bytes=VMEM_LIMIT)


def _rms(x, gain):
    return x * lax.rsqrt(jnp.mean(x * x, axis=-1, keepdims=True) + RMS_EPS) * gain


def _ada_kernel(c_ref, w_ref, b_ref, o_ref):
    c = c_ref[...]
    s = c * jax.nn.sigmoid(c)
    o_ref[...] = jnp.dot(s.astype(BF16), w_ref[...].astype(BF16),
                         preferred_element_type=F32) + b_ref[...]


def _ada(c, w, b, tn=1024):
    m, k = c.shape
    n = w.shape[1]
    return pl.pallas_call(
        _ada_kernel,
        name="ada",
        grid=(n // tn,),
        in_specs=[pl.BlockSpec((m, k), lambda j: (0, 0)),
                  pl.BlockSpec((k, tn), lambda j: (0, j)),
                  pl.BlockSpec((1, tn), lambda j: (0, j))],
        out_specs=pl.BlockSpec((m, tn), lambda j: (0, j)),
        out_shape=jax.ShapeDtypeStruct((m, n), F32),
        compiler_params=_cparams(("parallel",)),
    )(c, w, b)


def _inproj_kernel(x_ref, g_ref, sc_ref, sh_ref, w_ref, o_ref, h_ref):
    @pl.when(pl.program_id(1) == 0)
    def _():
        h = _rms(x_ref[...], g_ref[...]) * (1.0 + sc_ref[0]) + sh_ref[0]
        h_ref[...] = h.astype(BF16)

    o_ref[...] = jnp.dot(h_ref[...], w_ref[...], preferred_element_type=F32)


def _inproj(x, gain, sc, sh, w, seq, tm=1024, tn=1536):
    m, k = x.shape
    n = w.shape[1]
    bpr = seq // tm
    return pl.pallas_call(
        _inproj_kernel,
        name="inproj",
        grid=(m // tm, n // tn),
        in_specs=[pl.BlockSpec((tm, k), lambda i, j: (i, 0)),
                  pl.BlockSpec((1, k), lambda i, j: (0, 0)),
                  pl.BlockSpec((1, 1, k), lambda i, j: (i // bpr, 0, 0)),
                  pl.BlockSpec((1, 1, k), lambda i, j: (i // bpr, 0, 0)),
                  pl.BlockSpec((k, tn), lambda i, j: (0, j))],
        out_specs=pl.BlockSpec((tm, tn), lambda i, j: (i, j)),
        out_shape=jax.ShapeDtypeStruct((m, n), F32),
        scratch_shapes=[pltpu.VMEM((tm, k), BF16)],
        compiler_params=_cparams(("parallel", "arbitrary")),
    )(x, gain, sc, sh, w)


def _attn_kernel(q_ref, k_ref, v_ref, cos_ref, sin_ref, o_ref, qr_ref, kr_ref, og_ref, lse_ref, *, seq):
    gid = pl.program_id(2)
    rows = 512
    half = ATT_HEAD_DIM // 2

    def rot(i, _):
        sl = pl.ds(pl.multiple_of(i * rows, rows), rows)
        cos = cos_ref[sl, :]
        sin = sin_ref[sl, :]
        q = q_ref[0, sl, :]
        k = k_ref[0, sl, :]
        qr_ref[sl, :] = (q * cos + pltpu.roll(q, half, 1) * sin) * (ATT_HEAD_DIM ** -0.5)
        kr_ref[sl, :] = k * cos + pltpu.roll(k, half, 1) * sin
        return 0

    lax.fori_loop(0, seq // rows, rot, 0)

    def group(gi, dil):
        cls_len = seq // dil
        bq = min(128, cls_len)
        bk = min(bq + 2 * ATT_SPAN, cls_len)
        nqb = cls_len // bq
        qpos0 = lax.broadcasted_iota(jnp.int32, (bq, bk), 0)
        kpos0 = lax.broadcasted_iota(jnp.int32, (bq, bk), 1)

        def body(it, _):
            cls = it // nqb
            p0 = (it % nqb) * bq
            ks = jnp.clip(p0 - ATT_SPAN, 0, cls_len - bk)
            if dil == 1:
                rq = pl.ds(p0, bq)
                rk = pl.ds(ks, bk)
            else:
                rq = pl.ds(cls + dil * p0, bq, stride=dil)
                rk = pl.ds(cls + dil * ks, bk, stride=dil)
            qb = qr_ref[rq, :].astype(BF16)
            kb = kr_ref[rk, :].astype(BF16)
            vb = v_ref[0, rk, :].astype(BF16)
            s = lax.dot_general(qb, kb, (((1,), (1,)), ((), ())), preferred_element_type=F32)
            band = jnp.abs((qpos0 + p0) - (kpos0 + ks)) <= ATT_SPAN
            s = jnp.where(band, s, NEG_INF)
            m = jnp.max(s, axis=-1, keepdims=True)
            p = jnp.exp(s - m)
            l = jnp.sum(p, axis=-1, keepdims=True)
            o = jnp.dot(p.astype(BF16), vb, preferred_element_type=F32) / l
            og_ref[gi, rq, :] = o
            lse_ref[gi, rq, :] = jnp.broadcast_to(m + jnp.log(l), (bq, ATT_HEAD_DIM))
            return 0

        lax.fori_loop(0, dil * nqb, body, 0, unroll=4)

    for gi, (_, dil) in enumerate(ATT_GROUPS):
        @pl.when(gid == gi)
        def _(gi=gi, dil=dil):
            group(gi, dil)

    @pl.when(gid == len(ATT_GROUPS) - 1)
    def _():
        def comb(i, _):
            sl = pl.ds(pl.multiple_of(i * rows, rows), rows)
            l0 = lse_ref[0, sl, :]
            l1 = lse_ref[1, sl, :]
            l2 = lse_ref[2, sl, :]
            mx = jnp.maximum(jnp.maximum(l0, l1), l2)
            w0 = jnp.exp(l0 - mx)
            w1 = jnp.exp(l1 - mx)
            w2 = jnp.exp(l2 - mx)
            num = w0 * og_ref[0, sl, :] + w1 * og_ref[1, sl, :] + w2 * og_ref[2, sl, :]
            o_ref[0, sl, :] = (num / (w0 + w1 + w2)).astype(o_ref.dtype)
            return 0

        lax.fori_loop(0, seq // rows, comb, 0)


def _attention(proj3, cos, sin):
    bsz, seq, _ = proj3.shape
    hpg = ATT_HEADS_PER_GROUP
    ng = len(ATT_GROUPS)
    blk = (1, seq, ATT_HEAD_DIM)
    q0 = ATT_OFF // ATT_HEAD_DIM
    return pl.pallas_call(
        functools.partial(_attn_kernel, seq=seq),
        name="attn",
        grid=(bsz, hpg, ng),
        in_specs=[pl.BlockSpec(blk, lambda b, s, g: (b, 0, q0 + g * hpg + s)),
                  pl.BlockSpec(blk, lambda b, s, g: (b, 0, q0 + ATT_HEADS + g * hpg + s)),
                  pl.BlockSpec(blk, lambda b, s, g: (b, 0, q0 + 2 * ATT_HEADS + g * hpg + s)),
                  pl.BlockSpec((seq, ATT_HEAD_DIM), lambda b, s, g: (0, 0)),
                  pl.BlockSpec((seq, ATT_HEAD_DIM), lambda b, s, g: (0, 0))],
        out_specs=pl.BlockSpec(blk, lambda b, s, g: (b, 0, s)),
        out_shape=jax.ShapeDtypeStruct((bsz, seq, ATT_OUT), BF16),
        scratch_shapes=[pltpu.VMEM((seq, ATT_HEAD_DIM), F32),
                        pltpu.VMEM((seq, ATT_HEAD_DIM), F32),
                        pltpu.VMEM((ng, seq, ATT_HEAD_DIM), F32),
                        pltpu.VMEM((ng, seq, ATT_HEAD_DIM), F32)],
        compiler_params=_cparams(("parallel", "parallel", "arbitrary")),
    )(proj3, proj3, proj3, cos, sin)


def _rwkv_kernel(r_ref, v_ref, kk_ref, lw_ref, kd_ref, b_ref, y_ref, s_ref):
    cn = RW_CHUNK
    c2 = 2 * cn
    rev = pl.program_id(1) == 1
    sign = 1 - 2 * pl.program_id(1)

    @pl.when(pl.program_id(2) == 0)
    def _():
        s_ref[...] = jnp.zeros_like(s_ref)

    ti = lax.broadcasted_iota(jnp.int32, (cn, cn), 0)
    si = lax.broadcasted_iota(jnp.int32, (cn, cn), 1)
    tri = ((si - ti) * sign <= 0).astype(BF16)
    lw_all = lw_ref[0, 0]
    lw_hi = lw_all.astype(BF16)
    lw_lo = (lw_all - lw_hi.astype(F32)).astype(BF16)
    g_all = (jnp.dot(tri, lw_hi, preferred_element_type=F32)
             + jnp.dot(tri, lw_lo, preferred_element_type=F32))

    t2 = lax.broadcasted_iota(jnp.int32, (c2, c2), 0)
    s2 = lax.broadcasted_iota(jnp.int32, (c2, c2), 1)
    eye = (t2 == s2).astype(F32)
    order = (s2 % cn - t2 % cn) * sign
    same_blk = (t2 // cn) == (s2 // cn)
    strict = (order < 0) & same_blk
    incl = (order <= 0) & same_blk
    top = t2 < cn
    lane = lax.broadcasted_iota(jnp.int32, (1, LANES), 1)
    first = lane < RW_HEAD
    same_head = ((lax.broadcasted_iota(jnp.int32, (LANES, LANES), 0) // RW_HEAD)
                 == (lax.broadcasted_iota(jnp.int32, (LANES, LANES), 1) // RW_HEAD))

    def stack(x):
        return jnp.concatenate([jnp.where(first, x, 0.0), jnp.where(first, 0.0, x)], axis=0)

    def mm(a, b):
        return jnp.dot(a.astype(BF16), b.astype(BF16), preferred_element_type=F32)

    def mm_nt(a, b):
        return lax.dot_general(a.astype(BF16), b.astype(BF16), (((1,), (1,)), ((), ())),
                               preferred_element_type=F32)

    pairs = range(RW_WIDTH // LANES)
    lanes = [slice(hp * LANES, (hp + 1) * LANES) for hp in pairs]
    xq, vs, bk, decay_end, amat = [], [], [], [], []
    for ls in lanes:
        r = r_ref[0, :, ls]
        v = v_ref[0, :, ls]
        kk = kk_ref[0, :, ls]
        kd = kd_ref[0, 0, :, ls]
        b = b_ref[0, 0, :, ls]
        g = g_all[:, ls]
        g_end = jnp.where(rev, g[0:1], g[cn - 1:cn])
        e_neg = jnp.exp(-g)
        e_end = jnp.exp(g_end - g)
        x = jnp.concatenate([stack(kk * jnp.exp(g - lw_all[:, ls])), stack(r * jnp.exp(g))],
                            axis=0).astype(BF16)
        y = jnp.concatenate([b * e_neg, kd * e_neg], axis=0).astype(BF16)
        xq.append(x)
        vs.append(stack(v).astype(BF16))
        bk.append(jnp.concatenate([b * e_end, kd * e_end], axis=0).astype(BF16))
        decay_end.append(jnp.exp(g_end))
        amat.append(lax.dot_general(x, y, (((1,), (1,)), ((), ())), preferred_element_type=F32))
    swapped = [pltpu.roll(am, cn, 1) for am in amat]
    tmat = [jnp.where(strict, jnp.where(top, am[:c2], sw[:c2]), 0.0) for am, sw in zip(amat, swapped)]
    rest = [jnp.concatenate([jnp.where(strict, jnp.where(top, sw[:c2], am[:c2]), 0.0),
                             jnp.where(incl, jnp.where(top, sw[c2:], am[c2:]), 0.0)],
                            axis=0).astype(BF16) for am, sw in zip(amat, swapped)]
    arb = [jnp.where(incl, jnp.where(top, am[c2:], sw[c2:]), 0.0).astype(BF16) for am, sw in zip(amat, swapped)]
    ninv = [eye - t for t in tmat]
    pw = [t.astype(BF16) for t in tmat]
    pw = [jnp.dot(p, p, preferred_element_type=F32).astype(BF16) for p in pw]
    nsq = cn.bit_length() - 2
    for it in range(nsq):
        if it < nsq - 1:
            both = [jnp.dot(p, jnp.concatenate([p, n.astype(BF16)], axis=1), preferred_element_type=F32)
                    for p, n in zip(pw, ninv)]
            pw = [bo[:, :c2].astype(BF16) for bo in both]
            ninv = [n + bo[:, c2:] for n, bo in zip(ninv, both)]
        else:
            ninv = [n + jnp.dot(p, n.astype(BF16), preferred_element_type=F32) for p, n in zip(pw, ninv)]
    mv = [jnp.dot(m, v, preferred_element_type=F32) for m, v in zip(rest, vs)]
    s0 = [s_ref[hp] for hp in pairs]
    xs = [mm_nt(x, s) for x, s in zip(xq, s0)]
    us = [mm(n, -x[:c2] - m[:c2]) for n, x, m in zip(ninv, xs, mv)]
    ys = [x[c2:] + m[c2:] + mm(ab, u) for x, m, ab, u in zip(xs, mv, arb, us)]
    for hp in pairs:
        y_ref[0, 0, :, lanes[hp]] = ys[hp][:cn] + ys[hp][cn:]
        uv = jnp.concatenate([us[hp][:cn] + us[hp][cn:], v_ref[0, :, lanes[hp]]], axis=0)
        upd = mm(uv.T, bk[hp])
        s_ref[hp] = jnp.where(same_head, s0[hp] * decay_end[hp] + upd, 0.0)


def _rwkv_scan(r, v, kk, lw, kd, b):
    bsz, seq, width = r.shape
    cn = RW_CHUNK
    nc = seq // cn

    def shared(b, z, c):
        return (b, c + z * (nc - 1 - 2 * c), 0)

    def per_dir(b, z, c):
        return (z, b, c + z * (nc - 1 - 2 * c), 0)

    sblk = pl.BlockSpec((1, cn, width), shared)
    dblk = pl.BlockSpec((1, 1, cn, width), per_dir)
    return pl.pallas_call(
        _rwkv_kernel,
        name="rwkv",
        grid=(bsz, 2, nc),
        in_specs=[sblk, sblk, sblk, dblk, dblk, dblk],
        out_specs=dblk,
        out_shape=jax.ShapeDtypeStruct((2, bsz, seq, width), F32),
        scratch_shapes=[pltpu.VMEM((width // LANES, LANES, LANES), F32)],
        compiler_params=_cparams(("parallel", "parallel", "arbitrary")),
    )(r, v, kk, lw, kd, b)


def _head_sum(x, ones, passes):
    gw = ones.shape[0]
    cols = []
    for c0 in range(0, x.shape[1], gw):
        xc = x[:, c0:c0 + gw]
        hi = xc.astype(BF16)
        out = jnp.dot(hi, ones, preferred_element_type=F32)
        if passes == 2:
            lo = (xc - hi.astype(F32)).astype(BF16)
            out = out + jnp.dot(lo, ones, preferred_element_type=F32)
        cols.append(out)
    return jnp.concatenate(cols, axis=1)


def _prep_kernel(r_ref, k_ref, v_ref, l_ref, rp_ref, kp_ref, vp_ref, lp_ref, rn_ref, kn_ref, vn_ref, ln_ref,
                 mu_ref, dbase_ref, dup_ref, ibase_ref, iup_ref, gup_ref, kkw_ref, ka_ref, rk_ref, ones_ref,
                 ro_ref, vo_ref, kko_ref, go_ref, bvo_ref, lwo_ref, kdo_ref, bo_ref, *, tm, seq):
    i = pl.program_id(0)
    bpr = seq // tm
    has_prev = (i % bpr != 0).astype(F32)
    has_next = (i % bpr != bpr - 1).astype(F32)
    rowi = lax.broadcasted_iota(jnp.int32, (tm, 1), 0)
    cw = RW_WIDTH

    def shifted(main_ref, prev_ref, next_ref, c0, c1):
        z = main_ref[...]
        zp = jnp.where(rowi == 0, prev_ref[SUBLANES - 1:SUBLANES, :] * has_prev, pltpu.roll(z, 1, 0))
        zn = jnp.where(rowi == tm - 1, next_ref[0:1, :] * has_next, pltpu.roll(z, tm - 1, 0))
        return z + mu_ref[0:1, c0:c1] * (zp - z) + mu_ref[1:2, c0:c1] * (zn - z)

    r = shifted(r_ref, rp_ref, rn_ref, 0, cw)
    k = shifted(k_ref, kp_ref, kn_ref, cw, 2 * cw)
    v = shifted(v_ref, vp_ref, vn_ref, 2 * cw, 3 * cw)
    lat = shifted(l_ref, lp_ref, ln_ref, 3 * cw, 3 * cw + LAT_PAD)
    nd, ni = 2 * DECAY_LORA, 2 * ICLR_LORA
    wl = jnp.tanh(lat[:, :nd])
    wl_hi = wl.astype(BF16)
    wl_lo = (wl - wl_hi.astype(F32)).astype(BF16)
    w_raw = (dbase_ref[...] + jnp.dot(wl_hi, dup_ref[0], preferred_element_type=F32)
             + jnp.dot(wl_lo, dup_ref[0], preferred_element_type=F32)
             + jnp.dot(wl_hi, dup_ref[1], preferred_element_type=F32))
    neg = -w_raw
    softplus = jnp.maximum(neg, 0.0) + jnp.log(1.0 + jnp.exp(-jnp.abs(neg)))
    lw = -jnp.exp(-softplus - 0.5)
    a = jax.nn.sigmoid(ibase_ref[...] + jnp.dot(lat[:, nd:nd + ni].astype(BF16), iup_ref[...],
                                                preferred_element_type=F32))
    g = jnp.dot(jax.nn.sigmoid(lat[:, nd + ni:]).astype(BF16), gup_ref[...], preferred_element_type=F32)
    ones = ones_ref[...]
    kkr = k * kkw_ref[...]
    kk = kkr / jnp.maximum(jnp.sqrt(_head_sum(kkr * kkr, ones, 2)), 1e-12)
    ka = ka_ref[...]
    kd0 = k * (1.0 + (a[:, :cw] - 1.0) * ka)
    kd1 = k * (1.0 + (a[:, cw:] - 1.0) * ka)
    bonus = _head_sum(r * (kd0 + kd1) * rk_ref[...], ones, 2)
    ro_ref[...] = r
    vo_ref[...] = v
    kko_ref[...] = kk
    go_ref[...] = g
    bvo_ref[...] = bonus * v
    lwo_ref[0] = lw[:, :cw]
    lwo_ref[1] = lw[:, cw:]
    kdo_ref[0] = kd0
    kdo_ref[1] = kd1
    bo_ref[0] = kk * a[:, :cw]
    bo_ref[1] = kk * a[:, cw:]


def _prep(proj, mu, dbase, dup, ibase, iup, gup, kkw, ka, rk, ones, seq, tm=256):
    m = proj.shape[0]
    cw = RW_WIDTH
    bpr = seq // tm
    hb = tm // SUBLANES
    nhb = m // SUBLANES
    r0 = RW_OFF // cw
    l0 = LAT_OFF // LAT_PAD
    cols = [(cw, r0), (cw, r0 + 1), (cw, r0 + 2), (LAT_PAD, l0)]
    main = [pl.BlockSpec((tm, w), lambda i, c=c: (i, c)) for w, c in cols]
    prev = [pl.BlockSpec((SUBLANES, w), lambda i, c=c: (jnp.maximum(i * hb - 1, 0), c)) for w, c in cols]
    nxt = [pl.BlockSpec((SUBLANES, w), lambda i, c=c: (jnp.minimum((i + 1) * hb, nhb - 1), c)) for w, c in cols]
    params = [mu, dbase, dup, ibase, iup, gup, kkw, ka, rk, ones]
    pspecs = [pl.BlockSpec(t.shape, lambda i, n=t.ndim: (0,) * n) for t in params]
    oblk = pl.BlockSpec((tm, cw), lambda i: (i, 0))
    dblk = pl.BlockSpec((2, tm, cw), lambda i: (0, i, 0))
    one = jax.ShapeDtypeStruct((m, cw), F32)
    two = jax.ShapeDtypeStruct((2, m, cw), F32)
    return pl.pallas_call(
        functools.partial(_prep_kernel, tm=tm, seq=seq),
        name="rwkv_prep",
        grid=(m // tm,),
        in_specs=main + prev + nxt + pspecs,
        out_specs=[oblk] * 5 + [dblk] * 3,
        out_shape=[one] * 5 + [two] * 3,
        compiler_params=_cparams(("parallel",)),
    )(*([proj] * 12), *params)


def _mix_kernel(att_ref, y_ref, bv_ref, g_ref, lnw_ref, lnb_ref, ones_ref, ga_ref, gr_ref, x_ref, gt_ref,
                gain_ref, wa_ref, wr_ref, wo_ref, o_ref):
    ones = ones_ref[...]
    y = y_ref[0] + y_ref[1]
    mu = _head_sum(y, ones, 1) * (1.0 / RW_HEAD)
    yc = y - mu
    var = _head_sum(yc * yc, ones, 1) * (1.0 / RW_HEAD)
    rw = ((yc * lax.rsqrt(var + GN_EPS)) * lnw_ref[...] + lnb_ref[...] + bv_ref[...]) * g_ref[...]
    pa = jnp.dot(att_ref[...], wa_ref[...], preferred_element_type=F32)
    pr = jnp.dot(rw.astype(BF16), wr_ref[...], preferred_element_type=F32)
    merged = jax.nn.sigmoid(ga_ref[...]) * pa + jax.nn.sigmoid(gr_ref[...]) * pr
    mix = jnp.dot(merged.astype(BF16), wo_ref[...], preferred_element_type=F32)
    o_ref[...] = x_ref[...] + gt_ref[0] * _rms(mix, gain_ref[...])


def _mix(att, y, bv, g, lnw, lnb, ones, proj, x, gt, gain, wa, wr, wo, seq, tm=256):
    m, d = x.shape
    cw = RW_WIDTH
    bpr = seq // tm
    gblk = GATE_OFF // d
    const = lambda i: (0, 0)
    resident = lambda t: pl.BlockSpec(t.shape, const, pipeline_mode=pl.Buffered(1))
    return pl.pallas_call(
        _mix_kernel,
        name="mix",
        grid=(m // tm,),
        in_specs=[pl.BlockSpec((tm, att.shape[1]), lambda i: (i, 0)),
                  pl.BlockSpec((2, tm, cw), lambda i: (0, i, 0)),
                  pl.BlockSpec((tm, cw), lambda i: (i, 0)),
                  pl.BlockSpec((tm, cw), lambda i: (i, 0)),
                  pl.BlockSpec((1, cw), const),
                  pl.BlockSpec((1, cw), const),
                  resident(ones),
                  pl.BlockSpec((tm, d), lambda i: (i, gblk)),
                  pl.BlockSpec((tm, d), lambda i: (i, gblk + 1)),
                  pl.BlockSpec((tm, d), lambda i: (i, 0)),
                  pl.BlockSpec((1, 1, d), lambda i: (i // bpr, 0, 0)),
                  pl.BlockSpec((1, d), const),
                  resident(wa), resident(wr), resident(wo)],
        out_specs=pl.BlockSpec((tm, d), lambda i: (i, 0)),
        out_shape=jax.ShapeDtypeStruct((m, d), F32),
        compiler_params=_cparams(("parallel",)),
    )(att, y, bv, g, lnw, lnb, ones, proj, proj, x, gt, gain, wa, wr, wo)


def _ffn_kernel(x_ref, xp_ref, xn_ref, gpre_ref, sc_ref, sh_ref, wg_ref, wu_ref, cw_ref, cb_ref, wd_ref,
                gt_ref, gpost_ref, o_ref, h_ref, *, tm, seq):
    i = pl.program_id(0)
    j = pl.program_id(1)
    halo = SUBLANES
    bpr = seq // tm

    @pl.when(j == 0)
    def _():
        gain = gpre_ref[...]
        sc = 1.0 + sc_ref[0]
        sh = sh_ref[0]
        has_prev = (i % bpr != 0).astype(F32)
        has_next = (i % bpr != bpr - 1).astype(F32)
        h_ref[0:halo, :] = ((_rms(xp_ref[...], gain) * sc + sh) * has_prev).astype(BF16)
        h_ref[halo:halo + tm, :] = (_rms(x_ref[...], gain) * sc + sh).astype(BF16)
        h_ref[halo + tm:, :] = ((_rms(xn_ref[...], gain) * sc + sh) * has_next).astype(BF16)
        o_ref[...] = jnp.zeros_like(o_ref)

    ext = tm + 2 * halo
    gate = jnp.dot(h_ref[...], wg_ref[...], preferred_element_type=F32)
    prev = pltpu.roll(gate, 1, 0)[halo:halo + tm]
    nxt = pltpu.roll(gate, ext - 1, 0)[halo:halo + tm]
    u = cw_ref[0:1, :] * prev + cw_ref[1:2, :] * gate[halo:halo + tm] + cw_ref[2:3, :] * nxt + cb_ref[...]
    up = jnp.dot(h_ref[halo:halo + tm, :], wu_ref[...], preferred_element_type=F32)
    act = jax.nn.gelu(u, approximate=True) * up
    o_ref[...] += jnp.dot(act.astype(BF16), wd_ref[...], preferred_element_type=F32)

    @pl.when(j == pl.num_programs(1) - 1)
    def _():
        o_ref[...] = x_ref[...] + gt_ref[0] * _rms(o_ref[...], gpost_ref[...])


def _ffn(x, gpre, sc, sh, wg, wu, cw, cb, wd, gt, gpost, seq, tm=1024, tf=FF_TILE // 2):
    m, d = x.shape
    f = wg.shape[1]
    bpr = seq // tm
    hb = tm // SUBLANES
    nhb = m // SUBLANES
    const = lambda i, j: (0, 0)
    bidx = lambda i, j: (i // bpr, 0, 0)
    return pl.pallas_call(
        functools.partial(_ffn_kernel, tm=tm, seq=seq),
        name="ffn",
        grid=(m // tm, f // tf),
        in_specs=[pl.BlockSpec((tm, d), lambda i, j: (i, 0)),
                  pl.BlockSpec((SUBLANES, d), lambda i, j: (jnp.maximum(i * hb - 1, 0), 0)),
                  pl.BlockSpec((SUBLANES, d), lambda i, j: (jnp.minimum((i + 1) * hb, nhb - 1), 0)),
                  pl.BlockSpec((1, d), const),
                  pl.BlockSpec((1, 1, d), bidx),
                  pl.BlockSpec((1, 1, d), bidx),
                  pl.BlockSpec((d, tf), lambda i, j: (0, j)),
                  pl.BlockSpec((d, tf), lambda i, j: (0, j)),
                  pl.BlockSpec((SUBLANES, tf), lambda i, j: (0, j)),
                  pl.BlockSpec((1, tf), lambda i, j: (0, j)),
                  pl.BlockSpec((tf, d), lambda i, j: (j, 0)),
                  pl.BlockSpec((1, 1, d), bidx),
                  pl.BlockSpec((1, d), const)],
        out_specs=pl.BlockSpec((tm, d), lambda i, j: (i, 0)),
        out_shape=jax.ShapeDtypeStruct((m, d), F32),
        scratch_shapes=[pltpu.VMEM((tm + 2 * SUBLANES, d), BF16)],
        compiler_params=_cparams(("parallel", "arbitrary")),
    )(x, x, x, gpre, sc, sh, wg, wu, cw, cb, wd, gt, gpost)


def _rope_tables(seq):
    half = ATT_HEAD_DIM // 2
    inv_freq = 1.0 / (ROPE_THETA ** (jnp.arange(half, dtype=F32) / half))
    ang = jnp.arange(seq, dtype=F32)[:, None] * inv_freq[None, :]
    cos = jnp.cos(ang)
    sin = jnp.sin(ang)
    return jnp.concatenate([cos, cos], axis=-1), jnp.concatenate([-sin, sin], axis=-1)


def _trunk(x, ada, p):
    bsz, seq, d = x.shape
    m = bsz * seq
    sh1, sc1, gt1, sh2, sc2, gt2 = [t[:, None, :] for t in jnp.split(ada, 6, axis=-1)]
    x2 = x.reshape(m, d)
    proj = _inproj(x2, p['ln_mix_pre'], sc1, sh1, p['w_in'], seq)
    proj3 = proj.reshape(bsz, seq, N_IN_PAD)
    cos, sin = _rope_tables(seq)
    att = _attention(proj3, cos, sin)
    r, v, kk, g, bv, lw, kd, b = _prep(proj, p['shift_mu'], p['decay_base'], p['decay_up'], p['iclr_base'],
                                       p['iclr_up'], p['gate_up'], p['k_k'], p['k_a'], p['r_k'], p['head_ones'], seq)
    per_seq = lambda t: t.reshape(t.shape[:-2] + (bsz, seq, RW_WIDTH))
    ys = _rwkv_scan(per_seq(r), per_seq(v), per_seq(kk), per_seq(lw), per_seq(kd), per_seq(b))
    x1 = _mix(att.reshape(m, ATT_OUT), ys.reshape(2, m, RW_WIDTH), bv, g, p['lnx_w'], p['lnx_b'], p['head_ones'],
              proj, x2, gt1, p['ln_mix_post'], p['w_att_branch'], p['w_rwkv_branch'], p['w_out'], seq)
    y = _ffn(x1, p['ln_ffn_pre'], sc2, sh2, p['w_ffn_gate'], p['w_ffn_up'], p['ffn_conv_w'], p['ffn_conv_b'],
             p['w_ffn_down'], gt2, p['ln_ffn_post'], seq)
    return y.reshape(bsz, seq, d)


def kernel(x_prompt, x_sample, c_prompt, c_sample, ln_mix_pre, ln_mix_post, ln_ffn_pre, ln_ffn_post,
           w_ada, b_ada, w_in, shift_mu, decay_base, decay_up, iclr_base, iclr_up, gate_up,
           k_k, k_a, r_k, lnx_w, lnx_b, w_att_branch, w_rwkv_branch, w_out,
           w_ffn_gate, w_ffn_up, ffn_conv_w, ffn_conv_b, w_ffn_down):
    depth = w_in.shape[0]
    nb_p = c_prompt.shape[0]
    nb_s = c_sample.shape[0]
    c_all = jnp.concatenate([c_prompt, c_sample], axis=0)
    c_all = jnp.pad(c_all, ((0, (-c_all.shape[0]) % SUBLANES), (0, 0)))
    fpad = D_FF_PAD - D_FF
    xp, xs = x_prompt, x_sample
    lane_head = jnp.arange(MXU_DIM_V7X) // RW_HEAD
    head_ones = (lane_head[:, None] == lane_head[None, :]).astype(BF16)
    zdl = jnp.zeros((DECAY_LORA, RW_WIDTH), F32)
    zil = jnp.zeros((ICLR_LORA, RW_WIDTH), F32)
    for l in range(depth):
        att_end = 3 * ATT_WIDTH
        rw_end = att_end + RW_COLS
        w_in_b = w_in[l].astype(BF16)
        w_in_l = jnp.concatenate([w_in_b[:, rw_end:], w_in_b[:, att_end:rw_end],
                                  jnp.zeros((D_MODEL, LAT_PAD - LAT_COLS), BF16), w_in_b[:, :att_end]], axis=1)
        dup = jnp.block([[decay_up[l][0], zdl], [zdl, decay_up[l][1]]])
        dup_hi = dup.astype(BF16)
        p = dict(
            ln_mix_pre=ln_mix_pre[l][None], ln_mix_post=ln_mix_post[l][None],
            ln_ffn_pre=ln_ffn_pre[l][None], ln_ffn_post=ln_ffn_post[l][None],
            w_in=w_in_l, head_ones=head_ones,
            shift_mu=jnp.pad(shift_mu[l], ((0, 0), (0, RW_COLS_PAD - RW_COLS))),
            decay_base=decay_base[l].reshape(1, 2 * RW_WIDTH),
            decay_up=jnp.stack([dup_hi, (dup - dup_hi.astype(F32)).astype(BF16)]),
            iclr_base=iclr_base[l].reshape(1, 2 * RW_WIDTH),
            iclr_up=jnp.block([[iclr_up[l][0], zil], [zil, iclr_up[l][1]]]).astype(BF16),
            gate_up=jnp.pad(gate_up[l], ((0, LAT_PAD - LAT_COLS), (0, 0))).astype(BF16),
            k_k=k_k[l][None], k_a=k_a[l][None], r_k=r_k[l].reshape(1, RW_WIDTH),
            lnx_w=lnx_w[l][None], lnx_b=lnx_b[l][None],
            w_att_branch=w_att_branch[l].astype(BF16), w_rwkv_branch=w_rwkv_branch[l].astype(BF16),
            w_out=w_out[l].astype(BF16),
            w_ffn_gate=jnp.pad(w_ffn_gate[l], ((0, 0), (0, fpad))).astype(BF16),
            w_ffn_up=jnp.pad(w_ffn_up[l], ((0, 0), (0, fpad))).astype(BF16),
            ffn_conv_w=jnp.pad(ffn_conv_w[l], ((0, SUBLANES - 3), (0, fpad))),
            ffn_conv_b=jnp.pad(ffn_conv_b[l], ((0, fpad),))[None],
            w_ffn_down=jnp.pad(w_ffn_down[l], ((0, fpad), (0, 0))).astype(BF16),
        )
        ada = _ada(c_all, w_ada[l], b_ada[l][None])
        xp = _trunk(xp, ada[:nb_p], p)
        xs = _trunk(xs, ada[nb_p:nb_p + nb_s], p)
    return (xp, xs)
```

```python
import functools

import jax
import jax.numpy as jnp
from jax import lax
from jax.experimental import pallas as pl
from jax.experimental.pallas import tpu as pltpu

F32 = jnp.float32
BF16 = jnp.bfloat16

D_MODEL = 2048
ATT_GROUPS = ((128, 1), (512, 4), (2048, 16))
ATT_HEADS_PER_GROUP = 4
ATT_HEAD_DIM = 128
ATT_HEADS = ATT_HEADS_PER_GROUP * len(ATT_GROUPS)
ATT_WIDTH = ATT_HEADS * ATT_HEAD_DIM
ATT_OUT = ATT_HEADS_PER_GROUP * ATT_HEAD_DIM
ROPE_THETA = 10000.0
RW_HEAD = 64
RW_WIDTH = D_MODEL // 2
RW_HEADS = RW_WIDTH // RW_HEAD
DECAY_LORA = 64
ICLR_LORA = 64
GATE_LORA = 160
RW_COLS = 3 * RW_WIDTH + 2 * DECAY_LORA + 2 * ICLR_LORA + GATE_LORA
N_IN = 3 * ATT_WIDTH + RW_COLS + 2 * D_MODEL
D_FF = ((8 * D_MODEL // 3 + 127) // 128) * 128
RMS_EPS = 1e-6
GN_EPS = 64e-5
NEG_INF = -1e30

LANES = 128
SUBLANES = 8
MXU_DIM_V7X = 256
VMEM_BYTES_V7X = 64 * 1024 * 1024
VMEM_LIMIT = VMEM_BYTES_V7X - 8 * 1024 * 1024

GATE_OFF = 0
RW_OFF = GATE_OFF + 2 * D_MODEL
LAT_OFF = RW_OFF + 3 * RW_WIDTH
LAT_COLS = RW_COLS - 3 * RW_WIDTH
LAT_PAD = 512
ATT_OFF = LAT_OFF + LAT_PAD
N_IN_PAD = ATT_OFF + 3 * ATT_WIDTH
RW_COLS_PAD = 3 * RW_WIDTH + LAT_PAD
FF_TILE = 512
D_FF_PAD = -(-D_FF // FF_TILE) * FF_TILE
RW_CHUNK = 64
ATT_SPAN = 64


def _cparams(sem):
    return pltpu.CompilerParams(dimension_semantics=sem, vmem_limit_bytes=VMEM_LIMIT)


def _rms(x, gain):
    return x * lax.rsqrt(jnp.mean(x * x, axis=-1, keepdims=True) + RMS_EPS) * gain


def _ada_kernel(c_ref, w_ref, b_ref, o_ref):
    c = c_ref[...]
    s = c * jax.nn.sigmoid(c)
    o_ref[...] = jnp.dot(s.astype(BF16), w_ref[...].astype(BF16),
                         preferred_element_type=F32) + b_ref[...]


def _ada(c, w, b, tn=1024):
    m, k = c.shape
    n = w.shape[1]
    return pl.pallas_call(
        _ada_kernel,
        name="ada",
        grid=(n // tn,),
        in_specs=[pl.BlockSpec((m, k), lambda j: (0, 0)),
                  pl.BlockSpec((k, tn), lambda j: (0, j)),
                  pl.BlockSpec((1, tn), lambda j: (0, j))],
        out_specs=pl.BlockSpec((m, tn), lambda j: (0, j)),
        out_shape=jax.ShapeDtypeStruct((m, n), F32),
        compiler_params=_cparams(("parallel",)),
    )(c, w, b)


def _inproj_kernel(x_ref, g_ref, sc_ref, sh_ref, w_ref, o_ref, h_ref):
    @pl.when(pl.program_id(1) == 0)
    def _():
        h = _rms(x_ref[...], g_ref[...]) * (1.0 + sc_ref[0]) + sh_ref[0]
        h_ref[...] = h.astype(BF16)

    o_ref[...] = jnp.dot(h_ref[...], w_ref[...], preferred_element_type=F32)


def _inproj(x, gain, sc, sh, w, seq, tm=1024, tn=1536):
    m, k = x.shape
    n = w.shape[1]
    bpr = seq // tm
    return pl.pallas_call(
        _inproj_kernel,
        name="inproj",
        grid=(m // tm, n // tn),
        in_specs=[pl.BlockSpec((tm, k), lambda i, j: (i, 0)),
                  pl.BlockSpec((1, k), lambda i, j: (0, 0)),
                  pl.BlockSpec((1, 1, k), lambda i, j: (i // bpr, 0, 0)),
                  pl.BlockSpec((1, 1, k), lambda i, j: (i // bpr, 0, 0)),
                  pl.BlockSpec((k, tn), lambda i, j: (0, j))],
        out_specs=pl.BlockSpec((tm, tn), lambda i, j: (i, j)),
        out_shape=jax.ShapeDtypeStruct((m, n), F32),
        scratch_shapes=[pltpu.VMEM((tm, k), BF16)],
        compiler_params=_cparams(("parallel", "arbitrary")),
    )(x, gain, sc, sh, w)


def _attn_kernel(q_ref, k_ref, v_ref, cos_ref, sin_ref, o_ref, qr_ref, kr_ref, og_ref, lse_ref, *, seq):
    gid = pl.program_id(2)
    rows = 512
    half = ATT_HEAD_DIM // 2

    def rot(i, _):
        sl = pl.ds(pl.multiple_of(i * rows, rows), rows)
        cos = cos_ref[sl, :]
        sin = sin_ref[sl, :]
        q = q_ref[0, sl, :]
        k = k_ref[0, sl, :]
        qr_ref[sl, :] = (q * cos + pltpu.roll(q, half, 1) * sin) * (ATT_HEAD_DIM ** -0.5)
        kr_ref[sl, :] = k * cos + pltpu.roll(k, half, 1) * sin
        return 0

    lax.fori_loop(0, seq // rows, rot, 0)

    def group(gi, dil):
        cls_len = seq // dil
        bq = min(128, cls_len)
        bk = min(bq + 2 * ATT_SPAN, cls_len)
        nqb = cls_len // bq
        qpos0 = lax.broadcasted_iota(jnp.int32, (bq, bk), 0)
        kpos0 = lax.broadcasted_iota(jnp.int32, (bq, bk), 1)

        def body(it, _):
            cls = it // nqb
            p0 = (it % nqb) * bq
            ks = jnp.clip(p0 - ATT_SPAN, 0, cls_len - bk)
            if dil == 1:
                rq = pl.ds(p0, bq)
                rk = pl.ds(ks, bk)
            else:
                rq = pl.ds(cls + dil * p0, bq, stride=dil)
                rk = pl.ds(cls + dil * ks, bk, stride=dil)
            qb = qr_ref[rq, :].astype(BF16)
            kb = kr_ref[rk, :].astype(BF16)
            vb = v_ref[0, rk, :].astype(BF16)
            s = lax.dot_general(qb, kb, (((1,), (1,)), ((), ())), preferred_element_type=F32)
            band = jnp.abs((qpos0 + p0) - (kpos0 + ks)) <= ATT_SPAN
            s = jnp.where(band, s, NEG_INF)
            m = jnp.max(s, axis=-1, keepdims=True)
            p = jnp.exp(s - m)
            l = jnp.sum(p, axis=-1, keepdims=True)
            o = jnp.dot(p.astype(BF16), vb, preferred_element_type=F32) / l
            og_ref[gi, rq, :] = o
            lse_ref[gi, rq, :] = jnp.broadcast_to(m + jnp.log(l), (bq, ATT_HEAD_DIM))
            return 0

        lax.fori_loop(0, dil * nqb, body, 0, unroll=4)

    for gi, (_, dil) in enumerate(ATT_GROUPS):
        @pl.when(gid == gi)
        def _(gi=gi, dil=dil):
            group(gi, dil)

    @pl.when(gid == len(ATT_GROUPS) - 1)
    def _():
        def comb(i, _):
            sl = pl.ds(pl.multiple_of(i * rows, rows), rows)
            l0 = lse_ref[0, sl, :]
            l1 = lse_ref[1, sl, :]
            l2 = lse_ref[2, sl, :]
            mx = jnp.maximum(jnp.maximum(l0, l1), l2)
            w0 = jnp.exp(l0 - mx)
            w1 = jnp.exp(l1 - mx)
            w2 = jnp.exp(l2 - mx)
            num = w0 * og_ref[0, sl, :] + w1 * og_ref[1, sl, :] + w2 * og_ref[2, sl, :]
            o_ref[0, sl, :] = (num / (w0 + w1 + w2)).astype(o_ref.dtype)
            return 0

        lax.fori_loop(0, seq // rows, comb, 0)


def _attention(proj3, cos, sin):
    bsz, seq, _ = proj3.shape
    hpg = ATT_HEADS_PER_GROUP
    ng = len(ATT_GROUPS)
    blk = (1, seq, ATT_HEAD_DIM)
    q0 = ATT_OFF // ATT_HEAD_DIM
    return pl.pallas_call(
        functools.partial(_attn_kernel, seq=seq),
        name="attn",
        grid=(bsz, hpg, ng),
        in_specs=[pl.BlockSpec(blk, lambda b, s, g: (b, 0, q0 + g * hpg + s)),
                  pl.BlockSpec(blk, lambda b, s, g: (b, 0, q0 + ATT_HEADS + g * hpg + s)),
                  pl.BlockSpec(blk, lambda b, s, g: (b, 0, q0 + 2 * ATT_HEADS + g * hpg + s)),
                  pl.BlockSpec((seq, ATT_HEAD_DIM), lambda b, s, g: (0, 0)),
                  pl.BlockSpec((seq, ATT_HEAD_DIM), lambda b, s, g: (0, 0))],
        out_specs=pl.BlockSpec(blk, lambda b, s, g: (b, 0, s)),
        out_shape=jax.ShapeDtypeStruct((bsz, seq, ATT_OUT), BF16),
        scratch_shapes=[pltpu.VMEM((seq, ATT_HEAD_DIM), F32),
                        pltpu.VMEM((seq, ATT_HEAD_DIM), F32),
                        pltpu.VMEM((ng, seq, ATT_HEAD_DIM), F32),
                        pltpu.VMEM((ng, seq, ATT_HEAD_DIM), F32)],
        compiler_params=_cparams(("parallel", "parallel", "arbitrary")),
    )(proj3, proj3, proj3, cos, sin)


def _rwkv_kernel(rf_ref, vf_ref, kkf_ref, lwf_ref, kdf_ref, bf_ref, rb_ref, vb_ref, kkb_ref, lwb_ref, kdb_ref,
                 bb_ref, yf_ref, yb_ref, s_ref):
    cn = RW_CHUNK
    c2 = 2 * cn

    @pl.when(pl.program_id(1) == 0)
    def _():
        s_ref[...] = jnp.zeros_like(s_ref)

    ti = lax.broadcasted_iota(jnp.int32, (cn, cn), 0)
    si = lax.broadcasted_iota(jnp.int32, (cn, cn), 1)
    t2 = lax.broadcasted_iota(jnp.int32, (c2, c2), 0)
    s2 = lax.broadcasted_iota(jnp.int32, (c2, c2), 1)
    eye = (t2 == s2).astype(F32)
    same_blk = (t2 // cn) == (s2 // cn)
    top = t2 < cn
    lane = lax.broadcasted_iota(jnp.int32, (1, LANES), 1)
    first = lane < RW_HEAD
    same_head = ((lax.broadcasted_iota(jnp.int32, (LANES, LANES), 0) // RW_HEAD)
                 == (lax.broadcasted_iota(jnp.int32, (LANES, LANES), 1) // RW_HEAD))

    def stack(x):
        return jnp.concatenate([jnp.where(first, x, 0.0), jnp.where(first, 0.0, x)], axis=0)

    def mm(a, b):
        return jnp.dot(a.astype(BF16), b.astype(BF16), preferred_element_type=F32)

    def mm_nt(a, b):
        return lax.dot_general(a.astype(BF16), b.astype(BF16), (((1,), (1,)), ((), ())),
                               preferred_element_type=F32)

    lanes = [slice(hp * LANES, (hp + 1) * LANES) for hp in range(RW_WIDTH // LANES)]
    inst = []
    xq, vs, bk, decay_end, amat, strict, incl = [], [], [], [], [], [], []
    for z, (r_ref, v_ref, kk_ref, lw_ref, kd_ref, b_ref, y_ref) in enumerate(
            ((rf_ref, vf_ref, kkf_ref, lwf_ref, kdf_ref, bf_ref, yf_ref),
             (rb_ref, vb_ref, kkb_ref, lwb_ref, kdb_ref, bb_ref, yb_ref))):
        rev = z == 1
        tri = ((si >= ti) if rev else (si <= ti)).astype(BF16)
        order = (t2 % cn - s2 % cn) if rev else (s2 % cn - t2 % cn)
        strict_z = (order < 0) & same_blk
        incl_z = (order <= 0) & same_blk
        lw_all = lw_ref[0, 0]
        lw_hi = lw_all.astype(BF16)
        lw_lo = (lw_all - lw_hi.astype(F32)).astype(BF16)
        g_all = (jnp.dot(tri, lw_hi, preferred_element_type=F32)
                 + jnp.dot(tri, lw_lo, preferred_element_type=F32))
        for hp, ls in enumerate(lanes):
            r = r_ref[0, :, ls]
            v = v_ref[0, :, ls]
            kk = kk_ref[0, :, ls]
            kd = kd_ref[0, 0, :, ls]
            b = b_ref[0, 0, :, ls]
            g = g_all[:, ls]
            g_end = g[0:1] if rev else g[cn - 1:cn]
            e_neg = jnp.exp(-g)
            e_end = jnp.exp(g_end - g)
            x = jnp.concatenate([stack(kk * jnp.exp(g - lw_all[:, ls])), stack(r * jnp.exp(g))],
                                axis=0).astype(BF16)
            y = jnp.concatenate([b * e_neg, kd * e_neg], axis=0).astype(BF16)
            inst.append((z, hp, ls, v_ref, y_ref))
            strict.append(strict_z)
            incl.append(incl_z)
            xq.append(x)
            vs.append(stack(v).astype(BF16))
            bk.append(jnp.concatenate([b * e_end, kd * e_end], axis=0).astype(BF16))
            decay_end.append(jnp.exp(g_end))
            amat.append(lax.dot_general(x, y, (((1,), (1,)), ((), ())), preferred_element_type=F32))
    swapped = [pltpu.roll(am, cn, 1) for am in amat]
    tmat = [jnp.where(st, jnp.where(top, am[:c2], sw[:c2]), 0.0) for am, sw, st in zip(amat, swapped, strict)]
    rest = [jnp.concatenate([jnp.where(st, jnp.where(top, sw[:c2], am[:c2]), 0.0),
                             jnp.where(ic, jnp.where(top, sw[c2:], am[c2:]), 0.0)], axis=0).astype(BF16)
            for am, sw, st, ic in zip(amat, swapped, strict, incl)]
    arb = [jnp.where(ic, jnp.where(top, am[c2:], sw[c2:]), 0.0).astype(BF16)
           for am, sw, ic in zip(amat, swapped, incl)]
    ninv = [eye - t for t in tmat]
    pw = [t.astype(BF16) for t in tmat]
    pw = [jnp.dot(p, p, preferred_element_type=F32).astype(BF16) for p in pw]
    nsq = cn.bit_length() - 2
    for it in range(nsq):
        if it < nsq - 1:
            both = [jnp.dot(p, jnp.concatenate([p, n.astype(BF16)], axis=1), preferred_element_type=F32)
                    for p, n in zip(pw, ninv)]
            pw = [bo[:, :c2].astype(BF16) for bo in both]
            ninv = [n + bo[:, c2:] for n, bo in zip(ninv, both)]
        else:
            ninv = [n + jnp.dot(p, n.astype(BF16), preferred_element_type=F32) for p, n in zip(pw, ninv)]
    mv = [jnp.dot(m, v, preferred_element_type=F32) for m, v in zip(rest, vs)]
    s0 = [s_ref[z, hp] for z, hp, _, _, _ in inst]
    xs = [mm_nt(x, s) for x, s in zip(xq, s0)]
    us = [mm(n, -x[:c2] - m[:c2]) for n, x, m in zip(ninv, xs, mv)]
    ys = [x[c2:] + m[c2:] + mm(ab, u) for x, m, ab, u in zip(xs, mv, arb, us)]
    for n, (z, hp, ls, v_ref, y_ref) in enumerate(inst):
        y_ref[0, :, ls] = ys[n][:cn] + ys[n][cn:]
        uv = jnp.concatenate([us[n][:cn] + us[n][cn:], v_ref[0, :, ls]], axis=0)
        upd = mm(uv.T, bk[n])
        s_ref[z, hp] = jnp.where(same_head, s0[n] * decay_end[n] + upd, 0.0)


def _rwkv_scan(r, v, kk, lw, kd, b):
    bsz, seq, width = r.shape
    cn = RW_CHUNK
    nc = seq // cn
    fwd = pl.BlockSpec((1, cn, width), lambda b, c: (b, c, 0))
    bwd = pl.BlockSpec((1, cn, width), lambda b, c: (b, nc - 1 - c, 0))
    fwd_d = pl.BlockSpec((1, 1, cn, width), lambda b, c: (0, b, c, 0))
    bwd_d = pl.BlockSpec((1, 1, cn, width), lambda b, c: (1, b, nc - 1 - c, 0))
    out = jax.ShapeDtypeStruct((bsz, seq, width), F32)
    return pl.pallas_call(
        _rwkv_kernel,
        name="rwkv",
        grid=(bsz, nc),
        in_specs=[fwd, fwd, fwd, fwd_d, fwd_d, fwd_d, bwd, bwd, bwd, bwd_d, bwd_d, bwd_d],
        out_specs=[fwd, bwd],
        out_shape=[out, out],
        scratch_shapes=[pltpu.VMEM((2, width // LANES, LANES, LANES), F32)],
        compiler_params=_cparams(("parallel", "arbitrary")),
    )(r, v, kk, lw, kd, b, r, v, kk, lw, kd, b)


def _head_sum(x, ones, passes):
    gw = ones.shape[0]
    cols = []
    for c0 in range(0, x.shape[1], gw):
        xc = x[:, c0:c0 + gw]
        hi = xc.astype(BF16)
        out = jnp.dot(hi, ones, preferred_element_type=F32)
        if passes == 2:
            lo = (xc - hi.astype(F32)).astype(BF16)
            out = out + jnp.dot(lo, ones, preferred_element_type=F32)
        cols.append(out)
    return jnp.concatenate(cols, axis=1)


def _prep_kernel(r_ref, k_ref, v_ref, l_ref, rp_ref, kp_ref, vp_ref, lp_ref, rn_ref, kn_ref, vn_ref, ln_ref,
                 mu_ref, dbase_ref, dup_ref, ibase_ref, iup_ref, gup_ref, kkw_ref, ka_ref, rk_ref, ones_ref,
                 ro_ref, vo_ref, kko_ref, go_ref, bvo_ref, lwo_ref, kdo_ref, bo_ref, *, tm, seq):
    i = pl.program_id(0)
    bpr = seq // tm
    has_prev = (i % bpr != 0).astype(F32)
    has_next = (i % bpr != bpr - 1).astype(F32)
    rowi = lax.broadcasted_iota(jnp.int32, (tm, 1), 0)
    cw = RW_WIDTH

    def shifted(main_ref, prev_ref, next_ref, c0, c1):
        z = main_ref[...]
        zp = jnp.where(rowi == 0, prev_ref[SUBLANES - 1:SUBLANES, :] * has_prev, pltpu.roll(z, 1, 0))
        zn = jnp.where(rowi == tm - 1, next_ref[0:1, :] * has_next, pltpu.roll(z, tm - 1, 0))
        return z + mu_ref[0:1, c0:c1] * (zp - z) + mu_ref[1:2, c0:c1] * (zn - z)

    r = shifted(r_ref, rp_ref, rn_ref, 0, cw)
    k = shifted(k_ref, kp_ref, kn_ref, cw, 2 * cw)
    v = shifted(v_ref, vp_ref, vn_ref, 2 * cw, 3 * cw)
    lat = shifted(l_ref, lp_ref, ln_ref, 3 * cw, 3 * cw + LAT_PAD)
    nd, ni = 2 * DECAY_LORA, 2 * ICLR_LORA
    wl = jnp.tanh(lat[:, :nd])
    wl_hi = wl.astype(BF16)
    wl_lo = (wl - wl_hi.astype(F32)).astype(BF16)
    w_raw = (dbase_ref[...] + jnp.dot(wl_hi, dup_ref[0], preferred_element_type=F32)
             + jnp.dot(wl_lo, dup_ref[0], preferred_element_type=F32)
             + jnp.dot(wl_hi, dup_ref[1], preferred_element_type=F32))
    neg = -w_raw
    softplus = jnp.maximum(neg, 0.0) + jnp.log(1.0 + jnp.exp(-jnp.abs(neg)))
    lw = -jnp.exp(-softplus - 0.5)
    a = jax.nn.sigmoid(ibase_ref[...] + jnp.dot(lat[:, nd:nd + ni].astype(BF16), iup_ref[...],
                                                preferred_element_type=F32))
    g = jnp.dot(jax.nn.sigmoid(lat[:, nd + ni:]).astype(BF16), gup_ref[...], preferred_element_type=F32)
    ones = ones_ref[...]
    kkr = k * kkw_ref[...]
    kk = kkr / jnp.maximum(jnp.sqrt(_head_sum(kkr * kkr, ones, 2)), 1e-12)
    ka = ka_ref[...]
    kd0 = k * (1.0 + (a[:, :cw] - 1.0) * ka)
    kd1 = k * (1.0 + (a[:, cw:] - 1.0) * ka)
    bonus = _head_sum(r * (kd0 + kd1) * rk_ref[...], ones, 2)
    ro_ref[...] = r
    vo_ref[...] = v
    kko_ref[...] = kk
    go_ref[...] = g
    bvo_ref[...] = bonus * v
    lwo_ref[0] = lw[:, :cw]
    lwo_ref[1] = lw[:, cw:]
    kdo_ref[0] = kd0
    kdo_ref[1] = kd1
    bo_ref[0] = kk * a[:, :cw]
    bo_ref[1] = kk * a[:, cw:]


def _prep(proj, mu, dbase, dup, ibase, iup, gup, kkw, ka, rk, ones, seq, tm=256):
    m = proj.shape[0]
    cw = RW_WIDTH
    bpr = seq // tm
    hb = tm // SUBLANES
    nhb = m // SUBLANES
    r0 = RW_OFF // cw
    l0 = LAT_OFF // LAT_PAD
    cols = [(cw, r0), (cw, r0 + 1), (cw, r0 + 2), (LAT_PAD, l0)]
    main = [pl.BlockSpec((tm, w), lambda i, c=c: (i, c)) for w, c in cols]
    prev = [pl.BlockSpec((SUBLANES, w), lambda i, c=c: (jnp.maximum(i * hb - 1, 0), c)) for w, c in cols]
    nxt = [pl.BlockSpec((SUBLANES, w), lambda i, c=c: (jnp.minimum((i + 1) * hb, nhb - 1), c)) for w, c in cols]
    params = [mu, dbase, dup, ibase, iup, gup, kkw, ka, rk, ones]
    pspecs = [pl.BlockSpec(t.shape, lambda i, n=t.ndim: (0,) * n) for t in params]
    oblk = pl.BlockSpec((tm, cw), lambda i: (i, 0))
    dblk = pl.BlockSpec((2, tm, cw), lambda i: (0, i, 0))
    one = jax.ShapeDtypeStruct((m, cw), F32)
    two = jax.ShapeDtypeStruct((2, m, cw), F32)
    return pl.pallas_call(
        functools.partial(_prep_kernel, tm=tm, seq=seq),
        name="rwkv_prep",
        grid=(m // tm,),
        in_specs=main + prev + nxt + pspecs,
        out_specs=[oblk] * 5 + [dblk] * 3,
        out_shape=[one] * 5 + [two] * 3,
        compiler_params=_cparams(("parallel",)),
    )(*([proj] * 12), *params)


def _mix_kernel(att_ref, yf_ref, yb_ref, bv_ref, g_ref, lnw_ref, lnb_ref, ones_ref, ga_ref, gr_ref, x_ref, gt_ref,
                gain_ref, wa_ref, wr_ref, wo_ref, o_ref):
    ones = ones_ref[...]
    y = yf_ref[...] + yb_ref[...]
    mu = _head_sum(y, ones, 1) * (1.0 / RW_HEAD)
    yc = y - mu
    var = _head_sum(yc * yc, ones, 1) * (1.0 / RW_HEAD)
    rw = ((yc * lax.rsqrt(var + GN_EPS)) * lnw_ref[...] + lnb_ref[...] + bv_ref[...]) * g_ref[...]
    pa = jnp.dot(att_ref[...], wa_ref[...], preferred_element_type=F32)
    pr = jnp.dot(rw.astype(BF16), wr_ref[...], preferred_element_type=F32)
    merged = jax.nn.sigmoid(ga_ref[...]) * pa + jax.nn.sigmoid(gr_ref[...]) * pr
    mix = jnp.dot(merged.astype(BF16), wo_ref[...], preferred_element_type=F32)
    o_ref[...] = x_ref[...] + gt_ref[0] * _rms(mix, gain_ref[...])


def _mix(att, yf, yb, bv, g, lnw, lnb, ones, proj, x, gt, gain, wa, wr, wo, seq, tm=256):
    m, d = x.shape
    cw = RW_WIDTH
    bpr = seq // tm
    gblk = GATE_OFF // d
    const = lambda i: (0, 0)
    resident = lambda t: pl.BlockSpec(t.shape, const, pipeline_mode=pl.Buffered(1))
    return pl.pallas_call(
        _mix_kernel,
        name="mix",
        grid=(m // tm,),
        in_specs=[pl.BlockSpec((tm, att.shape[1]), lambda i: (i, 0)),
                  pl.BlockSpec((tm, cw), lambda i: (i, 0)),
                  pl.BlockSpec((tm, cw), lambda i: (i, 0)),
                  pl.BlockSpec((tm, cw), lambda i: (i, 0)),
                  pl.BlockSpec((tm, cw), lambda i: (i, 0)),
                  pl.BlockSpec((1, cw), const),
                  pl.BlockSpec((1, cw), const),
                  resident(ones),
                  pl.BlockSpec((tm, d), lambda i: (i, gblk)),
                  pl.BlockSpec((tm, d), lambda i: (i, gblk + 1)),
                  pl.BlockSpec((tm, d), lambda i: (i, 0)),
                  pl.BlockSpec((1, 1, d), lambda i: (i // bpr, 0, 0)),
                  pl.BlockSpec((1, d), const),
                  resident(wa), resident(wr), resident(wo)],
        out_specs=pl.BlockSpec((tm, d), lambda i: (i, 0)),
        out_shape=jax.ShapeDtypeStruct((m, d), F32),
        compiler_params=_cparams(("parallel",)),
    )(att, yf, yb, bv, g, lnw, lnb, ones, proj, proj, x, gt, gain, wa, wr, wo)


def _ffn_kernel(x_ref, xp_ref, xn_ref, gpre_ref, sc_ref, sh_ref, wg_ref, wu_ref, cw_ref, cb_ref, wd_ref,
                gt_ref, gpost_ref, o_ref, h_ref, *, tm, seq):
    i = pl.program_id(0)
    j = pl.program_id(1)
    halo = SUBLANES
    bpr = seq // tm

    @pl.when(j == 0)
    def _():
        gain = gpre_ref[...]
        sc = 1.0 + sc_ref[0]
        sh = sh_ref[0]
        has_prev = (i % bpr != 0).astype(F32)
        has_next = (i % bpr != bpr - 1).astype(F32)
        h_ref[0:halo, :] = ((_rms(xp_ref[...], gain) * sc + sh) * has_prev).astype(BF16)
        h_ref[halo:halo + tm, :] = (_rms(x_ref[...], gain) * sc + sh).astype(BF16)
        h_ref[halo + tm:, :] = ((_rms(xn_ref[...], gain) * sc + sh) * has_next).astype(BF16)
        o_ref[...] = jnp.zeros_like(o_ref)

    ext = tm + 2 * halo
    gate = jnp.dot(h_ref[...], wg_ref[...], preferred_element_type=F32)
    prev = pltpu.roll(gate, 1, 0)[halo:halo + tm]
    nxt = pltpu.roll(gate, ext - 1, 0)[halo:halo + tm]
    u = cw_ref[0:1, :] * prev + cw_ref[1:2, :] * gate[halo:halo + tm] + cw_ref[2:3, :] * nxt + cb_ref[...]
    up = jnp.dot(h_ref[halo:halo + tm, :], wu_ref[...], preferred_element_type=F32)
    act = jax.nn.gelu(u, approximate=True) * up
    o_ref[...] += jnp.dot(act.astype(BF16), wd_ref[...], preferred_element_type=F32)

    @pl.when(j == pl.num_programs(1) - 1)
    def _():
        o_ref[...] = x_ref[...] + gt_ref[0] * _rms(o_ref[...], gpost_ref[...])


def _ffn(x, gpre, sc, sh, wg, wu, cw, cb, wd, gt, gpost, seq, tm=512, tf=FF_TILE):
    m, d = x.shape
    f = wg.shape[1]
    bpr = seq // tm
    hb = tm // SUBLANES
    nhb = m // SUBLANES
    const = lambda i, j: (0, 0)
    bidx = lambda i, j: (i // bpr, 0, 0)
    return pl.pallas_call(
        functools.partial(_ffn_kernel, tm=tm, seq=seq),
        name="ffn",
        grid=(m // tm, f // tf),
        in_specs=[pl.BlockSpec((tm, d), lambda i, j: (i, 0)),
                  pl.BlockSpec((SUBLANES, d), lambda i, j: (jnp.maximum(i * hb - 1, 0), 0)),
                  pl.BlockSpec((SUBLANES, d), lambda i, j: (jnp.minimum((i + 1) * hb, nhb - 1), 0)),
                  pl.BlockSpec((1, d), const),
                  pl.BlockSpec((1, 1, d), bidx),
                  pl.BlockSpec((1, 1, d), bidx),
                  pl.BlockSpec((d, tf), lambda i, j: (0, j)),
                  pl.BlockSpec((d, tf), lambda i, j: (0, j)),
                  pl.BlockSpec((SUBLANES, tf), lambda i, j: (0, j)),
                  pl.BlockSpec((1, tf), lambda i, j: (0, j)),
                  pl.BlockSpec((tf, d), lambda i, j: (j, 0)),
                  pl.BlockSpec((1, 1, d), bidx),
                  pl.BlockSpec((1, d), const)],
        out_specs=pl.BlockSpec((tm, d), lambda i, j: (i, 0)),
        out_shape=jax.ShapeDtypeStruct((m, d), F32),
        scratch_shapes=[pltpu.VMEM((tm + 2 * SUBLANES, d), BF16)],
        compiler_params=_cparams(("parallel", "arbitrary")),
    )(x, x, x, gpre, sc, sh, wg, wu, cw, cb, wd, gt, gpost)


def _rope_tables(seq):
    half = ATT_HEAD_DIM // 2
    inv_freq = 1.0 / (ROPE_THETA ** (jnp.arange(half, dtype=F32) / half))
    ang = jnp.arange(seq, dtype=F32)[:, None] * inv_freq[None, :]
    cos = jnp.cos(ang)
    sin = jnp.sin(ang)
    return jnp.concatenate([cos, cos], axis=-1), jnp.concatenate([-sin, sin], axis=-1)


def _trunk(x, ada, p):
    bsz, seq, d = x.shape
    m = bsz * seq
    sh1, sc1, gt1, sh2, sc2, gt2 = [t[:, None, :] for t in jnp.split(ada, 6, axis=-1)]
    x2 = x.reshape(m, d)
    proj = _inproj(x2, p['ln_mix_pre'], sc1, sh1, p['w_in'], seq)
    proj3 = proj.reshape(bsz, seq, N_IN_PAD)
    cos, sin = _rope_tables(seq)
    att = _attention(proj3, cos, sin)
    r, v, kk, g, bv, lw, kd, b = _prep(proj, p['shift_mu'], p['decay_base'], p['decay_up'], p['iclr_base'],
                                       p['iclr_up'], p['gate_up'], p['k_k'], p['k_a'], p['r_k'], p['head_ones'], seq)
    per_seq = lambda t: t.reshape(t.shape[:-2] + (bsz, seq, RW_WIDTH))
    yf, yb = _rwkv_scan(per_seq(r), per_seq(v), per_seq(kk), per_seq(lw), per_seq(kd), per_seq(b))
    x1 = _mix(att.reshape(m, ATT_OUT), yf.reshape(m, RW_WIDTH), yb.reshape(m, RW_WIDTH), bv, g,
              p['lnx_w'], p['lnx_b'], p['head_ones'],
              proj, x2, gt1, p['ln_mix_post'], p['w_att_branch'], p['w_rwkv_branch'], p['w_out'], seq)
    y = _ffn(x1, p['ln_ffn_pre'], sc2, sh2, p['w_ffn_gate'], p['w_ffn_up'], p['ffn_conv_w'], p['ffn_conv_b'],
             p['w_ffn_down'], gt2, p['ln_ffn_post'], seq)
    return y.reshape(bsz, seq, d)


def kernel(x_prompt, x_sample, c_prompt, c_sample, ln_mix_pre, ln_mix_post, ln_ffn_pre, ln_ffn_post,
           w_ada, b_ada, w_in, shift_mu, decay_base, decay_up, iclr_base, iclr_up, gate_up,
           k_k, k_a, r_k, lnx_w, lnx_b, w_att_branch, w_rwkv_branch, w_out,
           w_ffn_gate, w_ffn_up, ffn_conv_w, ffn_conv_b, w_ffn_down):
    depth = w_in.shape[0]
    nb_p = c_prompt.shape[0]
    nb_s = c_sample.shape[0]
    c_all = jnp.concatenate([c_prompt, c_sample], axis=0)
    c_all = jnp.pad(c_all, ((0, (-c_all.shape[0]) % SUBLANES), (0, 0)))
    fpad = D_FF_PAD - D_FF
    xp, xs = x_prompt, x_sample
    lane_head = jnp.arange(MXU_DIM_V7X) // RW_HEAD
    head_ones = (lane_head[:, None] == lane_head[None, :]).astype(BF16)
    zdl = jnp.zeros((DECAY_LORA, RW_WIDTH), F32)
    zil = jnp.zeros((ICLR_LORA, RW_WIDTH), F32)
    for l in range(depth):
        att_end = 3 * ATT_WIDTH
        rw_end = att_end + RW_COLS
        w_in_b = w_in[l].astype(BF16)
        w_in_l = jnp.concatenate([w_in_b[:, rw_end:], w_in_b[:, att_end:rw_end],
                                  jnp.zeros((D_MODEL, LAT_PAD - LAT_COLS), BF16), w_in_b[:, :att_end]], axis=1)
        dup = jnp.block([[decay_up[l][0], zdl], [zdl, decay_up[l][1]]])
        dup_hi = dup.astype(BF16)
        p = dict(
            ln_mix_pre=ln_mix_pre[l][None], ln_mix_post=ln_mix_post[l][None],
            ln_ffn_pre=ln_ffn_pre[l][None], ln_ffn_post=ln_ffn_post[l][None],
            w_in=w_in_l, head_ones=head_ones,
            shift_mu=jnp.pad(shift_mu[l], ((0, 0), (0, RW_COLS_PAD - RW_COLS))),
            decay_base=decay_base[l].reshape(1, 2 * RW_WIDTH),
            decay_up=jnp.stack([dup_hi, (dup - dup_hi.astype(F32)).astype(BF16)]),
            iclr_base=iclr_base[l].reshape(1, 2 * RW_WIDTH),
            iclr_up=jnp.block([[iclr_up[l][0], zil], [zil, iclr_up[l][1]]]).astype(BF16),
            gate_up=jnp.pad(gate_up[l], ((0, LAT_PAD - LAT_COLS), (0, 0))).astype(BF16),
            k_k=k_k[l][None], k_a=k_a[l][None], r_k=r_k[l].reshape(1, RW_WIDTH),
            lnx_w=lnx_w[l][None], lnx_b=lnx_b[l][None],
            w_att_branch=w_att_branch[l].astype(BF16), w_rwkv_branch=w_rwkv_branch[l].astype(BF16),
            w_out=w_out[l].astype(BF16),
            w_ffn_gate=jnp.pad(w_ffn_gate[l], ((0, 0), (0, fpad))).astype(BF16),
            w_ffn_up=jnp.pad(w_ffn_up[l], ((0, 0), (0, fpad))).astype(BF16),
            ffn_conv_w=jnp.pad(ffn_conv_w[l], ((0, SUBLANES - 3), (0, fpad))),
            ffn_conv_b=jnp.pad(ffn_conv_b[l], ((0, fpad),))[None],
            w_ffn_down=jnp.pad(w_ffn_down[l], ((0, fpad), (0, 0))).astype(BF16),
        )
        ada = _ada(c_all, w_ada[l], b_ada[l][None])
        xp = _trunk(xp, ada[:nb_p], p)
        xs = _trunk(xs, ada[nb_p:nb_p + nb_s], p)
    return (xp, xs)
```

```python
import functools
import math

import jax
import jax.numpy as jnp
from jax import lax
from jax.experimental import pallas as pl
from jax.experimental.pallas import tpu as pltpu

F32 = jnp.float32
BF16 = jnp.bfloat16

D_MODEL = 2048
ATT_GROUPS = ((128, 1), (512, 4), (2048, 16))
ATT_HEADS_PER_GROUP = 4
ATT_HEAD_DIM = 128
ATT_HEADS = ATT_HEADS_PER_GROUP * len(ATT_GROUPS)
ATT_WIDTH = ATT_HEADS * ATT_HEAD_DIM
ATT_OUT = ATT_HEADS_PER_GROUP * ATT_HEAD_DIM
ROPE_THETA = 10000.0
RW_HEAD = 64
RW_WIDTH = D_MODEL // 2
RW_HEADS = RW_WIDTH // RW_HEAD
DECAY_LORA = 64
ICLR_LORA = 64
GATE_LORA = 160
RW_COLS = 3 * RW_WIDTH + 2 * DECAY_LORA + 2 * ICLR_LORA + GATE_LORA
N_IN = 3 * ATT_WIDTH + RW_COLS + 2 * D_MODEL
D_FF = ((8 * D_MODEL // 3 + 127) // 128) * 128
RMS_EPS = 1e-6
GN_EPS = 64e-5
NEG_INF = -1e30
DECAY_SCALE = math.exp(-0.5)

LANES = 128
SUBLANES = 8
MXU_DIM_V7X = 256
VMEM_BYTES_V7X = 64 * 1024 * 1024
VMEM_LIMIT = VMEM_BYTES_V7X - 8 * 1024 * 1024

GATE_OFF = 0
RW_OFF = GATE_OFF + 2 * D_MODEL
LAT_OFF = RW_OFF + 3 * RW_WIDTH
LAT_COLS = RW_COLS - 3 * RW_WIDTH
LAT_PAD = 512
ATT_OFF = LAT_OFF + LAT_PAD
N_IN_PAD = ATT_OFF + 3 * ATT_WIDTH
RW_COLS_PAD = 3 * RW_WIDTH + LAT_PAD
FF_TILE = 512
D_FF_PAD = -(-D_FF // FF_TILE) * FF_TILE
RW_CHUNK = 64
RW_ROWS_PER_STEP = 2
ATT_SPAN = 64


def _cparams(sem):
    return pltpu.CompilerParams(dimension_semantics=sem, vmem_limit_bytes=VMEM_LIMIT)


def _rms(x, gain):
    return x * lax.rsqrt(jnp.mean(x * x, axis=-1, keepdims=True) + RMS_EPS) * gain


def _ada_kernel(c_ref, w_ref, b_ref, o_ref):
    c = c_ref[...]
    s = c * jax.nn.sigmoid(c)
    o_ref[...] = jnp.dot(s.astype(BF16), w_ref[...].astype(BF16),
                         preferred_element_type=F32) + b_ref[...]


def _ada(c, w, b, tn=1024):
    m, k = c.shape
    n = w.shape[1]
    return pl.pallas_call(
        _ada_kernel,
        name="ada",
        grid=(n // tn,),
        in_specs=[pl.BlockSpec((m, k), lambda j: (0, 0)),
                  pl.BlockSpec((k, tn), lambda j: (0, j)),
                  pl.BlockSpec((1, tn), lambda j: (0, j))],
        out_specs=pl.BlockSpec((m, tn), lambda j: (0, j)),
        out_shape=jax.ShapeDtypeStruct((m, n), F32),
        compiler_params=_cparams(("parallel",)),
    )(c, w, b)


def _inproj_kernel(x_ref, g_ref, sc_ref, sh_ref, w_ref, o_ref, h_ref):
    @pl.when(pl.program_id(1) == 0)
    def _():
        h = _rms(x_ref[...], g_ref[...]) * (1.0 + sc_ref[0]) + sh_ref[0]
        h_ref[...] = h.astype(BF16)

    o_ref[...] = jnp.dot(h_ref[...], w_ref[...], preferred_element_type=F32)


def _inproj(x, gain, sc, sh, w, seq, tm=1024, tn=1536):
    m, k = x.shape
    n = w.shape[1]
    bpr = seq // tm
    return pl.pallas_call(
        _inproj_kernel,
        name="inproj",
        grid=(m // tm, n // tn),
        in_specs=[pl.BlockSpec((tm, k), lambda i, j: (i, 0)),
                  pl.BlockSpec((1, k), lambda i, j: (0, 0)),
                  pl.BlockSpec((1, 1, k), lambda i, j: (i // bpr, 0, 0)),
                  pl.BlockSpec((1, 1, k), lambda i, j: (i // bpr, 0, 0)),
                  pl.BlockSpec((k, tn), lambda i, j: (0, j))],
        out_specs=pl.BlockSpec((tm, tn), lambda i, j: (i, j)),
        out_shape=jax.ShapeDtypeStruct((m, n), F32),
        scratch_shapes=[pltpu.VMEM((tm, k), BF16)],
        compiler_params=_cparams(("parallel", "arbitrary")),
    )(x, gain, sc, sh, w)


def _attn_kernel(q_ref, k_ref, v_ref, cos_ref, sin_ref, o_ref, qr_ref, kr_ref, og_ref, lse_ref, *, seq):
    gid = pl.program_id(2)
    rows = 512
    half = ATT_HEAD_DIM // 2

    def rot(i, _):
        sl = pl.ds(pl.multiple_of(i * rows, rows), rows)
        cos = cos_ref[sl, :]
        sin = sin_ref[sl, :]
        q = q_ref[0, sl, :]
        k = k_ref[0, sl, :]
        qr_ref[sl, :] = (q * cos + pltpu.roll(q, half, 1) * sin) * (ATT_HEAD_DIM ** -0.5)
        kr_ref[sl, :] = k * cos + pltpu.roll(k, half, 1) * sin
        return 0

    lax.fori_loop(0, seq // rows, rot, 0)

    def group(gi, dil):
        cls_len = seq // dil
        bq = min(128, cls_len)
        bk = min(bq + 2 * ATT_SPAN, cls_len)
        nqb = cls_len // bq
        rel0 = (lax.broadcasted_iota(jnp.int32, (bq, bk), 0)
                - lax.broadcasted_iota(jnp.int32, (bq, bk), 1))
        ones = jnp.ones((bk, ATT_HEAD_DIM), BF16)

        def body(it, _):
            cls = it // nqb
            p0 = (it % nqb) * bq
            ks = jnp.clip(p0 - ATT_SPAN, 0, cls_len - bk)
            if dil == 1:
                rq = pl.ds(p0, bq)
                rk = pl.ds(ks, bk)
            else:
                rq = pl.ds(cls + dil * p0, bq, stride=dil)
                rk = pl.ds(cls + dil * ks, bk, stride=dil)
            qb = qr_ref[rq, :].astype(BF16)
            kb = kr_ref[rk, :].astype(BF16)
            vb = v_ref[0, rk, :].astype(BF16)
            s = lax.dot_general(qb, kb, (((1,), (1,)), ((), ())), preferred_element_type=F32)
            band = jnp.abs(rel0 + (p0 - ks)) <= ATT_SPAN
            s = jnp.where(band, s, NEG_INF)
            m = jnp.max(s, axis=-1, keepdims=True)
            p = jnp.exp(s - m)
            pv = jnp.dot(p.astype(BF16), jnp.concatenate([vb, ones], axis=1), preferred_element_type=F32)
            l = pv[:, ATT_HEAD_DIM:]
            og_ref[gi, rq, :] = pv[:, :ATT_HEAD_DIM] / l
            lse_ref[gi, rq, :] = m + jnp.log(l)
            return 0

        lax.fori_loop(0, dil * nqb, body, 0, unroll=4)

    for gi, (_, dil) in enumerate(ATT_GROUPS):
        @pl.when(gid == gi)
        def _(gi=gi, dil=dil):
            group(gi, dil)

    @pl.when(gid == len(ATT_GROUPS) - 1)
    def _():
        def comb(i, _):
            sl = pl.ds(pl.multiple_of(i * rows, rows), rows)
            l0 = lse_ref[0, sl, :]
            l1 = lse_ref[1, sl, :]
            l2 = lse_ref[2, sl, :]
            mx = jnp.maximum(jnp.maximum(l0, l1), l2)
            w0 = jnp.exp(l0 - mx)
            w1 = jnp.exp(l1 - mx)
            w2 = jnp.exp(l2 - mx)
            num = w0 * og_ref[0, sl, :] + w1 * og_ref[1, sl, :] + w2 * og_ref[2, sl, :]
            o_ref[0, sl, :] = (num / (w0 + w1 + w2)).astype(o_ref.dtype)
            return 0

        lax.fori_loop(0, seq // rows, comb, 0)


def _attention(proj3, cos, sin):
    bsz, seq, _ = proj3.shape
    hpg = ATT_HEADS_PER_GROUP
    ng = len(ATT_GROUPS)
    blk = (1, seq, ATT_HEAD_DIM)
    q0 = ATT_OFF // ATT_HEAD_DIM
    return pl.pallas_call(
        functools.partial(_attn_kernel, seq=seq),
        name="attn",
        grid=(bsz, hpg, ng),
        in_specs=[pl.BlockSpec(blk, lambda b, s, g: (b, 0, q0 + g * hpg + s)),
                  pl.BlockSpec(blk, lambda b, s, g: (b, 0, q0 + ATT_HEADS + g * hpg + s)),
                  pl.BlockSpec(blk, lambda b, s, g: (b, 0, q0 + 2 * ATT_HEADS + g * hpg + s)),
                  pl.BlockSpec((seq, ATT_HEAD_DIM), lambda b, s, g: (0, 0)),
                  pl.BlockSpec((seq, ATT_HEAD_DIM), lambda b, s, g: (0, 0))],
        out_specs=pl.BlockSpec(blk, lambda b, s, g: (b, 0, s)),
        out_shape=jax.ShapeDtypeStruct((bsz, seq, ATT_OUT), BF16),
        scratch_shapes=[pltpu.VMEM((seq, ATT_HEAD_DIM), F32),
                        pltpu.VMEM((seq, ATT_HEAD_DIM), F32),
                        pltpu.VMEM((ng, seq, ATT_HEAD_DIM), F32),
                        pltpu.VMEM((ng, seq, ATT_HEAD_DIM), F32)],
        compiler_params=_cparams(("parallel", "parallel", "arbitrary")),
    )(proj3, proj3, proj3, cos, sin)


def _rwkv_kernel(rf_ref, vf_ref, kkf_ref, lwf_ref, kdf_ref, bf_ref, rb_ref, vb_ref, kkb_ref, lwb_ref, kdb_ref,
                 bb_ref, yf_ref, yb_ref, s_ref):
    cn = RW_CHUNK
    c2 = 2 * cn

    @pl.when(pl.program_id(1) == 0)
    def _():
        s_ref[...] = jnp.zeros_like(s_ref)

    ti = lax.broadcasted_iota(jnp.int32, (cn, cn), 0)
    si = lax.broadcasted_iota(jnp.int32, (cn, cn), 1)
    t2 = lax.broadcasted_iota(jnp.int32, (c2, c2), 0)
    s2 = lax.broadcasted_iota(jnp.int32, (c2, c2), 1)
    eye = (t2 == s2).astype(F32)
    same_blk = (t2 // cn) == (s2 // cn)
    top = t2 < cn
    lane = lax.broadcasted_iota(jnp.int32, (1, LANES), 1)
    first = lane < RW_HEAD
    same_head = ((lax.broadcasted_iota(jnp.int32, (LANES, LANES), 0) // RW_HEAD)
                 == (lax.broadcasted_iota(jnp.int32, (LANES, LANES), 1) // RW_HEAD))

    def stack(x):
        return jnp.concatenate([jnp.where(first, x, 0.0), jnp.where(first, 0.0, x)], axis=0)

    def mm(a, b):
        return jnp.dot(a.astype(BF16), b.astype(BF16), preferred_element_type=F32)

    def mm_nt(a, b):
        return lax.dot_general(a.astype(BF16), b.astype(BF16), (((1,), (1,)), ((), ())),
                               preferred_element_type=F32)

    lanes = [slice(hp * LANES, (hp + 1) * LANES) for hp in range(RW_WIDTH // LANES)]
    inst = []
    xq, vs, bk, decay_end, amat, strict, incl = [], [], [], [], [], [], []
    for z, (r_ref, v_ref, kk_ref, lw_ref, kd_ref, b_ref, y_ref) in enumerate(
            ((rf_ref, vf_ref, kkf_ref, lwf_ref, kdf_ref, bf_ref, yf_ref),
             (rb_ref, vb_ref, kkb_ref, lwb_ref, kdb_ref, bb_ref, yb_ref))):
        rev = z == 1
        tri = ((si >= ti) if rev else (si <= ti)).astype(BF16)
        order = (t2 % cn - s2 % cn) if rev else (s2 % cn - t2 % cn)
        strict_z = (order < 0) & same_blk
        incl_z = (order <= 0) & same_blk
        for row in range(r_ref.shape[0]):
            lw_all = lw_ref[0, row]
            lw_hi = lw_all.astype(BF16)
            lw_lo = (lw_all - lw_hi.astype(F32)).astype(BF16)
            g_all = (jnp.dot(tri, lw_hi, preferred_element_type=F32)
                     + jnp.dot(tri, lw_lo, preferred_element_type=F32))
            for hp, ls in enumerate(lanes):
                r = r_ref[row, :, ls]
                v = v_ref[row, :, ls]
                kk = kk_ref[row, :, ls]
                kd = kd_ref[0, row, :, ls]
                b = b_ref[0, row, :, ls]
                g = g_all[:, ls]
                g_end = g[0:1] if rev else g[cn - 1:cn]
                e_neg = jnp.exp(-g)
                e_end = jnp.exp(g_end - g)
                x = jnp.concatenate([stack(kk * jnp.exp(g - lw_all[:, ls])), stack(r * jnp.exp(g))],
                                    axis=0).astype(BF16)
                y = jnp.concatenate([b * e_neg, kd * e_neg], axis=0).astype(BF16)
                inst.append((z, row, hp, ls, v_ref, y_ref))
                strict.append(strict_z)
                incl.append(incl_z)
                xq.append(x)
                vs.append(stack(v).astype(BF16))
                bk.append(jnp.concatenate([b * e_end, kd * e_end], axis=0).astype(BF16))
                decay_end.append(jnp.exp(g_end))
                amat.append(lax.dot_general(x, y, (((1,), (1,)), ((), ())), preferred_element_type=F32))
    swapped = [pltpu.roll(am, cn, 1) for am in amat]
    tmat = [jnp.where(st, jnp.where(top, am[:c2], sw[:c2]), 0.0) for am, sw, st in zip(amat, swapped, strict)]
    rest = [jnp.concatenate([jnp.where(st, jnp.where(top, sw[:c2], am[:c2]), 0.0),
                             jnp.where(ic, jnp.where(top, sw[c2:], am[c2:]), 0.0)], axis=0).astype(BF16)
            for am, sw, st, ic in zip(amat, swapped, strict, incl)]
    arb = [jnp.where(ic, jnp.where(top, am[c2:], sw[c2:]), 0.0).astype(BF16)
           for am, sw, ic in zip(amat, swapped, incl)]
    ninv = [eye - t for t in tmat]
    pw = [t.astype(BF16) for t in tmat]
    pw = [jnp.dot(p, p, preferred_element_type=F32).astype(BF16) for p in pw]
    nsq = cn.bit_length() - 2
    for it in range(nsq):
        if it < nsq - 1:
            both = [jnp.dot(p, jnp.concatenate([p, n.astype(BF16)], axis=1), preferred_element_type=F32)
                    for p, n in zip(pw, ninv)]
            pw = [bo[:, :c2].astype(BF16) for bo in both]
            ninv = [n + bo[:, c2:] for n, bo in zip(ninv, both)]
        else:
            ninv = [n + jnp.dot(p, n.astype(BF16), preferred_element_type=F32) for p, n in zip(pw, ninv)]
    mv = [jnp.dot(m, v, preferred_element_type=F32) for m, v in zip(rest, vs)]
    s0 = [s_ref[z, row, hp] for z, row, hp, _, _, _ in inst]
    xs = [mm_nt(x, s) for x, s in zip(xq, s0)]
    us = [mm(n, -x[:c2] - m[:c2]) for n, x, m in zip(ninv, xs, mv)]
    ys = [x[c2:] + m[c2:] + mm(ab, u) for x, m, ab, u in zip(xs, mv, arb, us)]
    for n, (z, row, hp, ls, v_ref, y_ref) in enumerate(inst):
        y_ref[row, :, ls] = ys[n][:cn] + ys[n][cn:]
        uv = jnp.concatenate([us[n][:cn] + us[n][cn:], v_ref[row, :, ls]], axis=0)
        upd = mm(uv.T, bk[n])
        s_ref[z, row, hp] = jnp.where(same_head, s0[n] * decay_end[n] + upd, 0.0)


def _rwkv_scan(r, v, kk, lw, kd, b):
    bsz, seq, width = r.shape
    cn = RW_CHUNK
    nc = seq // cn
    rows = RW_ROWS_PER_STEP
    assert bsz % rows == 0
    fwd = pl.BlockSpec((rows, cn, width), lambda b, c: (b, c, 0))
    bwd = pl.BlockSpec((rows, cn, width), lambda b, c: (b, nc - 1 - c, 0))
    fwd_d = pl.BlockSpec((1, rows, cn, width), lambda b, c: (0, b, c, 0))
    bwd_d = pl.BlockSpec((1, rows, cn, width), lambda b, c: (1, b, nc - 1 - c, 0))
    out = jax.ShapeDtypeStruct((bsz, seq, width), F32)
    return pl.pallas_call(
        _rwkv_kernel,
        name="rwkv",
        grid=(bsz // rows, nc),
        in_specs=[fwd, fwd, fwd, fwd_d, fwd_d, fwd_d, bwd, bwd, bwd, bwd_d, bwd_d, bwd_d],
        out_specs=[fwd, bwd],
        out_shape=[out, out],
        scratch_shapes=[pltpu.VMEM((2, rows, width // LANES, LANES, LANES), F32)],
        compiler_params=_cparams(("parallel", "arbitrary")),
    )(r, v, kk, lw, kd, b, r, v, kk, lw, kd, b)


def _head_sum(x, ones, passes):
    gw = ones.shape[0]
    cols = []
    for c0 in range(0, x.shape[1], gw):
        xc = x[:, c0:c0 + gw]
        hi = xc.astype(BF16)
        out = jnp.dot(hi, ones, preferred_element_type=F32)
        if passes == 2:
            lo = (xc - hi.astype(F32)).astype(BF16)
            out = out + jnp.dot(lo, ones, preferred_element_type=F32)
        cols.append(out)
    return jnp.concatenate(cols, axis=1)


def _prep_kernel(r_ref, k_ref, v_ref, l_ref, rp_ref, kp_ref, vp_ref, lp_ref, rn_ref, kn_ref, vn_ref, ln_ref,
                 mu_ref, dbase_ref, dup_ref, ibase_ref, iup_ref, gup_ref, kkw_ref, ka_ref, rk_ref, ones_ref,
                 ro_ref, vo_ref, kko_ref, go_ref, bvo_ref, lwo_ref, kdo_ref, bo_ref, *, tm, seq):
    i = pl.program_id(0)
    bpr = seq // tm
    has_prev = (i % bpr != 0).astype(F32)
    has_next = (i % bpr != bpr - 1).astype(F32)
    rowi = lax.broadcasted_iota(jnp.int32, (tm, 1), 0)
    cw = RW_WIDTH

    def shifted(main_ref, prev_ref, next_ref, c0, c1):
        z = main_ref[...]
        zp = jnp.where(rowi == 0, prev_ref[SUBLANES - 1:SUBLANES, :] * has_prev, pltpu.roll(z, 1, 0))
        zn = jnp.where(rowi == tm - 1, next_ref[0:1, :] * has_next, pltpu.roll(z, tm - 1, 0))
        return z + mu_ref[0:1, c0:c1] * (zp - z) + mu_ref[1:2, c0:c1] * (zn - z)

    r = shifted(r_ref, rp_ref, rn_ref, 0, cw)
    k = shifted(k_ref, kp_ref, kn_ref, cw, 2 * cw)
    v = shifted(v_ref, vp_ref, vn_ref, 2 * cw, 3 * cw)
    lat = shifted(l_ref, lp_ref, ln_ref, 3 * cw, 3 * cw + LAT_PAD)
    nd, ni = 2 * DECAY_LORA, 2 * ICLR_LORA
    wl = jnp.tanh(lat[:, :nd])
    wl_hi = wl.astype(BF16)
    wl_lo = (wl - wl_hi.astype(F32)).astype(BF16)
    w_raw = (dbase_ref[...] + jnp.dot(wl_hi, dup_ref[0], preferred_element_type=F32)
             + jnp.dot(wl_lo, dup_ref[0], preferred_element_type=F32)
             + jnp.dot(wl_hi, dup_ref[1], preferred_element_type=F32))
    lw = jax.nn.sigmoid(w_raw) * (-DECAY_SCALE)
    a = jax.nn.sigmoid(ibase_ref[...] + jnp.dot(lat[:, nd:nd + ni].astype(BF16), iup_ref[...],
                                                preferred_element_type=F32))
    g = jnp.dot(jax.nn.sigmoid(lat[:, nd + ni:]).astype(BF16), gup_ref[...], preferred_element_type=F32)
    ones = ones_ref[...]
    kkr = k * kkw_ref[...]
    kk = kkr / jnp.maximum(jnp.sqrt(_head_sum(kkr * kkr, ones, 2)), 1e-12)
    ka = ka_ref[...]
    k_fix = k * (1.0 - ka)
    k_var = k * ka
    kd0 = k_fix + k_var * a[:, :cw]
    kd1 = k_fix + k_var * a[:, cw:]
    bonus = _head_sum(r * (kd0 + kd1) * rk_ref[...], ones, 2)
    ro_ref[...] = r
    vo_ref[...] = v
    kko_ref[...] = kk
    go_ref[...] = g
    bvo_ref[...] = bonus * v
    lwo_ref[0] = lw[:, :cw]
    lwo_ref[1] = lw[:, cw:]
    kdo_ref[0] = kd0
    kdo_ref[1] = kd1
    bo_ref[0] = kk * a[:, :cw]
    bo_ref[1] = kk * a[:, cw:]


def _prep(proj, mu, dbase, dup, ibase, iup, gup, kkw, ka, rk, ones, seq, tm=256):
    m = proj.shape[0]
    cw = RW_WIDTH
    bpr = seq // tm
    hb = tm // SUBLANES
    nhb = m // SUBLANES
    r0 = RW_OFF // cw
    l0 = LAT_OFF // LAT_PAD
    cols = [(cw, r0), (cw, r0 + 1), (cw, r0 + 2), (LAT_PAD, l0)]
    main = [pl.BlockSpec((tm, w), lambda i, c=c: (i, c)) for w, c in cols]
    prev = [pl.BlockSpec((SUBLANES, w), lambda i, c=c: (jnp.maximum(i * hb - 1, 0), c)) for w, c in cols]
    nxt = [pl.BlockSpec((SUBLANES, w), lambda i, c=c: (jnp.minimum((i + 1) * hb, nhb - 1), c)) for w, c in cols]
    params = [mu, dbase, dup, ibase, iup, gup, kkw, ka, rk, ones]
    pspecs = [pl.BlockSpec(t.shape, lambda i, n=t.ndim: (0,) * n) for t in params]
    oblk = pl.BlockSpec((tm, cw), lambda i: (i, 0))
    dblk = pl.BlockSpec((2, tm, cw), lambda i: (0, i, 0))
    one = jax.ShapeDtypeStruct((m, cw), F32)
    two = jax.ShapeDtypeStruct((2, m, cw), F32)
    return pl.pallas_call(
        functools.partial(_prep_kernel, tm=tm, seq=seq),
        name="rwkv_prep",
        grid=(m // tm,),
        in_specs=main + prev + nxt + pspecs,
        out_specs=[oblk] * 5 + [dblk] * 3,
        out_shape=[one] * 5 + [two] * 3,
        compiler_params=_cparams(("parallel",)),
    )(*([proj] * 12), *params)


def _mix_kernel(att_ref, yf_ref, yb_ref, bv_ref, g_ref, lnw_ref, lnb_ref, ones_ref, ga_ref, gr_ref, x_ref, gt_ref,
                gain_ref, wa_ref, wr_ref, wo_ref, o_ref):
    ones = ones_ref[...]
    y = yf_ref[...] + yb_ref[...]
    mu = _head_sum(y, ones, 1) * (1.0 / RW_HEAD)
    yc = y - mu
    var = _head_sum(yc * yc, ones, 1) * (1.0 / RW_HEAD)
    rw = ((yc * lax.rsqrt(var + GN_EPS)) * lnw_ref[...] + lnb_ref[...] + bv_ref[...]) * g_ref[...]
    pa = jnp.dot(att_ref[...], wa_ref[...], preferred_element_type=F32)
    pr = jnp.dot(rw.astype(BF16), wr_ref[...], preferred_element_type=F32)
    merged = jax.nn.sigmoid(ga_ref[...]) * pa + jax.nn.sigmoid(gr_ref[...]) * pr
    mix = jnp.dot(merged.astype(BF16), wo_ref[...], preferred_element_type=F32)
    o_ref[...] = x_ref[...] + gt_ref[0] * _rms(mix, gain_ref[...])


def _mix(att, yf, yb, bv, g, lnw, lnb, ones, proj, x, gt, gain, wa, wr, wo, seq, tm=256):
    m, d = x.shape
    cw = RW_WIDTH
    bpr = seq // tm
    gblk = GATE_OFF // d
    const = lambda i: (0, 0)
    resident = lambda t: pl.BlockSpec(t.shape, const, pipeline_mode=pl.Buffered(1))
    return pl.pallas_call(
        _mix_kernel,
        name="mix",
        grid=(m // tm,),
        in_specs=[pl.BlockSpec((tm, att.shape[1]), lambda i: (i, 0)),
                  pl.BlockSpec((tm, cw), lambda i: (i, 0)),
                  pl.BlockSpec((tm, cw), lambda i: (i, 0)),
                  pl.BlockSpec((tm, cw), lambda i: (i, 0)),
                  pl.BlockSpec((tm, cw), lambda i: (i, 0)),
                  pl.BlockSpec((1, cw), const),
                  pl.BlockSpec((1, cw), const),
                  resident(ones),
                  pl.BlockSpec((tm, d), lambda i: (i, gblk)),
                  pl.BlockSpec((tm, d), lambda i: (i, gblk + 1)),
                  pl.BlockSpec((tm, d), lambda i: (i, 0)),
                  pl.BlockSpec((1, 1, d), lambda i: (i // bpr, 0, 0)),
                  pl.BlockSpec((1, d), const),
                  resident(wa), resident(wr), resident(wo)],
        out_specs=pl.BlockSpec((tm, d), lambda i: (i, 0)),
        out_shape=jax.ShapeDtypeStruct((m, d), F32),
        compiler_params=_cparams(("parallel",)),
    )(att, yf, yb, bv, g, lnw, lnb, ones, proj, proj, x, gt, gain, wa, wr, wo)


def _ffn_kernel(x_ref, xp_ref, xn_ref, gpre_ref, sc_ref, sh_ref, wg_ref, wu_ref, cw_ref, cb_ref, wd_ref,
                gt_ref, gpost_ref, o_ref, h_ref, *, tm, seq):
    i = pl.program_id(0)
    j = pl.program_id(1)
    halo = SUBLANES
    bpr = seq // tm

    @pl.when(j == 0)
    def _():
        gain = gpre_ref[...]
        sc = 1.0 + sc_ref[0]
        sh = sh_ref[0]
        has_prev = (i % bpr != 0).astype(F32)
        has_next = (i % bpr != bpr - 1).astype(F32)
        h_ref[0:halo, :] = ((_rms(xp_ref[...], gain) * sc + sh) * has_prev).astype(BF16)
        h_ref[halo:halo + tm, :] = (_rms(x_ref[...], gain) * sc + sh).astype(BF16)
        h_ref[halo + tm:, :] = ((_rms(xn_ref[...], gain) * sc + sh) * has_next).astype(BF16)
        o_ref[...] = jnp.zeros_like(o_ref)

    ext = tm + 2 * halo
    gate = jnp.dot(h_ref[...], wg_ref[...], preferred_element_type=F32)
    prev = pltpu.roll(gate, 1, 0)[halo:halo + tm]
    nxt = pltpu.roll(gate, ext - 1, 0)[halo:halo + tm]
    u = cw_ref[0:1, :] * prev + cw_ref[1:2, :] * gate[halo:halo + tm] + cw_ref[2:3, :] * nxt + cb_ref[...]
    up = jnp.dot(h_ref[halo:halo + tm, :], wu_ref[...], preferred_element_type=F32)
    act = jax.nn.gelu(u, approximate=True) * up
    o_ref[...] += jnp.dot(act.astype(BF16), wd_ref[...], preferred_element_type=F32)

    @pl.when(j == pl.num_programs(1) - 1)
    def _():
        o_ref[...] = x_ref[...] + gt_ref[0] * _rms(o_ref[...], gpost_ref[...])


def _ffn(x, gpre, sc, sh, wg, wu, cw, cb, wd, gt, gpost, seq, tm=512, tf=FF_TILE):
    m, d = x.shape
    f = wg.shape[1]
    bpr = seq // tm
    hb = tm // SUBLANES
    nhb = m // SUBLANES
    const = lambda i, j: (0, 0)
    bidx = lambda i, j: (i // bpr, 0, 0)
    return pl.pallas_call(
        functools.partial(_ffn_kernel, tm=tm, seq=seq),
        name="ffn",
        grid=(m // tm, f // tf),
        in_specs=[pl.BlockSpec((tm, d), lambda i, j: (i, 0)),
                  pl.BlockSpec((SUBLANES, d), lambda i, j: (jnp.maximum(i * hb - 1, 0), 0)),
                  pl.BlockSpec((SUBLANES, d), lambda i, j: (jnp.minimum((i + 1) * hb, nhb - 1), 0)),
                  pl.BlockSpec((1, d), const),
                  pl.BlockSpec((1, 1, d), bidx),
                  pl.BlockSpec((1, 1, d), bidx),
                  pl.BlockSpec((d, tf), lambda i, j: (0, j)),
                  pl.BlockSpec((d, tf), lambda i, j: (0, j)),
                  pl.BlockSpec((SUBLANES, tf), lambda i, j: (0, j)),
                  pl.BlockSpec((1, tf), lambda i, j: (0, j)),
                  pl.BlockSpec((tf, d), lambda i, j: (j, 0)),
                  pl.BlockSpec((1, 1, d), bidx),
                  pl.BlockSpec((1, d), const)],
        out_specs=pl.BlockSpec((tm, d), lambda i, j: (i, 0)),
        out_shape=jax.ShapeDtypeStruct((m, d), F32),
        scratch_shapes=[pltpu.VMEM((tm + 2 * SUBLANES, d), BF16)],
        compiler_params=_cparams(("parallel", "arbitrary")),
    )(x, x, x, gpre, sc, sh, wg, wu, cw, cb, wd, gt, gpost)


def _rope_tables(seq):
    half = ATT_HEAD_DIM // 2
    inv_freq = 1.0 / (ROPE_THETA ** (jnp.arange(half, dtype=F32) / half))
    ang = jnp.arange(seq, dtype=F32)[:, None] * inv_freq[None, :]
    cos = jnp.cos(ang)
    sin = jnp.sin(ang)
    return jnp.concatenate([cos, cos], axis=-1), jnp.concatenate([-sin, sin], axis=-1)


def _trunk(x, ada, p):
    bsz, seq, d = x.shape
    m = bsz * seq
    sh1, sc1, gt1, sh2, sc2, gt2 = [t[:, None, :] for t in jnp.split(ada, 6, axis=-1)]
    x2 = x.reshape(m, d)
    proj = _inproj(x2, p['ln_mix_pre'], sc1, sh1, p['w_in'], seq)
    proj3 = proj.reshape(bsz, seq, N_IN_PAD)
    cos, sin = _rope_tables(seq)
    att = _attention(proj3, cos, sin)
    r, v, kk, g, bv, lw, kd, b = _prep(proj, p['shift_mu'], p['decay_base'], p['decay_up'], p['iclr_base'],
                                       p['iclr_up'], p['gate_up'], p['k_k'], p['k_a'], p['r_k'], p['head_ones'], seq)
    per_seq = lambda t: t.reshape(t.shape[:-2] + (bsz, seq, RW_WIDTH))
    yf, yb = _rwkv_scan(per_seq(r), per_seq(v), per_seq(kk), per_seq(lw), per_seq(kd), per_seq(b))
    x1 = _mix(att.reshape(m, ATT_OUT), yf.reshape(m, RW_WIDTH), yb.reshape(m, RW_WIDTH), bv, g,
              p['lnx_w'], p['lnx_b'], p['head_ones'],
              proj, x2, gt1, p['ln_mix_post'], p['w_att_branch'], p['w_rwkv_branch'], p['w_out'], seq)
    y = _ffn(x1, p['ln_ffn_pre'], sc2, sh2, p['w_ffn_gate'], p['w_ffn_up'], p['ffn_conv_w'], p['ffn_conv_b'],
             p['w_ffn_down'], gt2, p['ln_ffn_post'], seq)
    return y.reshape(bsz, seq, d)


def kernel(x_prompt, x_sample, c_prompt, c_sample, ln_mix_pre, ln_mix_post, ln_ffn_pre, ln_ffn_post,
           w_ada, b_ada, w_in, shift_mu, decay_base, decay_up, iclr_base, iclr_up, gate_up,
           k_k, k_a, r_k, lnx_w, lnx_b, w_att_branch, w_rwkv_branch, w_out,
           w_ffn_gate, w_ffn_up, ffn_conv_w, ffn_conv_b, w_ffn_down):
    depth = w_in.shape[0]
    nb_p = c_prompt.shape[0]
    nb_s = c_sample.shape[0]
    c_all = jnp.concatenate([c_prompt, c_sample], axis=0)
    c_all = jnp.pad(c_all, ((0, (-c_all.shape[0]) % SUBLANES), (0, 0)))
    fpad = D_FF_PAD - D_FF
    xp, xs = x_prompt, x_sample
    lane_head = jnp.arange(MXU_DIM_V7X) // RW_HEAD
    head_ones = (lane_head[:, None] == lane_head[None, :]).astype(BF16)
    zdl = jnp.zeros((DECAY_LORA, RW_WIDTH), F32)
    zil = jnp.zeros((ICLR_LORA, RW_WIDTH), F32)
    for l in range(depth):
        att_end = 3 * ATT_WIDTH
        rw_end = att_end + RW_COLS
        w_in_b = w_in[l].astype(BF16)
        w_in_l = jnp.concatenate([w_in_b[:, rw_end:], w_in_b[:, att_end:rw_end],
                                  jnp.zeros((D_MODEL, LAT_PAD - LAT_COLS), BF16), w_in_b[:, :att_end]], axis=1)
        dup = jnp.block([[decay_up[l][0], zdl], [zdl, decay_up[l][1]]])
        dup_hi = dup.astype(BF16)
        p = dict(
            ln_mix_pre=ln_mix_pre[l][None], ln_mix_post=ln_mix_post[l][None],
            ln_ffn_pre=ln_ffn_pre[l][None], ln_ffn_post=ln_ffn_post[l][None],
            w_in=w_in_l, head_ones=head_ones,
            shift_mu=jnp.pad(shift_mu[l], ((0, 0), (0, RW_COLS_PAD - RW_COLS))),
            decay_base=decay_base[l].reshape(1, 2 * RW_WIDTH),
            decay_up=jnp.stack([dup_hi, (dup - dup_hi.astype(F32)).astype(BF16)]),
            iclr_base=iclr_base[l].reshape(1, 2 * RW_WIDTH),
            iclr_up=jnp.block([[iclr_up[l][0], zil], [zil, iclr_up[l][1]]]).astype(BF16),
            gate_up=jnp.pad(gate_up[l], ((0, LAT_PAD - LAT_COLS), (0, 0))).astype(BF16),
            k_k=k_k[l][None], k_a=k_a[l][None], r_k=r_k[l].reshape(1, RW_WIDTH),
            lnx_w=lnx_w[l][None], lnx_b=lnx_b[l][None],
            w_att_branch=w_att_branch[l].astype(BF16), w_rwkv_branch=w_rwkv_branch[l].astype(BF16),
            w_out=w_out[l].astype(BF16),
            w_ffn_gate=jnp.pad(w_ffn_gate[l], ((0, 0), (0, fpad))).astype(BF16),
            w_ffn_up=jnp.pad(w_ffn_up[l], ((0, 0), (0, fpad))).astype(BF16),
            ffn_conv_w=jnp.pad(ffn_conv_w[l], ((0, SUBLANES - 3), (0, fpad))),
            ffn_conv_b=jnp.pad(ffn_conv_b[l], ((0, fpad),))[None],
            w_ffn_down=jnp.pad(w_ffn_down[l], ((0, fpad), (0, 0))).astype(BF16),
        )
        ada = _ada(c_all, w_ada[l], b_ada[l][None])
        xp = _trunk(xp, ada[:nb_p], p)
        xs = _trunk(xs, ada[nb_p:nb_p + nb_s], p)
    return (xp, xs)
```

```python
import functools
import math

import jax
import jax.numpy as jnp
from jax import lax
from jax.experimental import pallas as pl
from jax.experimental.pallas import tpu as pltpu

F32 = jnp.float32
BF16 = jnp.bfloat16

D_MODEL = 2048
ATT_GROUPS = ((128, 1), (512, 4), (2048, 16))
ATT_HEADS_PER_GROUP = 4
ATT_HEAD_DIM = 128
ATT_HEADS = ATT_HEADS_PER_GROUP * len(ATT_GROUPS)
ATT_WIDTH = ATT_HEADS * ATT_HEAD_DIM
ATT_OUT = ATT_HEADS_PER_GROUP * ATT_HEAD_DIM
ROPE_THETA = 10000.0
RW_HEAD = 64
RW_WIDTH = D_MODEL // 2
RW_HEADS = RW_WIDTH // RW_HEAD
DECAY_LORA = 64
ICLR_LORA = 64
GATE_LORA = 160
RW_COLS = 3 * RW_WIDTH + 2 * DECAY_LORA + 2 * ICLR_LORA + GATE_LORA
N_IN = 3 * ATT_WIDTH + RW_COLS + 2 * D_MODEL
D_FF = ((8 * D_MODEL // 3 + 127) // 128) * 128
RMS_EPS = 1e-6
GN_EPS = 64e-5
NEG_INF = -1e30
DECAY_SCALE = math.exp(-0.5)

LANES = 128
SUBLANES = 8
BF16_ROWS = 16
MXU_DIM_V7X = 256
VMEM_BYTES_V7X = 64 * 1024 * 1024
VMEM_LIMIT = VMEM_BYTES_V7X - 4 * 1024 * 1024

GATE_OFF = 0
RW_OFF = GATE_OFF + 2 * D_MODEL
LAT_OFF = RW_OFF + 3 * RW_WIDTH
LAT_COLS = RW_COLS - 3 * RW_WIDTH
LAT_PAD = 512
ATT_OFF = LAT_OFF + LAT_PAD
N_IN_PAD = ATT_OFF + 3 * ATT_WIDTH
RW_COLS_PAD = 3 * RW_WIDTH + LAT_PAD
FF_TILE = 512
D_FF_PAD = -(-D_FF // FF_TILE) * FF_TILE
RW_CHUNK = 64
RW_ROWS_PER_STEP = 2
ATT_SPAN = 64
ATT_QBLOCK = 128


def _cparams(sem):
    return pltpu.CompilerParams(dimension_semantics=sem, vmem_limit_bytes=VMEM_LIMIT)


def _rms(x, gain):
    return x * lax.rsqrt(jnp.mean(x * x, axis=-1, keepdims=True) + RMS_EPS) * gain


def _ada_kernel(c_ref, w_ref, b_ref, o_ref):
    c = c_ref[...]
    s = c * jax.nn.sigmoid(c)
    o_ref[...] = jnp.dot(s.astype(BF16), w_ref[...].astype(BF16),
                         preferred_element_type=F32) + b_ref[...]


def _ada(c, w, b, tn=1024):
    m, k = c.shape
    n = w.shape[1]
    return pl.pallas_call(
        _ada_kernel,
        name="ada",
        grid=(n // tn,),
        in_specs=[pl.BlockSpec((m, k), lambda j: (0, 0)),
                  pl.BlockSpec((k, tn), lambda j: (0, j)),
                  pl.BlockSpec((1, tn), lambda j: (0, j))],
        out_specs=pl.BlockSpec((m, tn), lambda j: (0, j)),
        out_shape=jax.ShapeDtypeStruct((m, n), F32),
        compiler_params=_cparams(("parallel",)),
    )(c, w, b)


def _inproj_kernel(x_ref, g_ref, sc_ref, sh_ref, w_ref, o_ref, h_ref, *, part, stride):
    i = pl.program_id(0)
    j = pl.program_id(1)

    def norm(rows):
        return (_rms(x_ref[rows, :], g_ref[...]) * (1.0 + sc_ref[0]) + sh_ref[0]).astype(BF16)

    @pl.when((i == 0) & (j == 0))
    def _():
        h_ref[0] = norm(slice(None))

    o_ref[...] = jnp.dot(h_ref[i % 2], w_ref[...], preferred_element_type=F32)

    rows = pl.ds(pl.multiple_of(jnp.maximum(j - 1, 0) * stride, BF16_ROWS), part)
    h_ref[(i + 1) % 2, rows, :] = norm(rows)


def _inproj(x, gain, sc, sh, w, seq, tm=1024, tn=1536):
    m, k = x.shape
    n = w.shape[1]
    bpr = seq // tm
    ni = m // tm
    nsl = n // tn - 1
    part = -(-tm // (nsl * BF16_ROWS)) * BF16_ROWS
    stride = (tm - part) // (nsl - 1)
    assert stride * (nsl - 1) + part == tm and stride % BF16_ROWS == 0 and stride <= part
    ahead = lambda i, j: jnp.minimum(i + jnp.minimum(j, 1), ni - 1)
    return pl.pallas_call(
        functools.partial(_inproj_kernel, part=part, stride=stride),
        name="inproj",
        grid=(ni, n // tn),
        in_specs=[pl.BlockSpec((tm, k), lambda i, j: (ahead(i, j), 0)),
                  pl.BlockSpec((1, k), lambda i, j: (0, 0)),
                  pl.BlockSpec((1, 1, k), lambda i, j: (ahead(i, j) // bpr, 0, 0)),
                  pl.BlockSpec((1, 1, k), lambda i, j: (ahead(i, j) // bpr, 0, 0)),
                  pl.BlockSpec((k, tn), lambda i, j: (0, j))],
        out_specs=pl.BlockSpec((tm, tn), lambda i, j: (i, j)),
        out_shape=jax.ShapeDtypeStruct((m, n), F32),
        scratch_shapes=[pltpu.VMEM((2, tm, k), BF16)],
        compiler_params=_cparams(("arbitrary", "arbitrary")),
    )(x, gain, sc, sh, w)


def _attn_kernel(q_ref, k_ref, v_ref, cos_ref, sin_ref, o_ref, qr_ref, kr_ref, og_ref, lse_ref, *, seq):
    gid = pl.program_id(2)
    rows = 512
    half = ATT_HEAD_DIM // 2

    def rot(i, _):
        sl = pl.ds(pl.multiple_of(i * rows, rows), rows)
        cos = cos_ref[sl, :]
        sin = sin_ref[sl, :]
        q = q_ref[0, sl, :]
        k = k_ref[0, sl, :]
        qr_ref[sl, :] = (q * cos + pltpu.roll(q, half, 1) * sin) * (ATT_HEAD_DIM ** -0.5)
        kr_ref[sl, :] = k * cos + pltpu.roll(k, half, 1) * sin
        return 0

    lax.fori_loop(0, seq // rows, rot, 0)

    def group(gi, dil):
        cls_len = seq // dil
        bq = min(ATT_QBLOCK, cls_len)
        if cls_len <= bq + 2 * ATT_SPAN:
            bq = cls_len
        bk = min(bq + 2 * ATT_SPAN, cls_len)
        nqb = cls_len // bq
        rel0 = (lax.broadcasted_iota(jnp.int32, (bq, bk), 0)
                - lax.broadcasted_iota(jnp.int32, (bq, bk), 1))
        ones = jnp.ones((bk, ATT_HEAD_DIM), BF16)

        def body(it, _):
            cls = it // nqb
            p0 = (it % nqb) * bq
            ks = jnp.clip(p0 - ATT_SPAN, 0, cls_len - bk)
            if dil == 1:
                rq = pl.ds(p0, bq)
                rk = pl.ds(ks, bk)
            else:
                rq = pl.ds(cls + dil * p0, bq, stride=dil)
                rk = pl.ds(cls + dil * ks, bk, stride=dil)
            qb = qr_ref[rq, :].astype(BF16)
            kb = kr_ref[rk, :].astype(BF16)
            vb = v_ref[0, rk, :].astype(BF16)
            s = lax.dot_general(qb, kb, (((1,), (1,)), ((), ())), preferred_element_type=F32)
            band = jnp.abs(rel0 + (p0 - ks)) <= ATT_SPAN
            s = jnp.where(band, s, NEG_INF)
            m = jnp.max(s, axis=-1, keepdims=True)
            p = jnp.exp(s - m)
            pv = jnp.dot(p.astype(BF16), jnp.concatenate([vb, ones], axis=1), preferred_element_type=F32)
            l = pv[:, ATT_HEAD_DIM:]
            og_ref[gi, rq, :] = pv[:, :ATT_HEAD_DIM] / l
            lse_ref[gi, rq, :] = m + jnp.log(l)
            return 0

        lax.fori_loop(0, dil * nqb, body, 0, unroll=min(8, dil * nqb))

    for gi, (_, dil) in enumerate(ATT_GROUPS):
        @pl.when(gid == gi)
        def _(gi=gi, dil=dil):
            group(gi, dil)

    @pl.when(gid == len(ATT_GROUPS) - 1)
    def _():
        def comb(i, _):
            sl = pl.ds(pl.multiple_of(i * rows, rows), rows)
            l0 = lse_ref[0, sl, :]
            l1 = lse_ref[1, sl, :]
            l2 = lse_ref[2, sl, :]
            mx = jnp.maximum(jnp.maximum(l0, l1), l2)
            w0 = jnp.exp(l0 - mx)
            w1 = jnp.exp(l1 - mx)
            w2 = jnp.exp(l2 - mx)
            num = w0 * og_ref[0, sl, :] + w1 * og_ref[1, sl, :] + w2 * og_ref[2, sl, :]
            o_ref[0, sl, :] = (num / (w0 + w1 + w2)).astype(o_ref.dtype)
            return 0

        lax.fori_loop(0, seq // rows, comb, 0)


def _attention(proj3, cos, sin):
    bsz, seq, _ = proj3.shape
    hpg = ATT_HEADS_PER_GROUP
    ng = len(ATT_GROUPS)
    blk = (1, seq, ATT_HEAD_DIM)
    q0 = ATT_OFF // ATT_HEAD_DIM
    return pl.pallas_call(
        functools.partial(_attn_kernel, seq=seq),
        name="attn",
        grid=(bsz, hpg, ng),
        in_specs=[pl.BlockSpec(blk, lambda b, s, g: (b, 0, q0 + g * hpg + s)),
                  pl.BlockSpec(blk, lambda b, s, g: (b, 0, q0 + ATT_HEADS + g * hpg + s)),
                  pl.BlockSpec(blk, lambda b, s, g: (b, 0, q0 + 2 * ATT_HEADS + g * hpg + s)),
                  pl.BlockSpec((seq, ATT_HEAD_DIM), lambda b, s, g: (0, 0)),
                  pl.BlockSpec((seq, ATT_HEAD_DIM), lambda b, s, g: (0, 0))],
        out_specs=pl.BlockSpec(blk, lambda b, s, g: (b, 0, s)),
        out_shape=jax.ShapeDtypeStruct((bsz, seq, ATT_OUT), BF16),
        scratch_shapes=[pltpu.VMEM((seq, ATT_HEAD_DIM), F32),
                        pltpu.VMEM((seq, ATT_HEAD_DIM), F32),
                        pltpu.VMEM((ng, seq, ATT_HEAD_DIM), F32),
                        pltpu.VMEM((ng, seq, ATT_HEAD_DIM), F32)],
        compiler_params=_cparams(("parallel", "parallel", "arbitrary")),
    )(proj3, proj3, proj3, cos, sin)


def _rwkv_kernel(rf_ref, vf_ref, kkf_ref, lwf_ref, kdf_ref, bf_ref, rb_ref, vb_ref, kkb_ref, lwb_ref, kdb_ref,
                 bb_ref, yf_ref, yb_ref, s_ref):
    cn = RW_CHUNK
    c2 = 2 * cn

    @pl.when(pl.program_id(1) == 0)
    def _():
        s_ref[...] = jnp.zeros_like(s_ref)

    ti = lax.broadcasted_iota(jnp.int32, (cn, cn), 0)
    si = lax.broadcasted_iota(jnp.int32, (cn, cn), 1)
    t2 = lax.broadcasted_iota(jnp.int32, (c2, c2), 0)
    s2 = lax.broadcasted_iota(jnp.int32, (c2, c2), 1)
    eye = (t2 == s2).astype(F32)
    same_blk = (t2 // cn) == (s2 // cn)
    top = t2 < cn
    lane = lax.broadcasted_iota(jnp.int32, (1, LANES), 1)
    first = lane < RW_HEAD
    same_head = ((lax.broadcasted_iota(jnp.int32, (LANES, LANES), 0) // RW_HEAD)
                 == (lax.broadcasted_iota(jnp.int32, (LANES, LANES), 1) // RW_HEAD))

    def stack(x):
        return jnp.concatenate([jnp.where(first, x, 0.0), jnp.where(first, 0.0, x)], axis=0)

    def mm(a, b):
        return jnp.dot(a.astype(BF16), b.astype(BF16), preferred_element_type=F32)

    def mm_nt(a, b):
        return lax.dot_general(a.astype(BF16), b.astype(BF16), (((1,), (1,)), ((), ())),
                               preferred_element_type=F32)

    lanes = [slice(hp * LANES, (hp + 1) * LANES) for hp in range(RW_WIDTH // LANES)]
    inst = []
    xq, vs, bk, decay_end, amat, strict, incl = [], [], [], [], [], [], []
    for z, (r_ref, v_ref, kk_ref, lw_ref, kd_ref, b_ref, y_ref) in enumerate(
            ((rf_ref, vf_ref, kkf_ref, lwf_ref, kdf_ref, bf_ref, yf_ref),
             (rb_ref, vb_ref, kkb_ref, lwb_ref, kdb_ref, bb_ref, yb_ref))):
        rev = z == 1
        tri = ((si >= ti) if rev else (si <= ti)).astype(BF16)
        order = (t2 % cn - s2 % cn) if rev else (s2 % cn - t2 % cn)
        strict_z = (order < 0) & same_blk
        incl_z = (order <= 0) & same_blk
        for row in range(r_ref.shape[0]):
            lw_all = lw_ref[0, row]
            lw_hi = lw_all.astype(BF16)
            lw_lo = (lw_all - lw_hi.astype(F32)).astype(BF16)
            g_all = (jnp.dot(tri, lw_hi, preferred_element_type=F32)
                     + jnp.dot(tri, lw_lo, preferred_element_type=F32))
            for hp, ls in enumerate(lanes):
                r = r_ref[row, :, ls]
                v = v_ref[row, :, ls]
                kk = kk_ref[row, :, ls]
                kd = kd_ref[0, row, :, ls]
                b = b_ref[0, row, :, ls]
                g = g_all[:, ls]
                g_end = g[0:1] if rev else g[cn - 1:cn]
                e_neg = jnp.exp(-g)
                e_end = jnp.exp(g_end - g)
                x = jnp.concatenate([stack(kk * jnp.exp(g - lw_all[:, ls])), stack(r * jnp.exp(g))],
                                    axis=0).astype(BF16)
                y = jnp.concatenate([b * e_neg, kd * e_neg], axis=0).astype(BF16)
                inst.append((z, row, hp, ls, v_ref, y_ref))
                strict.append(strict_z)
                incl.append(incl_z)
                xq.append(x)
                vs.append(stack(v).astype(BF16))
                bk.append(jnp.concatenate([b * e_end, kd * e_end], axis=0).astype(BF16))
                decay_end.append(jnp.exp(g_end))
                amat.append(lax.dot_general(x, y, (((1,), (1,)), ((), ())), preferred_element_type=F32))
    swapped = [pltpu.roll(am, cn, 1) for am in amat]
    tmat = [jnp.where(st, jnp.where(top, am[:c2], sw[:c2]), 0.0) for am, sw, st in zip(amat, swapped, strict)]
    rest = [jnp.concatenate([jnp.where(st, jnp.where(top, sw[:c2], am[:c2]), 0.0),
                             jnp.where(ic, jnp.where(top, sw[c2:], am[c2:]), 0.0)], axis=0).astype(BF16)
            for am, sw, st, ic in zip(amat, swapped, strict, incl)]
    arb = [jnp.where(ic, jnp.where(top, am[c2:], sw[c2:]), 0.0).astype(BF16)
           for am, sw, ic in zip(amat, swapped, incl)]
    ninv = [eye - t for t in tmat]
    pw = [t.astype(BF16) for t in tmat]
    pw = [jnp.dot(p, p, preferred_element_type=F32).astype(BF16) for p in pw]
    nsq = cn.bit_length() - 2
    for it in range(nsq):
        if it < nsq - 1:
            both = [jnp.dot(p, jnp.concatenate([p, n.astype(BF16)], axis=1), preferred_element_type=F32)
                    for p, n in zip(pw, ninv)]
            pw = [bo[:, :c2].astype(BF16) for bo in both]
            ninv = [n + bo[:, c2:] for n, bo in zip(ninv, both)]
        else:
            ninv = [n + jnp.dot(p, n.astype(BF16), preferred_element_type=F32) for p, n in zip(pw, ninv)]
    mv = [jnp.dot(m, v, preferred_element_type=F32) for m, v in zip(rest, vs)]
    s0 = [s_ref[z, row, hp] for z, row, hp, _, _, _ in inst]
    xs = [mm_nt(x, s) for x, s in zip(xq, s0)]
    us = [mm(n, -x[:c2] - m[:c2]) for n, x, m in zip(ninv, xs, mv)]
    ys = [x[c2:] + m[c2:] + mm(ab, u) for x, m, ab, u in zip(xs, mv, arb, us)]
    for n, (z, row, hp, ls, v_ref, y_ref) in enumerate(inst):
        y_ref[row, :, ls] = ys[n][:cn] + ys[n][cn:]
        uv = jnp.concatenate([us[n][:cn] + us[n][cn:], v_ref[row, :, ls]], axis=0)
        upd = mm(uv.T, bk[n])
        s_ref[z, row, hp] = jnp.where(same_head, s0[n] * decay_end[n] + upd, 0.0)


def _rwkv_scan(r, v, kk, lw, kd, b):
    bsz, seq, width = r.shape
    cn = RW_CHUNK
    nc = seq // cn
    rows = RW_ROWS_PER_STEP
    assert bsz % rows == 0
    fwd = pl.BlockSpec((rows, cn, width), lambda b, c: (b, c, 0))
    bwd = pl.BlockSpec((rows, cn, width), lambda b, c: (b, nc - 1 - c, 0))
    fwd_d = pl.BlockSpec((1, rows, cn, width), lambda b, c: (0, b, c, 0))
    bwd_d = pl.BlockSpec((1, rows, cn, width), lambda b, c: (1, b, nc - 1 - c, 0))
    out = jax.ShapeDtypeStruct((bsz, seq, width), F32)
    return pl.pallas_call(
        _rwkv_kernel,
        name="rwkv",
        grid=(bsz // rows, nc),
        in_specs=[fwd, fwd, fwd, fwd_d, fwd_d, fwd_d, bwd, bwd, bwd, bwd_d, bwd_d, bwd_d],
        out_specs=[fwd, bwd],
        out_shape=[out, out],
        scratch_shapes=[pltpu.VMEM((2, rows, width // LANES, LANES, LANES), F32)],
        compiler_params=_cparams(("parallel", "arbitrary")),
    )(r, v, kk, lw, kd, b, r, v, kk, lw, kd, b)


def _head_sum(x, ones, passes):
    gw = ones.shape[0]
    cols = []
    for c0 in range(0, x.shape[1], gw):
        xc = x[:, c0:c0 + gw]
        hi = xc.astype(BF16)
        out = jnp.dot(hi, ones, preferred_element_type=F32)
        if passes == 2:
            lo = (xc - hi.astype(F32)).astype(BF16)
            out = out + jnp.dot(lo, ones, preferred_element_type=F32)
        cols.append(out)
    return jnp.concatenate(cols, axis=1)


def _prep_kernel(r_ref, k_ref, v_ref, l_ref, rp_ref, kp_ref, vp_ref, lp_ref, rn_ref, kn_ref, vn_ref, ln_ref,
                 mu_ref, dbase_ref, dup_ref, ibase_ref, iup_ref, gup_ref, kkw_ref, ka_ref, rk_ref, ones_ref,
                 ro_ref, vo_ref, kko_ref, go_ref, bvo_ref, lwo_ref, kdo_ref, bo_ref, *, tm, seq):
    i = pl.program_id(0)
    bpr = seq // tm
    has_prev = (i % bpr != 0).astype(F32)
    has_next = (i % bpr != bpr - 1).astype(F32)
    rowi = lax.broadcasted_iota(jnp.int32, (tm, 1), 0)
    cw = RW_WIDTH

    def shifted(main_ref, prev_ref, next_ref, c0, c1):
        z = main_ref[...]
        zp = jnp.where(rowi == 0, prev_ref[SUBLANES - 1:SUBLANES, :] * has_prev, pltpu.roll(z, 1, 0))
        zn = jnp.where(rowi == tm - 1, next_ref[0:1, :] * has_next, pltpu.roll(z, tm - 1, 0))
        return z + mu_ref[0:1, c0:c1] * (zp - z) + mu_ref[1:2, c0:c1] * (zn - z)

    r = shifted(r_ref, rp_ref, rn_ref, 0, cw)
    k = shifted(k_ref, kp_ref, kn_ref, cw, 2 * cw)
    v = shifted(v_ref, vp_ref, vn_ref, 2 * cw, 3 * cw)
    lat = shifted(l_ref, lp_ref, ln_ref, 3 * cw, 3 * cw + LAT_PAD)
    nd, ni = 2 * DECAY_LORA, 2 * ICLR_LORA
    wl = jnp.tanh(lat[:, :nd])
    wl_hi = wl.astype(BF16)
    wl_lo = (wl - wl_hi.astype(F32)).astype(BF16)
    w_raw = (dbase_ref[...] + jnp.dot(wl_hi, dup_ref[0], preferred_element_type=F32)
             + jnp.dot(wl_lo, dup_ref[0], preferred_element_type=F32)
             + jnp.dot(wl_hi, dup_ref[1], preferred_element_type=F32))
    lw = jax.nn.sigmoid(w_raw) * (-DECAY_SCALE)
    a = jax.nn.sigmoid(ibase_ref[...] + jnp.dot(lat[:, nd:nd + ni].astype(BF16), iup_ref[...],
                                                preferred_element_type=F32))
    g = jnp.dot(jax.nn.sigmoid(lat[:, nd + ni:]).astype(BF16), gup_ref[...], preferred_element_type=F32)
    ones = ones_ref[...]
    kkr = k * kkw_ref[...]
    kk = kkr / jnp.maximum(jnp.sqrt(_head_sum(kkr * kkr, ones, 2)), 1e-12)
    ka = ka_ref[...]
    k_fix = k * (1.0 - ka)
    k_var = k * ka
    kd0 = k_fix + k_var * a[:, :cw]
    kd1 = k_fix + k_var * a[:, cw:]
    bonus = _head_sum(r * (kd0 + kd1) * rk_ref[...], ones, 2)
    ro_ref[...] = r
    vo_ref[...] = v
    kko_ref[...] = kk
    go_ref[...] = g
    bvo_ref[...] = bonus * v
    lwo_ref[0] = lw[:, :cw]
    lwo_ref[1] = lw[:, cw:]
    kdo_ref[0] = kd0
    kdo_ref[1] = kd1
    bo_ref[0] = kk * a[:, :cw]
    bo_ref[1] = kk * a[:, cw:]


def _prep(proj, mu, dbase, dup, ibase, iup, gup, kkw, ka, rk, ones, seq, tm=256):
    m = proj.shape[0]
    cw = RW_WIDTH
    bpr = seq // tm
    hb = tm // SUBLANES
    nhb = m // SUBLANES
    r0 = RW_OFF // cw
    l0 = LAT_OFF // LAT_PAD
    cols = [(cw, r0), (cw, r0 + 1), (cw, r0 + 2), (LAT_PAD, l0)]
    main = [pl.BlockSpec((tm, w), lambda i, c=c: (i, c)) for w, c in cols]
    prev = [pl.BlockSpec((SUBLANES, w), lambda i, c=c: (jnp.maximum(i * hb - 1, 0), c)) for w, c in cols]
    nxt = [pl.BlockSpec((SUBLANES, w), lambda i, c=c: (jnp.minimum((i + 1) * hb, nhb - 1), c)) for w, c in cols]
    params = [mu, dbase, dup, ibase, iup, gup, kkw, ka, rk, ones]
    pspecs = [pl.BlockSpec(t.shape, lambda i, n=t.ndim: (0,) * n) for t in params]
    oblk = pl.BlockSpec((tm, cw), lambda i: (i, 0))
    dblk = pl.BlockSpec((2, tm, cw), lambda i: (0, i, 0))
    one = jax.ShapeDtypeStruct((m, cw), F32)
    two = jax.ShapeDtypeStruct((2, m, cw), F32)
    return pl.pallas_call(
        functools.partial(_prep_kernel, tm=tm, seq=seq),
        name="rwkv_prep",
        grid=(m // tm,),
        in_specs=main + prev + nxt + pspecs,
        out_specs=[oblk] * 5 + [dblk] * 3,
        out_shape=[one] * 5 + [two] * 3,
        compiler_params=_cparams(("parallel",)),
    )(*([proj] * 12), *params)


def _mix_kernel(att_ref, yf_ref, yb_ref, bv_ref, g_ref, lnw_ref, lnb_ref, ones_ref, ga_ref, gr_ref, x_ref, gt_ref,
                gain_ref, wa_ref, wr_ref, wo_ref, o_ref):
    ones = ones_ref[...]
    y = yf_ref[...] + yb_ref[...]
    mu = _head_sum(y, ones, 1) * (1.0 / RW_HEAD)
    yc = y - mu
    var = _head_sum(yc * yc, ones, 1) * (1.0 / RW_HEAD)
    rw = ((yc * lax.rsqrt(var + GN_EPS)) * lnw_ref[...] + lnb_ref[...] + bv_ref[...]) * g_ref[...]
    pa = jnp.dot(att_ref[...], wa_ref[...], preferred_element_type=F32)
    pr = jnp.dot(rw.astype(BF16), wr_ref[...], preferred_element_type=F32)
    merged = jax.nn.sigmoid(ga_ref[...]) * pa + jax.nn.sigmoid(gr_ref[...]) * pr
    mix = jnp.dot(merged.astype(BF16), wo_ref[...], preferred_element_type=F32)
    o_ref[...] = x_ref[...] + gt_ref[0] * _rms(mix, gain_ref[...])


def _mix(att, yf, yb, bv, g, lnw, lnb, ones, proj, x, gt, gain, wa, wr, wo, seq, tm=256):
    m, d = x.shape
    cw = RW_WIDTH
    bpr = seq // tm
    gblk = GATE_OFF // d
    const = lambda i: (0, 0)
    resident = lambda t: pl.BlockSpec(t.shape, const, pipeline_mode=pl.Buffered(1))
    return pl.pallas_call(
        _mix_kernel,
        name="mix",
        grid=(m // tm,),
        in_specs=[pl.BlockSpec((tm, att.shape[1]), lambda i: (i, 0)),
                  pl.BlockSpec((tm, cw), lambda i: (i, 0)),
                  pl.BlockSpec((tm, cw), lambda i: (i, 0)),
                  pl.BlockSpec((tm, cw), lambda i: (i, 0)),
                  pl.BlockSpec((tm, cw), lambda i: (i, 0)),
                  pl.BlockSpec((1, cw), const),
                  pl.BlockSpec((1, cw), const),
                  resident(ones),
                  pl.BlockSpec((tm, d), lambda i: (i, gblk)),
                  pl.BlockSpec((tm, d), lambda i: (i, gblk + 1)),
                  pl.BlockSpec((tm, d), lambda i: (i, 0)),
                  pl.BlockSpec((1, 1, d), lambda i: (i // bpr, 0, 0)),
                  pl.BlockSpec((1, d), const),
                  resident(wa), resident(wr), resident(wo)],
        out_specs=pl.BlockSpec((tm, d), lambda i: (i, 0)),
        out_shape=jax.ShapeDtypeStruct((m, d), F32),
        compiler_params=_cparams(("parallel",)),
    )(att, yf, yb, bv, g, lnw, lnb, ones, proj, proj, x, gt, gain, wa, wr, wo)


def _ffn_kernel(x_ref, xp_ref, xn_ref, gpre_ref, sc_ref, sh_ref, wg_ref, wu_ref, cw_ref, cb_ref, wd_ref,
                gt_ref, gpost_ref, o_ref, h_ref, *, tm, seq):
    i = pl.program_id(0)
    j = pl.program_id(1)
    halo = SUBLANES
    bpr = seq // tm

    @pl.when(j == 0)
    def _():
        gain = gpre_ref[...]
        sc = 1.0 + sc_ref[0]
        sh = sh_ref[0]
        has_prev = (i % bpr != 0).astype(F32)
        has_next = (i % bpr != bpr - 1).astype(F32)
        h_ref[0:halo, :] = ((_rms(xp_ref[...], gain) * sc + sh) * has_prev).astype(BF16)
        h_ref[halo:halo + tm, :] = (_rms(x_ref[...], gain) * sc + sh).astype(BF16)
        h_ref[halo + tm:, :] = ((_rms(xn_ref[...], gain) * sc + sh) * has_next).astype(BF16)
        o_ref[...] = jnp.zeros_like(o_ref)

    ext = tm + 2 * halo
    gate = jnp.dot(h_ref[...], wg_ref[...], preferred_element_type=F32)
    prev = pltpu.roll(gate, 1, 0)[halo:halo + tm]
    nxt = pltpu.roll(gate, ext - 1, 0)[halo:halo + tm]
    u = cw_ref[0:1, :] * prev + cw_ref[1:2, :] * gate[halo:halo + tm] + cw_ref[2:3, :] * nxt + cb_ref[...]
    up = jnp.dot(h_ref[halo:halo + tm, :], wu_ref[...], preferred_element_type=F32)
    act = jax.nn.gelu(u, approximate=True) * up
    o_ref[...] += jnp.dot(act.astype(BF16), wd_ref[...], preferred_element_type=F32)

    @pl.when(j == pl.num_programs(1) - 1)
    def _():
        o_ref[...] = x_ref[...] + gt_ref[0] * _rms(o_ref[...], gpost_ref[...])


def _ffn(x, gpre, sc, sh, wg, wu, cw, cb, wd, gt, gpost, seq, tm=512, tf=FF_TILE):
    m, d = x.shape
    f = wg.shape[1]
    bpr = seq // tm
    hb = tm // SUBLANES
    nhb = m // SUBLANES
    const = lambda i, j: (0, 0)
    bidx = lambda i, j: (i // bpr, 0, 0)
    return pl.pallas_call(
        functools.partial(_ffn_kernel, tm=tm, seq=seq),
        name="ffn",
        grid=(m // tm, f // tf),
        in_specs=[pl.BlockSpec((tm, d), lambda i, j: (i, 0)),
                  pl.BlockSpec((SUBLANES, d), lambda i, j: (jnp.maximum(i * hb - 1, 0), 0)),
                  pl.BlockSpec((SUBLANES, d), lambda i, j: (jnp.minimum((i + 1) * hb, nhb - 1), 0)),
                  pl.BlockSpec((1, d), const),
                  pl.BlockSpec((1, 1, d), bidx),
                  pl.BlockSpec((1, 1, d), bidx),
                  pl.BlockSpec((d, tf), lambda i, j: (0, j)),
                  pl.BlockSpec((d, tf), lambda i, j: (0, j)),
                  pl.BlockSpec((SUBLANES, tf), lambda i, j: (0, j)),
                  pl.BlockSpec((1, tf), lambda i, j: (0, j)),
                  pl.BlockSpec((tf, d), lambda i, j: (j, 0)),
                  pl.BlockSpec((1, 1, d), bidx),
                  pl.BlockSpec((1, d), const)],
        out_specs=pl.BlockSpec((tm, d), lambda i, j: (i, 0)),
        out_shape=jax.ShapeDtypeStruct((m, d), F32),
        scratch_shapes=[pltpu.VMEM((tm + 2 * SUBLANES, d), BF16)],
        compiler_params=_cparams(("parallel", "arbitrary")),
    )(x, x, x, gpre, sc, sh, wg, wu, cw, cb, wd, gt, gpost)


def _rope_tables(seq):
    half = ATT_HEAD_DIM // 2
    inv_freq = 1.0 / (ROPE_THETA ** (jnp.arange(half, dtype=F32) / half))
    ang = jnp.arange(seq, dtype=F32)[:, None] * inv_freq[None, :]
    cos = jnp.cos(ang)
    sin = jnp.sin(ang)
    return jnp.concatenate([cos, cos], axis=-1), jnp.concatenate([-sin, sin], axis=-1)


def _trunk(x, ada, p):
    bsz, seq, d = x.shape
    m = bsz * seq
    sh1, sc1, gt1, sh2, sc2, gt2 = [t[:, None, :] for t in jnp.split(ada, 6, axis=-1)]
    x2 = x.reshape(m, d)
    proj = _inproj(x2, p['ln_mix_pre'], sc1, sh1, p['w_in'], seq)
    proj3 = proj.reshape(bsz, seq, N_IN_PAD)
    cos, sin = _rope_tables(seq)
    att = _attention(proj3, cos, sin)
    r, v, kk, g, bv, lw, kd, b = _prep(proj, p['shift_mu'], p['decay_base'], p['decay_up'], p['iclr_base'],
                                       p['iclr_up'], p['gate_up'], p['k_k'], p['k_a'], p['r_k'], p['head_ones'], seq)
    per_seq = lambda t: t.reshape(t.shape[:-2] + (bsz, seq, RW_WIDTH))
    yf, yb = _rwkv_scan(per_seq(r), per_seq(v), per_seq(kk), per_seq(lw), per_seq(kd), per_seq(b))
    x1 = _mix(att.reshape(m, ATT_OUT), yf.reshape(m, RW_WIDTH), yb.reshape(m, RW_WIDTH), bv, g,
              p['lnx_w'], p['lnx_b'], p['head_ones'],
              proj, x2, gt1, p['ln_mix_post'], p['w_att_branch'], p['w_rwkv_branch'], p['w_out'], seq)
    y = _ffn(x1, p['ln_ffn_pre'], sc2, sh2, p['w_ffn_gate'], p['w_ffn_up'], p['ffn_conv_w'], p['ffn_conv_b'],
             p['w_ffn_down'], gt2, p['ln_ffn_post'], seq)
    return y.reshape(bsz, seq, d)


def kernel(x_prompt, x_sample, c_prompt, c_sample, ln_mix_pre, ln_mix_post, ln_ffn_pre, ln_ffn_post,
           w_ada, b_ada, w_in, shift_mu, decay_base, decay_up, iclr_base, iclr_up, gate_up,
           k_k, k_a, r_k, lnx_w, lnx_b, w_att_branch, w_rwkv_branch, w_out,
           w_ffn_gate, w_ffn_up, ffn_conv_w, ffn_conv_b, w_ffn_down):
    depth = w_in.shape[0]
    nb_p = c_prompt.shape[0]
    nb_s = c_sample.shape[0]
    c_all = jnp.concatenate([c_prompt, c_sample], axis=0)
    c_all = jnp.pad(c_all, ((0, (-c_all.shape[0]) % SUBLANES), (0, 0)))
    fpad = D_FF_PAD - D_FF
    xp, xs = x_prompt, x_sample
    lane_head = jnp.arange(MXU_DIM_V7X) // RW_HEAD
    head_ones = (lane_head[:, None] == lane_head[None, :]).astype(BF16)
    zdl = jnp.zeros((DECAY_LORA, RW_WIDTH), F32)
    zil = jnp.zeros((ICLR_LORA, RW_WIDTH), F32)
    for l in range(depth):
        att_end = 3 * ATT_WIDTH
        rw_end = att_end + RW_COLS
        w_in_b = w_in[l].astype(BF16)
        w_in_l = jnp.concatenate([w_in_b[:, rw_end:], w_in_b[:, att_end:rw_end],
                                  jnp.zeros((D_MODEL, LAT_PAD - LAT_COLS), BF16), w_in_b[:, :att_end]], axis=1)
        dup = jnp.block([[decay_up[l][0], zdl], [zdl, decay_up[l][1]]])
        dup_hi = dup.astype(BF16)
        p = dict(
            ln_mix_pre=ln_mix_pre[l][None], ln_mix_post=ln_mix_post[l][None],
            ln_ffn_pre=ln_ffn_pre[l][None], ln_ffn_post=ln_ffn_post[l][None],
            w_in=w_in_l, head_ones=head_ones,
            shift_mu=jnp.pad(shift_mu[l], ((0, 0), (0, RW_COLS_PAD - RW_COLS))),
            decay_base=decay_base[l].reshape(1, 2 * RW_WIDTH),
            decay_up=jnp.stack([dup_hi, (dup - dup_hi.astype(F32)).astype(BF16)]),
            iclr_base=iclr_base[l].reshape(1, 2 * RW_WIDTH),
            iclr_up=jnp.block([[iclr_up[l][0], zil], [zil, iclr_up[l][1]]]).astype(BF16),
            gate_up=jnp.pad(gate_up[l], ((0, LAT_PAD - LAT_COLS), (0, 0))).astype(BF16),
            k_k=k_k[l][None], k_a=k_a[l][None], r_k=r_k[l].reshape(1, RW_WIDTH),
            lnx_w=lnx_w[l][None], lnx_b=lnx_b[l][None],
            w_att_branch=w_att_branch[l].astype(BF16), w_rwkv_branch=w_rwkv_branch[l].astype(BF16),
            w_out=w_out[l].astype(BF16),
            w_ffn_gate=jnp.pad(w_ffn_gate[l], ((0, 0), (0, fpad))).astype(BF16),
            w_ffn_up=jnp.pad(w_ffn_up[l], ((0, 0), (0, fpad))).astype(BF16),
            ffn_conv_w=jnp.pad(ffn_conv_w[l], ((0, SUBLANES - 3), (0, fpad))),
            ffn_conv_b=jnp.pad(ffn_conv_b[l], ((0, fpad),))[None],
            w_ffn_down=jnp.pad(w_ffn_down[l], ((0, fpad), (0, 0))).astype(BF16),
        )
        ada = _ada(c_all, w_ada[l], b_ada[l][None])
        xp = _trunk(xp, ada[:nb_p], p)
        xs = _trunk(xs, ada[nb_p:nb_p + nb_s], p)
    return (xp, xs)
```

```python
import functools
import math

import jax
import jax.numpy as jnp
from jax import lax
from jax.experimental import pallas as pl
from jax.experimental.pallas import tpu as pltpu

F32 = jnp.float32
BF16 = jnp.bfloat16

D_MODEL = 2048
ATT_GROUPS = ((128, 1), (512, 4), (2048, 16))
ATT_HEADS_PER_GROUP = 4
ATT_HEAD_DIM = 128
ATT_HEADS = ATT_HEADS_PER_GROUP * len(ATT_GROUPS)
ATT_WIDTH = ATT_HEADS * ATT_HEAD_DIM
ATT_OUT = ATT_HEADS_PER_GROUP * ATT_HEAD_DIM
ROPE_THETA = 10000.0
RW_HEAD = 64
RW_WIDTH = D_MODEL // 2
RW_HEADS = RW_WIDTH // RW_HEAD
DECAY_LORA = 64
ICLR_LORA = 64
GATE_LORA = 160
RW_COLS = 3 * RW_WIDTH + 2 * DECAY_LORA + 2 * ICLR_LORA + GATE_LORA
N_IN = 3 * ATT_WIDTH + RW_COLS + 2 * D_MODEL
D_FF = ((8 * D_MODEL // 3 + 127) // 128) * 128
RMS_EPS = 1e-6
GN_EPS = 64e-5
NEG_INF = -1e30
DECAY_SCALE = math.exp(-0.5)

LANES = 128
SUBLANES = 8
BF16_ROWS = 16
MXU_DIM_V7X = 256
VMEM_BYTES_V7X = 64 * 1024 * 1024
VMEM_LIMIT = VMEM_BYTES_V7X - 4 * 1024 * 1024

GATE_OFF = 0
RW_OFF = GATE_OFF + 2 * D_MODEL
LAT_OFF = RW_OFF + 3 * RW_WIDTH
LAT_COLS = RW_COLS - 3 * RW_WIDTH
LAT_PAD = 512
ATT_OFF = LAT_OFF + LAT_PAD
N_IN_PAD = ATT_OFF + 3 * ATT_WIDTH
RW_COLS_PAD = 3 * RW_WIDTH + LAT_PAD
FF_TILE = 512
D_FF_PAD = -(-D_FF // FF_TILE) * FF_TILE
RW_CHUNK = 64
RW_ROWS_PER_STEP = 2
ATT_SPAN = 64
ATT_QBLOCK = 128


def _cparams(sem):
    return pltpu.CompilerParams(dimension_semantics=sem, vmem_limit_bytes=VMEM_LIMIT)


def _rms(x, gain):
    return x * lax.rsqrt(jnp.mean(x * x, axis=-1, keepdims=True) + RMS_EPS) * gain


def _row_chunks(first, last, fn, unroll=4):
    if unroll is None:
        for c in range(first, last):
            fn(c * BF16_ROWS)
        return

    def body(c, carry):
        fn(pl.multiple_of(c * BF16_ROWS, BF16_ROWS))
        return carry

    lax.fori_loop(first, last, body, 0, unroll=min(unroll, last - first))


def _ada_kernel(c_ref, w_ref, b_ref, o_ref):
    c = c_ref[...]
    s = c * jax.nn.sigmoid(c)
    o_ref[...] = jnp.dot(s.astype(BF16), w_ref[...].astype(BF16),
                         preferred_element_type=F32) + b_ref[...]


def _ada(c, w, b, tn=1024):
    m, k = c.shape
    n = w.shape[1]
    return pl.pallas_call(
        _ada_kernel,
        name="ada",
        grid=(n // tn,),
        in_specs=[pl.BlockSpec((m, k), lambda j: (0, 0)),
                  pl.BlockSpec((k, tn), lambda j: (0, j)),
                  pl.BlockSpec((1, tn), lambda j: (0, j))],
        out_specs=pl.BlockSpec((m, tn), lambda j: (0, j)),
        out_shape=jax.ShapeDtypeStruct((m, n), F32),
        compiler_params=_cparams(("parallel",)),
    )(c, w, b)


def _inproj_kernel(x_ref, g_ref, sc_ref, sh_ref, w_ref, o_ref, h_ref, *, part, stride):
    i = pl.program_id(0)
    j = pl.program_id(1)

    def norm_rows(slot, start, nrows, unroll):
        def chunk(r0):
            rows = pl.ds(pl.multiple_of(start + r0, BF16_ROWS), BF16_ROWS)
            h = _rms(x_ref[rows, :], g_ref[...]) * (1.0 + sc_ref[0]) + sh_ref[0]
            h_ref[slot, rows, :] = h.astype(BF16)

        _row_chunks(0, nrows // BF16_ROWS, chunk, unroll)

    @pl.when((i == 0) & (j == 0))
    def _():
        norm_rows(0, 0, x_ref.shape[0], 4)

    o_ref[...] = jnp.dot(h_ref[i % 2], w_ref[...], preferred_element_type=F32)

    norm_rows((i + 1) % 2, pl.multiple_of(jnp.maximum(j - 1, 0) * stride, BF16_ROWS), part, None)


def _inproj(x, gain, sc, sh, w, seq, tm=1024, tn=1536):
    m, k = x.shape
    n = w.shape[1]
    bpr = seq // tm
    ni = m // tm
    nsl = n // tn - 1
    part = -(-tm // (nsl * BF16_ROWS)) * BF16_ROWS
    stride = (tm - part) // (nsl - 1)
    assert stride * (nsl - 1) + part == tm and stride % BF16_ROWS == 0 and stride <= part
    ahead = lambda i, j: jnp.minimum(i + jnp.minimum(j, 1), ni - 1)
    return pl.pallas_call(
        functools.partial(_inproj_kernel, part=part, stride=stride),
        name="inproj",
        grid=(ni, n // tn),
        in_specs=[pl.BlockSpec((tm, k), lambda i, j: (ahead(i, j), 0)),
                  pl.BlockSpec((1, k), lambda i, j: (0, 0)),
                  pl.BlockSpec((1, 1, k), lambda i, j: (ahead(i, j) // bpr, 0, 0)),
                  pl.BlockSpec((1, 1, k), lambda i, j: (ahead(i, j) // bpr, 0, 0)),
                  pl.BlockSpec((k, tn), lambda i, j: (0, j))],
        out_specs=pl.BlockSpec((tm, tn), lambda i, j: (i, j)),
        out_shape=jax.ShapeDtypeStruct((m, n), F32),
        scratch_shapes=[pltpu.VMEM((2, tm, k), BF16)],
        compiler_params=_cparams(("arbitrary", "arbitrary")),
    )(x, gain, sc, sh, w)


def _attn_kernel(q_ref, k_ref, v_ref, cos_ref, sin_ref, o_ref, qr_ref, kr_ref, og_ref, lse_ref, *, seq):
    gid = pl.program_id(2)
    rows = 512
    half = ATT_HEAD_DIM // 2

    def rot(i, _):
        sl = pl.ds(pl.multiple_of(i * rows, rows), rows)
        cos = cos_ref[sl, :]
        sin = sin_ref[sl, :]
        q = q_ref[0, sl, :]
        k = k_ref[0, sl, :]
        qr_ref[sl, :] = (q * cos + pltpu.roll(q, half, 1) * sin) * (ATT_HEAD_DIM ** -0.5)
        kr_ref[sl, :] = k * cos + pltpu.roll(k, half, 1) * sin
        return 0

    lax.fori_loop(0, seq // rows, rot, 0)

    def group(gi, dil):
        cls_len = seq // dil
        bq = min(ATT_QBLOCK, cls_len)
        if cls_len <= bq + 2 * ATT_SPAN:
            bq = cls_len
        bk = min(bq + 2 * ATT_SPAN, cls_len)
        nqb = cls_len // bq
        rel0 = (lax.broadcasted_iota(jnp.int32, (bq, bk), 0)
                - lax.broadcasted_iota(jnp.int32, (bq, bk), 1))
        ones = jnp.ones((bk, ATT_HEAD_DIM), BF16)

        def body(it, _):
            cls = it // nqb
            p0 = (it % nqb) * bq
            ks = jnp.clip(p0 - ATT_SPAN, 0, cls_len - bk)
            if dil == 1:
                rq = pl.ds(p0, bq)
                rk = pl.ds(ks, bk)
            else:
                rq = pl.ds(cls + dil * p0, bq, stride=dil)
                rk = pl.ds(cls + dil * ks, bk, stride=dil)
            qb = qr_ref[rq, :].astype(BF16)
            kb = kr_ref[rk, :].astype(BF16)
            vb = v_ref[0, rk, :].astype(BF16)
            s = lax.dot_general(qb, kb, (((1,), (1,)), ((), ())), preferred_element_type=F32)
            band = jnp.abs(rel0 + (p0 - ks)) <= ATT_SPAN
            s = jnp.where(band, s, NEG_INF)
            m = jnp.max(s, axis=-1, keepdims=True)
            p = jnp.exp(s - m)
            pv = jnp.dot(p.astype(BF16), jnp.concatenate([vb, ones], axis=1), preferred_element_type=F32)
            l = pv[:, ATT_HEAD_DIM:]
            og_ref[gi, rq, :] = pv[:, :ATT_HEAD_DIM] / l
            lse_ref[gi, rq, :] = m + jnp.log(l)
            return 0

        lax.fori_loop(0, dil * nqb, body, 0, unroll=min(8, dil * nqb))

    for gi, (_, dil) in enumerate(ATT_GROUPS):
        @pl.when(gid == gi)
        def _(gi=gi, dil=dil):
            group(gi, dil)

    @pl.when(gid == len(ATT_GROUPS) - 1)
    def _():
        def comb(i, _):
            sl = pl.ds(pl.multiple_of(i * rows, rows), rows)
            l0 = lse_ref[0, sl, :]
            l1 = lse_ref[1, sl, :]
            l2 = lse_ref[2, sl, :]
            mx = jnp.maximum(jnp.maximum(l0, l1), l2)
            w0 = jnp.exp(l0 - mx)
            w1 = jnp.exp(l1 - mx)
            w2 = jnp.exp(l2 - mx)
            num = w0 * og_ref[0, sl, :] + w1 * og_ref[1, sl, :] + w2 * og_ref[2, sl, :]
            o_ref[0, sl, :] = (num / (w0 + w1 + w2)).astype(o_ref.dtype)
            return 0

        lax.fori_loop(0, seq // rows, comb, 0)


def _attention(proj3, cos, sin):
    bsz, seq, _ = proj3.shape
    hpg = ATT_HEADS_PER_GROUP
    ng = len(ATT_GROUPS)
    blk = (1, seq, ATT_HEAD_DIM)
    q0 = ATT_OFF // ATT_HEAD_DIM
    return pl.pallas_call(
        functools.partial(_attn_kernel, seq=seq),
        name="attn",
        grid=(bsz, hpg, ng),
        in_specs=[pl.BlockSpec(blk, lambda b, s, g: (b, 0, q0 + g * hpg + s)),
                  pl.BlockSpec(blk, lambda b, s, g: (b, 0, q0 + ATT_HEADS + g * hpg + s)),
                  pl.BlockSpec(blk, lambda b, s, g: (b, 0, q0 + 2 * ATT_HEADS + g * hpg + s)),
                  pl.BlockSpec((seq, ATT_HEAD_DIM), lambda b, s, g: (0, 0)),
                  pl.BlockSpec((seq, ATT_HEAD_DIM), lambda b, s, g: (0, 0))],
        out_specs=pl.BlockSpec(blk, lambda b, s, g: (b, 0, s)),
        out_shape=jax.ShapeDtypeStruct((bsz, seq, ATT_OUT), BF16),
        scratch_shapes=[pltpu.VMEM((seq, ATT_HEAD_DIM), F32),
                        pltpu.VMEM((seq, ATT_HEAD_DIM), F32),
                        pltpu.VMEM((ng, seq, ATT_HEAD_DIM), F32),
                        pltpu.VMEM((ng, seq, ATT_HEAD_DIM), F32)],
        compiler_params=_cparams(("parallel", "parallel", "arbitrary")),
    )(proj3, proj3, proj3, cos, sin)


def _rwkv_kernel(rf_ref, vf_ref, kkf_ref, lwf_ref, kdf_ref, bf_ref, rb_ref, vb_ref, kkb_ref, lwb_ref, kdb_ref,
                 bb_ref, yf_ref, yb_ref, s_ref):
    cn = RW_CHUNK
    c2 = 2 * cn

    @pl.when(pl.program_id(1) == 0)
    def _():
        s_ref[...] = jnp.zeros_like(s_ref)

    ti = lax.broadcasted_iota(jnp.int32, (cn, cn), 0)
    si = lax.broadcasted_iota(jnp.int32, (cn, cn), 1)
    t2 = lax.broadcasted_iota(jnp.int32, (c2, c2), 0)
    s2 = lax.broadcasted_iota(jnp.int32, (c2, c2), 1)
    eye = (t2 == s2).astype(F32)
    same_blk = (t2 // cn) == (s2 // cn)
    top = t2 < cn
    lane = lax.broadcasted_iota(jnp.int32, (1, LANES), 1)
    first = lane < RW_HEAD
    same_head = ((lax.broadcasted_iota(jnp.int32, (LANES, LANES), 0) // RW_HEAD)
                 == (lax.broadcasted_iota(jnp.int32, (LANES, LANES), 1) // RW_HEAD))

    def stack(x):
        return jnp.concatenate([jnp.where(first, x, 0.0), jnp.where(first, 0.0, x)], axis=0)

    def mm(a, b):
        return jnp.dot(a.astype(BF16), b.astype(BF16), preferred_element_type=F32)

    def mm_nt(a, b):
        return lax.dot_general(a.astype(BF16), b.astype(BF16), (((1,), (1,)), ((), ())),
                               preferred_element_type=F32)

    lanes = [slice(hp * LANES, (hp + 1) * LANES) for hp in range(RW_WIDTH // LANES)]
    inst = []
    xq, vs, bk, decay_end, amat, strict, incl = [], [], [], [], [], [], []
    for z, (r_ref, v_ref, kk_ref, lw_ref, kd_ref, b_ref, y_ref) in enumerate(
            ((rf_ref, vf_ref, kkf_ref, lwf_ref, kdf_ref, bf_ref, yf_ref),
             (rb_ref, vb_ref, kkb_ref, lwb_ref, kdb_ref, bb_ref, yb_ref))):
        rev = z == 1
        tri = ((si >= ti) if rev else (si <= ti)).astype(BF16)
        order = (t2 % cn - s2 % cn) if rev else (s2 % cn - t2 % cn)
        strict_z = (order < 0) & same_blk
        incl_z = (order <= 0) & same_blk
        for row in range(r_ref.shape[0]):
            lw_all = lw_ref[0, row]
            lw_hi = lw_all.astype(BF16)
            lw_lo = (lw_all - lw_hi.astype(F32)).astype(BF16)
            g_all = (jnp.dot(tri, lw_hi, preferred_element_type=F32)
                     + jnp.dot(tri, lw_lo, preferred_element_type=F32))
            for hp, ls in enumerate(lanes):
                r = r_ref[row, :, ls]
                v = v_ref[row, :, ls]
                kk = kk_ref[row, :, ls]
                kd = kd_ref[0, row, :, ls]
                b = b_ref[0, row, :, ls]
                g = g_all[:, ls]
                g_end = g[0:1] if rev else g[cn - 1:cn]
                e_neg = jnp.exp(-g)
                e_end = jnp.exp(g_end - g)
                x = jnp.concatenate([stack(kk * jnp.exp(g - lw_all[:, ls])), stack(r * jnp.exp(g))],
                                    axis=0).astype(BF16)
                y = jnp.concatenate([b * e_neg, kd * e_neg], axis=0).astype(BF16)
                inst.append((z, row, hp, ls, v_ref, y_ref))
                strict.append(strict_z)
                incl.append(incl_z)
                xq.append(x)
                vs.append(stack(v).astype(BF16))
                bk.append(jnp.concatenate([b * e_end, kd * e_end], axis=0).astype(BF16))
                decay_end.append(jnp.exp(g_end))
                amat.append(lax.dot_general(x, y, (((1,), (1,)), ((), ())), preferred_element_type=F32))
    swapped = [pltpu.roll(am, cn, 1) for am in amat]
    tmat = [jnp.where(st, jnp.where(top, am[:c2], sw[:c2]), 0.0) for am, sw, st in zip(amat, swapped, strict)]
    rest = [jnp.concatenate([jnp.where(st, jnp.where(top, sw[:c2], am[:c2]), 0.0),
                             jnp.where(ic, jnp.where(top, sw[c2:], am[c2:]), 0.0)], axis=0).astype(BF16)
            for am, sw, st, ic in zip(amat, swapped, strict, incl)]
    arb = [jnp.where(ic, jnp.where(top, am[c2:], sw[c2:]), 0.0).astype(BF16)
           for am, sw, ic in zip(amat, swapped, incl)]
    ninv = [eye - t for t in tmat]
    pw = [t.astype(BF16) for t in tmat]
    pw = [jnp.dot(p, p, preferred_element_type=F32).astype(BF16) for p in pw]
    nsq = cn.bit_length() - 2
    for it in range(nsq):
        if it < nsq - 1:
            both = [jnp.dot(p, jnp.concatenate([p, n.astype(BF16)], axis=1), preferred_element_type=F32)
                    for p, n in zip(pw, ninv)]
            pw = [bo[:, :c2].astype(BF16) for bo in both]
            ninv = [n + bo[:, c2:] for n, bo in zip(ninv, both)]
        else:
            ninv = [n + jnp.dot(p, n.astype(BF16), preferred_element_type=F32) for p, n in zip(pw, ninv)]
    mv = [jnp.dot(m, v, preferred_element_type=F32) for m, v in zip(rest, vs)]
    s0 = [s_ref[z, row, hp] for z, row, hp, _, _, _ in inst]
    xs = [mm_nt(x, s) for x, s in zip(xq, s0)]
    us = [mm(n, -x[:c2] - m[:c2]) for n, x, m in zip(ninv, xs, mv)]
    ys = [x[c2:] + m[c2:] + mm(ab, u) for x, m, ab, u in zip(xs, mv, arb, us)]
    for n, (z, row, hp, ls, v_ref, y_ref) in enumerate(inst):
        y_ref[row, :, ls] = ys[n][:cn] + ys[n][cn:]
        uv = jnp.concatenate([us[n][:cn] + us[n][cn:], v_ref[row, :, ls].astype(F32)], axis=0)
        upd = mm(uv.T, bk[n])
        s_ref[z, row, hp] = jnp.where(same_head, s0[n] * decay_end[n] + upd, 0.0)


def _rwkv_scan(r, v, kk, lw, kd, b):
    bsz, seq, width = r.shape
    cn = RW_CHUNK
    nc = seq // cn
    rows = RW_ROWS_PER_STEP
    assert bsz % rows == 0
    fwd = pl.BlockSpec((rows, cn, width), lambda b, c: (b, c, 0))
    bwd = pl.BlockSpec((rows, cn, width), lambda b, c: (b, nc - 1 - c, 0))
    fwd_d = pl.BlockSpec((1, rows, cn, width), lambda b, c: (0, b, c, 0))
    bwd_d = pl.BlockSpec((1, rows, cn, width), lambda b, c: (1, b, nc - 1 - c, 0))
    out = jax.ShapeDtypeStruct((bsz, seq, width), F32)
    return pl.pallas_call(
        _rwkv_kernel,
        name="rwkv",
        grid=(bsz // rows, nc),
        in_specs=[fwd, fwd, fwd, fwd_d, fwd_d, fwd_d, bwd, bwd, bwd, bwd_d, bwd_d, bwd_d],
        out_specs=[fwd, bwd],
        out_shape=[out, out],
        scratch_shapes=[pltpu.VMEM((2, rows, width // LANES, LANES, LANES), F32)],
        compiler_params=_cparams(("parallel", "arbitrary")),
    )(r, v, kk, lw, kd, b, r, v, kk, lw, kd, b)


def _head_sum(x, ones, passes):
    gw = ones.shape[0]
    cols = []
    for c0 in range(0, x.shape[1], gw):
        xc = x[:, c0:c0 + gw]
        hi = xc.astype(BF16)
        out = jnp.dot(hi, ones, preferred_element_type=F32)
        if passes == 2:
            lo = (xc - hi.astype(F32)).astype(BF16)
            out = out + jnp.dot(lo, ones, preferred_element_type=F32)
        cols.append(out)
    return jnp.concatenate(cols, axis=1)


def _prep_kernel(r_ref, k_ref, v_ref, l_ref, rp_ref, kp_ref, vp_ref, lp_ref, rn_ref, kn_ref, vn_ref, ln_ref,
                 mu_ref, dbase_ref, dup_ref, ibase_ref, iup_ref, gup_ref, kkw_ref, ka_ref, rk_ref, ones_ref,
                 ro_ref, vo_ref, kko_ref, go_ref, bvo_ref, lwo_ref, kdo_ref, bo_ref, *, tm, seq):
    i = pl.program_id(0)
    bpr = seq // tm
    has_prev = (i % bpr != 0).astype(F32)
    has_next = (i % bpr != bpr - 1).astype(F32)
    rowi = lax.broadcasted_iota(jnp.int32, (tm, 1), 0)
    cw = RW_WIDTH

    def shifted(main_ref, prev_ref, next_ref, c0, c1):
        z = main_ref[...]
        zp = jnp.where(rowi == 0, prev_ref[SUBLANES - 1:SUBLANES, :] * has_prev, pltpu.roll(z, 1, 0))
        zn = jnp.where(rowi == tm - 1, next_ref[0:1, :] * has_next, pltpu.roll(z, tm - 1, 0))
        return z + mu_ref[0:1, c0:c1] * (zp - z) + mu_ref[1:2, c0:c1] * (zn - z)

    r = shifted(r_ref, rp_ref, rn_ref, 0, cw)
    k = shifted(k_ref, kp_ref, kn_ref, cw, 2 * cw)
    v = shifted(v_ref, vp_ref, vn_ref, 2 * cw, 3 * cw)
    lat = shifted(l_ref, lp_ref, ln_ref, 3 * cw, 3 * cw + LAT_PAD)
    nd, ni = 2 * DECAY_LORA, 2 * ICLR_LORA
    wl = jnp.tanh(lat[:, :nd])
    wl_hi = wl.astype(BF16)
    wl_lo = (wl - wl_hi.astype(F32)).astype(BF16)
    w_raw = (dbase_ref[...] + jnp.dot(wl_hi, dup_ref[0], preferred_element_type=F32)
             + jnp.dot(wl_lo, dup_ref[0], preferred_element_type=F32)
             + jnp.dot(wl_hi, dup_ref[1], preferred_element_type=F32))
    lw = jax.nn.sigmoid(w_raw) * (-DECAY_SCALE)
    a = jax.nn.sigmoid(ibase_ref[...] + jnp.dot(lat[:, nd:nd + ni].astype(BF16), iup_ref[...],
                                                preferred_element_type=F32))
    g = jnp.dot(jax.nn.sigmoid(lat[:, nd + ni:]).astype(BF16), gup_ref[...], preferred_element_type=F32)
    ones = ones_ref[...]
    kkr = k * kkw_ref[...]
    kk = kkr / jnp.maximum(jnp.sqrt(_head_sum(kkr * kkr, ones, 2)), 1e-12)
    ka = ka_ref[...]
    k_fix = k * (1.0 - ka)
    k_var = k * ka
    kd0 = k_fix + k_var * a[:, :cw]
    kd1 = k_fix + k_var * a[:, cw:]
    bonus = _head_sum(r * (kd0 + kd1) * rk_ref[...], ones, 2)
    ro_ref[...] = r
    vo_ref[...] = v.astype(vo_ref.dtype)
    kko_ref[...] = kk
    go_ref[...] = g.astype(go_ref.dtype)
    bvo_ref[...] = (bonus * v).astype(bvo_ref.dtype)
    lwo_ref[0] = lw[:, :cw]
    lwo_ref[1] = lw[:, cw:]
    kdo_ref[0] = kd0
    kdo_ref[1] = kd1
    bo_ref[0] = kk * a[:, :cw]
    bo_ref[1] = kk * a[:, cw:]


def _prep(proj, mu, dbase, dup, ibase, iup, gup, kkw, ka, rk, ones, seq, tm=256):
    m = proj.shape[0]
    cw = RW_WIDTH
    bpr = seq // tm
    hb = tm // SUBLANES
    nhb = m // SUBLANES
    r0 = RW_OFF // cw
    l0 = LAT_OFF // LAT_PAD
    cols = [(cw, r0), (cw, r0 + 1), (cw, r0 + 2), (LAT_PAD, l0)]
    main = [pl.BlockSpec((tm, w), lambda i, c=c: (i, c)) for w, c in cols]
    prev = [pl.BlockSpec((SUBLANES, w), lambda i, c=c: (jnp.maximum(i * hb - 1, 0), c)) for w, c in cols]
    nxt = [pl.BlockSpec((SUBLANES, w), lambda i, c=c: (jnp.minimum((i + 1) * hb, nhb - 1), c)) for w, c in cols]
    params = [mu, dbase, dup, ibase, iup, gup, kkw, ka, rk, ones]
    pspecs = [pl.BlockSpec(t.shape, lambda i, n=t.ndim: (0,) * n) for t in params]
    oblk = pl.BlockSpec((tm, cw), lambda i: (i, 0))
    dblk = pl.BlockSpec((2, tm, cw), lambda i: (0, i, 0))
    one = jax.ShapeDtypeStruct((m, cw), F32)
    half = jax.ShapeDtypeStruct((m, cw), BF16)
    two = jax.ShapeDtypeStruct((2, m, cw), F32)
    return pl.pallas_call(
        functools.partial(_prep_kernel, tm=tm, seq=seq),
        name="rwkv_prep",
        grid=(m // tm,),
        in_specs=main + prev + nxt + pspecs,
        out_specs=[oblk] * 5 + [dblk] * 3,
        out_shape=[one, half, one, half, half] + [two] * 3,
        compiler_params=_cparams(("parallel",)),
    )(*([proj] * 12), *params)


def _mix_kernel(att_ref, yf_ref, yb_ref, bv_ref, g_ref, lnw_ref, lnb_ref, ones_ref, ga_ref, gr_ref, x_ref, gt_ref,
                gain_ref, wa_ref, wr_ref, wo_ref, o_ref):
    ones = ones_ref[...]
    y = yf_ref[...] + yb_ref[...]
    mu = _head_sum(y, ones, 1) * (1.0 / RW_HEAD)
    yc = y - mu
    var = _head_sum(yc * yc, ones, 1) * (1.0 / RW_HEAD)
    rw = ((yc * lax.rsqrt(var + GN_EPS)) * lnw_ref[...] + lnb_ref[...] + bv_ref[...]) * g_ref[...]
    pa = jnp.dot(att_ref[...], wa_ref[...], preferred_element_type=F32)
    pr = jnp.dot(rw.astype(BF16), wr_ref[...], preferred_element_type=F32)
    merged = jax.nn.sigmoid(ga_ref[...]) * pa + jax.nn.sigmoid(gr_ref[...]) * pr
    mix = jnp.dot(merged.astype(BF16), wo_ref[...], preferred_element_type=F32)
    o_ref[...] = x_ref[...] + gt_ref[0] * _rms(mix, gain_ref[...])


def _mix(att, yf, yb, bv, g, lnw, lnb, ones, proj, x, gt, gain, wa, wr, wo, seq, tm=256):
    m, d = x.shape
    cw = RW_WIDTH
    bpr = seq // tm
    gblk = GATE_OFF // d
    const = lambda i: (0, 0)
    resident = lambda t: pl.BlockSpec(t.shape, const, pipeline_mode=pl.Buffered(1))
    return pl.pallas_call(
        _mix_kernel,
        name="mix",
        grid=(m // tm,),
        in_specs=[pl.BlockSpec((tm, att.shape[1]), lambda i: (i, 0)),
                  pl.BlockSpec((tm, cw), lambda i: (i, 0)),
                  pl.BlockSpec((tm, cw), lambda i: (i, 0)),
                  pl.BlockSpec((tm, cw), lambda i: (i, 0)),
                  pl.BlockSpec((tm, cw), lambda i: (i, 0)),
                  pl.BlockSpec((1, cw), const),
                  pl.BlockSpec((1, cw), const),
                  resident(ones),
                  pl.BlockSpec((tm, d), lambda i: (i, gblk)),
                  pl.BlockSpec((tm, d), lambda i: (i, gblk + 1)),
                  pl.BlockSpec((tm, d), lambda i: (i, 0)),
                  pl.BlockSpec((1, 1, d), lambda i: (i // bpr, 0, 0)),
                  pl.BlockSpec((1, d), const),
                  resident(wa), resident(wr), resident(wo)],
        out_specs=pl.BlockSpec((tm, d), lambda i: (i, 0)),
        out_shape=jax.ShapeDtypeStruct((m, d), F32),
        compiler_params=_cparams(("parallel",)),
    )(att, yf, yb, bv, g, lnw, lnb, ones, proj, proj, x, gt, gain, wa, wr, wo)


def _ffn_kernel(x_ref, xp_ref, xn_ref, gpre_ref, sc_ref, sh_ref, wg_ref, wu_ref, cw_ref, cb_ref, wd_ref,
                gt_ref, gpost_ref, o_ref, h_ref, *, tm, seq):
    i = pl.program_id(0)
    j = pl.program_id(1)
    halo = SUBLANES
    bpr = seq // tm

    @pl.when(j == 0)
    def _():
        gain = gpre_ref[...]
        sc = 1.0 + sc_ref[0]
        sh = sh_ref[0]
        has_prev = (i % bpr != 0).astype(F32)
        has_next = (i % bpr != bpr - 1).astype(F32)
        h_ref[0:halo, :] = ((_rms(xp_ref[...], gain) * sc + sh) * has_prev).astype(BF16)
        h_ref[halo:halo + tm, :] = (_rms(x_ref[...], gain) * sc + sh).astype(BF16)
        h_ref[halo + tm:, :] = ((_rms(xn_ref[...], gain) * sc + sh) * has_next).astype(BF16)
        o_ref[...] = jnp.zeros_like(o_ref)

    ext = tm + 2 * halo
    gate = jnp.dot(h_ref[...], wg_ref[...], preferred_element_type=F32)
    prev = pltpu.roll(gate, 1, 0)[halo:halo + tm]
    nxt = pltpu.roll(gate, ext - 1, 0)[halo:halo + tm]
    u = cw_ref[0:1, :] * prev + cw_ref[1:2, :] * gate[halo:halo + tm] + cw_ref[2:3, :] * nxt + cb_ref[...]
    up = jnp.dot(h_ref[halo:halo + tm, :], wu_ref[...], preferred_element_type=F32)
    act = jax.nn.gelu(u, approximate=True) * up
    o_ref[...] += jnp.dot(act.astype(BF16), wd_ref[...], preferred_element_type=F32)

    @pl.when(j == pl.num_programs(1) - 1)
    def _():
        o_ref[...] = x_ref[...] + gt_ref[0] * _rms(o_ref[...], gpost_ref[...])


def _ffn(x, gpre, sc, sh, wg, wu, cw, cb, wd, gt, gpost, seq, tm=512, tf=FF_TILE):
    m, d = x.shape
    f = wg.shape[1]
    bpr = seq // tm
    hb = tm // SUBLANES
    nhb = m // SUBLANES
    const = lambda i, j: (0, 0)
    bidx = lambda i, j: (i // bpr, 0, 0)
    return pl.pallas_call(
        functools.partial(_ffn_kernel, tm=tm, seq=seq),
        name="ffn",
        grid=(m // tm, f // tf),
        in_specs=[pl.BlockSpec((tm, d), lambda i, j: (i, 0)),
                  pl.BlockSpec((SUBLANES, d), lambda i, j: (jnp.maximum(i * hb - 1, 0), 0)),
                  pl.BlockSpec((SUBLANES, d), lambda i, j: (jnp.minimum((i + 1) * hb, nhb - 1), 0)),
                  pl.BlockSpec((1, d), const),
                  pl.BlockSpec((1, 1, d), bidx),
                  pl.BlockSpec((1, 1, d), bidx),
                  pl.BlockSpec((d, tf), lambda i, j: (0, j)),
                  pl.BlockSpec((d, tf), lambda i, j: (0, j)),
                  pl.BlockSpec((SUBLANES, tf), lambda i, j: (0, j)),
                  pl.BlockSpec((1, tf), lambda i, j: (0, j)),
                  pl.BlockSpec((tf, d), lambda i, j: (j, 0)),
                  pl.BlockSpec((1, 1, d), bidx),
                  pl.BlockSpec((1, d), const)],
        out_specs=pl.BlockSpec((tm, d), lambda i, j: (i, 0)),
        out_shape=jax.ShapeDtypeStruct((m, d), F32),
        scratch_shapes=[pltpu.VMEM((tm + 2 * SUBLANES, d), BF16)],
        compiler_params=_cparams(("parallel", "arbitrary")),
    )(x, x, x, gpre, sc, sh, wg, wu, cw, cb, wd, gt, gpost)


def _rope_tables(seq):
    half = ATT_HEAD_DIM // 2
    inv_freq = 1.0 / (ROPE_THETA ** (jnp.arange(half, dtype=F32) / half))
    ang = jnp.arange(seq, dtype=F32)[:, None] * inv_freq[None, :]
    cos = jnp.cos(ang)
    sin = jnp.sin(ang)
    return jnp.concatenate([cos, cos], axis=-1), jnp.concatenate([-sin, sin], axis=-1)


def _trunk(x, ada, p):
    bsz, seq, d = x.shape
    m = bsz * seq
    sh1, sc1, gt1, sh2, sc2, gt2 = [t[:, None, :] for t in jnp.split(ada, 6, axis=-1)]
    x2 = x.reshape(m, d)
    proj = _inproj(x2, p['ln_mix_pre'], sc1, sh1, p['w_in'], seq)
    proj3 = proj.reshape(bsz, seq, N_IN_PAD)
    cos, sin = _rope_tables(seq)
    att = _attention(proj3, cos, sin)
    r, v, kk, g, bv, lw, kd, b = _prep(proj, p['shift_mu'], p['decay_base'], p['decay_up'], p['iclr_base'],
                                       p['iclr_up'], p['gate_up'], p['k_k'], p['k_a'], p['r_k'], p['head_ones'], seq)
    per_seq = lambda t: t.reshape(t.shape[:-2] + (bsz, seq, RW_WIDTH))
    yf, yb = _rwkv_scan(per_seq(r), per_seq(v), per_seq(kk), per_seq(lw), per_seq(kd), per_seq(b))
    x1 = _mix(att.reshape(m, ATT_OUT), yf.reshape(m, RW_WIDTH), yb.reshape(m, RW_WIDTH), bv, g,
              p['lnx_w'], p['lnx_b'], p['head_ones'],
              proj, x2, gt1, p['ln_mix_post'], p['w_att_branch'], p['w_rwkv_branch'], p['w_out'], seq)
    y = _ffn(x1, p['ln_ffn_pre'], sc2, sh2, p['w_ffn_gate'], p['w_ffn_up'], p['ffn_conv_w'], p['ffn_conv_b'],
             p['w_ffn_down'], gt2, p['ln_ffn_post'], seq)
    return y.reshape(bsz, seq, d)


def kernel(x_prompt, x_sample, c_prompt, c_sample, ln_mix_pre, ln_mix_post, ln_ffn_pre, ln_ffn_post,
           w_ada, b_ada, w_in, shift_mu, decay_base, decay_up, iclr_base, iclr_up, gate_up,
           k_k, k_a, r_k, lnx_w, lnx_b, w_att_branch, w_rwkv_branch, w_out,
           w_ffn_gate, w_ffn_up, ffn_conv_w, ffn_conv_b, w_ffn_down):
    depth = w_in.shape[0]
    nb_p = c_prompt.shape[0]
    nb_s = c_sample.shape[0]
    c_all = jnp.concatenate([c_prompt, c_sample], axis=0)
    c_all = jnp.pad(c_all, ((0, (-c_all.shape[0]) % SUBLANES), (0, 0)))
    fpad = D_FF_PAD - D_FF
    xp, xs = x_prompt, x_sample
    lane_head = jnp.arange(MXU_DIM_V7X) // RW_HEAD
    head_ones = (lane_head[:, None] == lane_head[None, :]).astype(BF16)
    zdl = jnp.zeros((DECAY_LORA, RW_WIDTH), F32)
    zil = jnp.zeros((ICLR_LORA, RW_WIDTH), F32)
    for l in range(depth):
        att_end = 3 * ATT_WIDTH
        rw_end = att_end + RW_COLS
        w_in_b = w_in[l].astype(BF16)
        rw_cols = jnp.pad(w_in_b[:, att_end:rw_end], ((0, 0), (0, LAT_PAD - LAT_COLS)))
        w_in_l = jnp.concatenate([w_in_b[:, rw_end:], rw_cols, w_in_b[:, :att_end]], axis=1)
        dup = jnp.block([[decay_up[l][0], zdl], [zdl, decay_up[l][1]]])
        dup_hi = dup.astype(BF16)
        p = dict(
            ln_mix_pre=ln_mix_pre[l][None], ln_mix_post=ln_mix_post[l][None],
            ln_ffn_pre=ln_ffn_pre[l][None], ln_ffn_post=ln_ffn_post[l][None],
            w_in=w_in_l, head_ones=head_ones,
            shift_mu=jnp.pad(shift_mu[l], ((0, 0), (0, RW_COLS_PAD - RW_COLS))),
            decay_base=decay_base[l].reshape(1, 2 * RW_WIDTH),
            decay_up=jnp.stack([dup_hi, (dup - dup_hi.astype(F32)).astype(BF16)]),
            iclr_base=iclr_base[l].reshape(1, 2 * RW_WIDTH),
            iclr_up=jnp.block([[iclr_up[l][0], zil], [zil, iclr_up[l][1]]]).astype(BF16),
            gate_up=jnp.pad(gate_up[l], ((0, LAT_PAD - LAT_COLS), (0, 0))).astype(BF16),
            k_k=k_k[l][None], k_a=k_a[l][None], r_k=r_k[l].reshape(1, RW_WIDTH),
            lnx_w=lnx_w[l][None], lnx_b=lnx_b[l][None],
            w_att_branch=w_att_branch[l].astype(BF16), w_rwkv_branch=w_rwkv_branch[l].astype(BF16),
            w_out=w_out[l].astype(BF16),
            w_ffn_gate=jnp.pad(w_ffn_gate[l], ((0, 0), (0, fpad))).astype(BF16),
            w_ffn_up=jnp.pad(w_ffn_up[l], ((0, 0), (0, fpad))).astype(BF16),
            ffn_conv_w=jnp.pad(ffn_conv_w[l], ((0, SUBLANES - 3), (0, fpad))),
            ffn_conv_b=jnp.pad(ffn_conv_b[l], ((0, fpad),))[None],
            w_ffn_down=jnp.pad(w_ffn_down[l], ((0, fpad), (0, 0))).astype(BF16),
        )
        ada = _ada(c_all, w_ada[l], b_ada[l][None])
        xp = _trunk(xp, ada[:nb_p], p)
        xs = _trunk(xs, ada[nb_p:nb_p + nb_s], p)
    return (xp, xs)
```

```python
import functools
import math

import jax
import jax.numpy as jnp
from jax import lax
from jax.experimental import pallas as pl
from jax.experimental.pallas import tpu as pltpu

F32 = jnp.float32
BF16 = jnp.bfloat16

D_MODEL = 2048
ATT_GROUPS = ((128, 1), (512, 4), (2048, 16))
ATT_HEADS_PER_GROUP = 4
ATT_HEAD_DIM = 128
ATT_HEADS = ATT_HEADS_PER_GROUP * len(ATT_GROUPS)
ATT_WIDTH = ATT_HEADS * ATT_HEAD_DIM
ATT_OUT = ATT_HEADS_PER_GROUP * ATT_HEAD_DIM
ROPE_THETA = 10000.0
RW_HEAD = 64
RW_WIDTH = D_MODEL // 2
RW_HEADS = RW_WIDTH // RW_HEAD
DECAY_LORA = 64
ICLR_LORA = 64
GATE_LORA = 160
RW_COLS = 3 * RW_WIDTH + 2 * DECAY_LORA + 2 * ICLR_LORA + GATE_LORA
N_IN = 3 * ATT_WIDTH + RW_COLS + 2 * D_MODEL
D_FF = ((8 * D_MODEL // 3 + 127) // 128) * 128
RMS_EPS = 1e-6
GN_EPS = 64e-5
NEG_INF = -1e30
DECAY_SCALE = math.exp(-0.5)

LANES = 128
SUBLANES = 8
BF16_ROWS = 16
MXU_DIM_V7X = 256
VMEM_BYTES_V7X = 64 * 1024 * 1024
VMEM_LIMIT = VMEM_BYTES_V7X - 4 * 1024 * 1024

GATE_OFF = 0
RW_OFF = GATE_OFF + 2 * D_MODEL
LAT_OFF = RW_OFF + 3 * RW_WIDTH
LAT_COLS = RW_COLS - 3 * RW_WIDTH
LAT_PAD = 512
ATT_OFF = LAT_OFF + LAT_PAD
N_IN_PAD = ATT_OFF + 3 * ATT_WIDTH
RW_COLS_PAD = 3 * RW_WIDTH + LAT_PAD
FF_TILE = 512
D_FF_PAD = -(-D_FF // FF_TILE) * FF_TILE
RW_CHUNK = 64
RW_ROWS_PER_STEP = 2
ATT_SPAN = 64
ATT_QBLOCK = 128


def _cparams(sem):
    return pltpu.CompilerParams(dimension_semantics=sem, vmem_limit_bytes=VMEM_LIMIT)


def _rms(x, gain):
    return x * lax.rsqrt(jnp.mean(x * x, axis=-1, keepdims=True) + RMS_EPS) * gain


def _row_chunks(first, last, fn, unroll=4):
    if unroll is None:
        for c in range(first, last):
            fn(c * BF16_ROWS)
        return

    def body(c, carry):
        fn(pl.multiple_of(c * BF16_ROWS, BF16_ROWS))
        return carry

    lax.fori_loop(first, last, body, 0, unroll=min(unroll, last - first))


def _ada_kernel(c_ref, w_ref, b_ref, o_ref):
    c = c_ref[...]
    s = c * jax.nn.sigmoid(c)
    o_ref[...] = jnp.dot(s.astype(BF16), w_ref[...].astype(BF16),
                         preferred_element_type=F32) + b_ref[...]


def _ada(c, w, b, tn=1024):
    m, k = c.shape
    n = w.shape[1]
    return pl.pallas_call(
        _ada_kernel,
        name="ada",
        grid=(n // tn,),
        in_specs=[pl.BlockSpec((m, k), lambda j: (0, 0)),
                  pl.BlockSpec((k, tn), lambda j: (0, j)),
                  pl.BlockSpec((1, tn), lambda j: (0, j))],
        out_specs=pl.BlockSpec((m, tn), lambda j: (0, j)),
        out_shape=jax.ShapeDtypeStruct((m, n), F32),
        compiler_params=_cparams(("parallel",)),
    )(c, w, b)


def _inproj_kernel(x_ref, g_ref, sc_ref, sh_ref, w_ref, o_ref, h_ref, *, part, stride):
    i = pl.program_id(0)
    j = pl.program_id(1)

    def norm_rows(slot, start, nrows, unroll):
        def chunk(r0):
            rows = pl.ds(pl.multiple_of(start + r0, BF16_ROWS), BF16_ROWS)
            h = _rms(x_ref[rows, :], g_ref[...]) * (1.0 + sc_ref[0]) + sh_ref[0]
            h_ref[slot, rows, :] = h.astype(BF16)

        _row_chunks(0, nrows // BF16_ROWS, chunk, unroll)

    @pl.when((i == 0) & (j == 0))
    def _():
        norm_rows(0, 0, x_ref.shape[0], 4)

    o_ref[...] = jnp.dot(h_ref[i % 2], w_ref[...], preferred_element_type=F32)

    norm_rows((i + 1) % 2, pl.multiple_of(jnp.maximum(j - 1, 0) * stride, BF16_ROWS), part, None)


def _inproj(x, gain, sc, sh, w, seq, tm=1024, tn=1536):
    m, k = x.shape
    n = w.shape[1]
    bpr = seq // tm
    ni = m // tm
    nsl = n // tn - 1
    part = -(-tm // (nsl * BF16_ROWS)) * BF16_ROWS
    stride = (tm - part) // (nsl - 1)
    assert stride * (nsl - 1) + part == tm and stride % BF16_ROWS == 0 and stride <= part
    ahead = lambda i, j: jnp.minimum(i + jnp.minimum(j, 1), ni - 1)
    return pl.pallas_call(
        functools.partial(_inproj_kernel, part=part, stride=stride),
        name="inproj",
        grid=(ni, n // tn),
        in_specs=[pl.BlockSpec((tm, k), lambda i, j: (ahead(i, j), 0)),
                  pl.BlockSpec((1, k), lambda i, j: (0, 0)),
                  pl.BlockSpec((1, 1, k), lambda i, j: (ahead(i, j) // bpr, 0, 0)),
                  pl.BlockSpec((1, 1, k), lambda i, j: (ahead(i, j) // bpr, 0, 0)),
                  pl.BlockSpec((k, tn), lambda i, j: (0, j))],
        out_specs=pl.BlockSpec((tm, tn), lambda i, j: (i, j)),
        out_shape=jax.ShapeDtypeStruct((m, n), F32),
        scratch_shapes=[pltpu.VMEM((2, tm, k), BF16)],
        compiler_params=_cparams(("arbitrary", "arbitrary")),
    )(x, gain, sc, sh, w)


def _attn_kernel(q_ref, k_ref, v_ref, cos_ref, sin_ref, o_ref, qr_ref, kr_ref, og_ref, lse_ref, *, seq):
    gid = pl.program_id(2)
    rows = 512
    half = ATT_HEAD_DIM // 2

    def rot(i, _):
        sl = pl.ds(pl.multiple_of(i * rows, rows), rows)
        cos = cos_ref[sl, :]
        sin = sin_ref[sl, :]
        q = q_ref[0, sl, :]
        k = k_ref[0, sl, :]
        qr_ref[sl, :] = (q * cos + pltpu.roll(q, half, 1) * sin) * (ATT_HEAD_DIM ** -0.5)
        kr_ref[sl, :] = k * cos + pltpu.roll(k, half, 1) * sin
        return 0

    lax.fori_loop(0, seq // rows, rot, 0)

    def group(gi, dil):
        cls_len = seq // dil
        bq = min(ATT_QBLOCK, cls_len)
        if cls_len <= bq + 2 * ATT_SPAN:
            bq = cls_len
        bk = min(bq + 2 * ATT_SPAN, cls_len)
        nqb = cls_len // bq
        rel0 = (lax.broadcasted_iota(jnp.int32, (bq, bk), 0)
                - lax.broadcasted_iota(jnp.int32, (bq, bk), 1))
        ones = jnp.ones((bk, ATT_HEAD_DIM), BF16)

        def body(it, _):
            cls = it // nqb
            p0 = (it % nqb) * bq
            ks = jnp.clip(p0 - ATT_SPAN, 0, cls_len - bk)
            if dil == 1:
                rq = pl.ds(p0, bq)
                rk = pl.ds(ks, bk)
            else:
                rq = pl.ds(cls + dil * p0, bq, stride=dil)
                rk = pl.ds(cls + dil * ks, bk, stride=dil)
            qb = qr_ref[rq, :].astype(BF16)
            kb = kr_ref[rk, :].astype(BF16)
            vb = v_ref[0, rk, :].astype(BF16)
            s = lax.dot_general(qb, kb, (((1,), (1,)), ((), ())), preferred_element_type=F32)
            band = jnp.abs(rel0 + (p0 - ks)) <= ATT_SPAN
            s = jnp.where(band, s, NEG_INF)
            m = jnp.max(s, axis=-1, keepdims=True)
            p = jnp.exp(s - m)
            pv = jnp.dot(p.astype(BF16), jnp.concatenate([vb, ones], axis=1), preferred_element_type=F32)
            l = pv[:, ATT_HEAD_DIM:]
            og_ref[gi, rq, :] = pv[:, :ATT_HEAD_DIM] / l
            lse_ref[gi, rq, :] = m + jnp.log(l)
            return 0

        lax.fori_loop(0, dil * nqb, body, 0, unroll=min(8, dil * nqb))

    for gi, (_, dil) in enumerate(ATT_GROUPS):
        @pl.when(gid == gi)
        def _(gi=gi, dil=dil):
            group(gi, dil)

    @pl.when(gid == len(ATT_GROUPS) - 1)
    def _():
        def comb(i, _):
            sl = pl.ds(pl.multiple_of(i * rows, rows), rows)
            l0 = lse_ref[0, sl, :]
            l1 = lse_ref[1, sl, :]
            l2 = lse_ref[2, sl, :]
            mx = jnp.maximum(jnp.maximum(l0, l1), l2)
            w0 = jnp.exp(l0 - mx)
            w1 = jnp.exp(l1 - mx)
            w2 = jnp.exp(l2 - mx)
            num = w0 * og_ref[0, sl, :] + w1 * og_ref[1, sl, :] + w2 * og_ref[2, sl, :]
            o_ref[0, sl, :] = (num / (w0 + w1 + w2)).astype(o_ref.dtype)
            return 0

        lax.fori_loop(0, seq // rows, comb, 0)


def _attention(proj3, cos, sin):
    bsz, seq, _ = proj3.shape
    hpg = ATT_HEADS_PER_GROUP
    ng = len(ATT_GROUPS)
    blk = (1, seq, ATT_HEAD_DIM)
    q0 = ATT_OFF // ATT_HEAD_DIM
    return pl.pallas_call(
        functools.partial(_attn_kernel, seq=seq),
        name="attn",
        grid=(bsz, hpg, ng),
        in_specs=[pl.BlockSpec(blk, lambda b, s, g: (b, 0, q0 + g * hpg + s)),
                  pl.BlockSpec(blk, lambda b, s, g: (b, 0, q0 + ATT_HEADS + g * hpg + s)),
                  pl.BlockSpec(blk, lambda b, s, g: (b, 0, q0 + 2 * ATT_HEADS + g * hpg + s)),
                  pl.BlockSpec((seq, ATT_HEAD_DIM), lambda b, s, g: (0, 0)),
                  pl.BlockSpec((seq, ATT_HEAD_DIM), lambda b, s, g: (0, 0))],
        out_specs=pl.BlockSpec(blk, lambda b, s, g: (b, 0, s)),
        out_shape=jax.ShapeDtypeStruct((bsz, seq, ATT_OUT), BF16),
        scratch_shapes=[pltpu.VMEM((seq, ATT_HEAD_DIM), F32),
                        pltpu.VMEM((seq, ATT_HEAD_DIM), F32),
                        pltpu.VMEM((ng, seq, ATT_HEAD_DIM), F32),
                        pltpu.VMEM((ng, seq, ATT_HEAD_DIM), F32)],
        compiler_params=_cparams(("parallel", "parallel", "arbitrary")),
    )(proj3, proj3, proj3, cos, sin)


def _rwkv_kernel(rf_ref, vf_ref, kkf_ref, lwf_ref, kdf_ref, bf_ref, rb_ref, vb_ref, kkb_ref, lwb_ref, kdb_ref,
                 bb_ref, yf_ref, yb_ref, s_ref):
    cn = RW_CHUNK
    c2 = 2 * cn

    @pl.when(pl.program_id(1) == 0)
    def _():
        s_ref[...] = jnp.zeros_like(s_ref)

    ti = lax.broadcasted_iota(jnp.int32, (cn, cn), 0)
    si = lax.broadcasted_iota(jnp.int32, (cn, cn), 1)
    t2 = lax.broadcasted_iota(jnp.int32, (c2, c2), 0)
    s2 = lax.broadcasted_iota(jnp.int32, (c2, c2), 1)
    eye = (t2 == s2).astype(F32)
    same_blk = (t2 // cn) == (s2 // cn)
    top = t2 < cn
    lane = lax.broadcasted_iota(jnp.int32, (1, LANES), 1)
    first = lane < RW_HEAD
    same_head = ((lax.broadcasted_iota(jnp.int32, (LANES, LANES), 0) // RW_HEAD)
                 == (lax.broadcasted_iota(jnp.int32, (LANES, LANES), 1) // RW_HEAD))

    def stack(x):
        return jnp.concatenate([jnp.where(first, x, 0.0), jnp.where(first, 0.0, x)], axis=0)

    def mm(a, b):
        return jnp.dot(a.astype(BF16), b.astype(BF16), preferred_element_type=F32)

    def mm_nt(a, b):
        return lax.dot_general(a.astype(BF16), b.astype(BF16), (((1,), (1,)), ((), ())),
                               preferred_element_type=F32)

    lanes = [slice(hp * LANES, (hp + 1) * LANES) for hp in range(RW_WIDTH // LANES)]
    inst = []
    xq, vs, bk, decay_end, amat, strict, incl = [], [], [], [], [], [], []
    for z, (r_ref, v_ref, kk_ref, lw_ref, kd_ref, b_ref, y_ref) in enumerate(
            ((rf_ref, vf_ref, kkf_ref, lwf_ref, kdf_ref, bf_ref, yf_ref),
             (rb_ref, vb_ref, kkb_ref, lwb_ref, kdb_ref, bb_ref, yb_ref))):
        rev = z == 1
        tri = ((si >= ti) if rev else (si <= ti)).astype(BF16)
        order = (t2 % cn - s2 % cn) if rev else (s2 % cn - t2 % cn)
        strict_z = (order < 0) & same_blk
        incl_z = (order <= 0) & same_blk
        for row in range(r_ref.shape[0]):
            lw_all = lw_ref[0, row]
            lw_hi = lw_all.astype(BF16)
            lw_lo = (lw_all - lw_hi.astype(F32)).astype(BF16)
            g_all = (jnp.dot(tri, lw_hi, preferred_element_type=F32)
                     + jnp.dot(tri, lw_lo, preferred_element_type=F32))
            for hp, ls in enumerate(lanes):
                r = r_ref[row, :, ls]
                v = v_ref[row, :, ls]
                kk = kk_ref[row, :, ls]
                kd = kd_ref[0, row, :, ls]
                b = b_ref[0, row, :, ls]
                g = g_all[:, ls]
                g_end = g[0:1] if rev else g[cn - 1:cn]
                e_neg = jnp.exp(-g)
                e_end = jnp.exp(g_end - g)
                x = jnp.concatenate([stack(kk * jnp.exp(g - lw_all[:, ls])), stack(r * jnp.exp(g))],
                                    axis=0).astype(BF16)
                y = jnp.concatenate([b * e_neg, kd * e_neg], axis=0).astype(BF16)
                inst.append((z, row, hp, ls, v_ref, y_ref))
                strict.append(strict_z)
                incl.append(incl_z)
                xq.append(x)
                vs.append(stack(v).astype(BF16))
                bk.append(jnp.concatenate([b * e_end, kd * e_end], axis=0).astype(BF16))
                decay_end.append(jnp.exp(g_end))
                amat.append(lax.dot_general(x, y, (((1,), (1,)), ((), ())), preferred_element_type=F32))
    swapped = [pltpu.roll(am, cn, 1) for am in amat]
    tmat = [jnp.where(st, jnp.where(top, am[:c2], sw[:c2]), 0.0) for am, sw, st in zip(amat, swapped, strict)]
    rest = [jnp.concatenate([jnp.where(st, jnp.where(top, sw[:c2], am[:c2]), 0.0),
                             jnp.where(ic, jnp.where(top, sw[c2:], am[c2:]), 0.0)], axis=0).astype(BF16)
            for am, sw, st, ic in zip(amat, swapped, strict, incl)]
    arb = [jnp.where(ic, jnp.where(top, am[c2:], sw[c2:]), 0.0).astype(BF16)
           for am, sw, ic in zip(amat, swapped, incl)]
    ninv = [eye - t for t in tmat]
    pw = [t.astype(BF16) for t in tmat]
    pw = [jnp.dot(p, p, preferred_element_type=F32).astype(BF16) for p in pw]
    nsq = cn.bit_length() - 2
    for it in range(nsq):
        if it < nsq - 1:
            both = [jnp.dot(p, jnp.concatenate([p, n.astype(BF16)], axis=1), preferred_element_type=F32)
                    for p, n in zip(pw, ninv)]
            pw = [bo[:, :c2].astype(BF16) for bo in both]
            ninv = [n + bo[:, c2:] for n, bo in zip(ninv, both)]
        else:
            ninv = [n + jnp.dot(p, n.astype(BF16), preferred_element_type=F32) for p, n in zip(pw, ninv)]
    mv = [jnp.dot(m, v, preferred_element_type=F32) for m, v in zip(rest, vs)]
    s0 = [s_ref[z, row, hp] for z, row, hp, _, _, _ in inst]
    xs = [mm_nt(x, s) for x, s in zip(xq, s0)]
    us = [mm(n, -x[:c2] - m[:c2]) for n, x, m in zip(ninv, xs, mv)]
    ys = [x[c2:] + m[c2:] + mm(ab, u) for x, m, ab, u in zip(xs, mv, arb, us)]
    for n, (z, row, hp, ls, v_ref, y_ref) in enumerate(inst):
        y_ref[row, :, ls] = ys[n][:cn] + ys[n][cn:]
        uv = jnp.concatenate([us[n][:cn] + us[n][cn:], v_ref[row, :, ls].astype(F32)], axis=0)
        upd = mm(uv.T, bk[n])
        s_ref[z, row, hp] = jnp.where(same_head, s0[n] * decay_end[n] + upd, 0.0)


def _rwkv_scan(r, v, kk, lw, kd, b):
    bsz, seq, width = r.shape
    cn = RW_CHUNK
    nc = seq // cn
    rows = RW_ROWS_PER_STEP
    assert bsz % rows == 0
    fwd = pl.BlockSpec((rows, cn, width), lambda b, c: (b, c, 0))
    bwd = pl.BlockSpec((rows, cn, width), lambda b, c: (b, nc - 1 - c, 0))
    fwd_d = pl.BlockSpec((1, rows, cn, width), lambda b, c: (0, b, c, 0))
    bwd_d = pl.BlockSpec((1, rows, cn, width), lambda b, c: (1, b, nc - 1 - c, 0))
    out = jax.ShapeDtypeStruct((bsz, seq, width), F32)
    return pl.pallas_call(
        _rwkv_kernel,
        name="rwkv",
        grid=(bsz // rows, nc),
        in_specs=[fwd, fwd, fwd, fwd_d, fwd_d, fwd_d, bwd, bwd, bwd, bwd_d, bwd_d, bwd_d],
        out_specs=[fwd, bwd],
        out_shape=[out, out],
        scratch_shapes=[pltpu.VMEM((2, rows, width // LANES, LANES, LANES), F32)],
        compiler_params=_cparams(("parallel", "arbitrary")),
    )(r, v, kk, lw, kd, b, r, v, kk, lw, kd, b)


def _head_sum(x, ones, passes):
    gw = ones.shape[0]
    cols = []
    for c0 in range(0, x.shape[1], gw):
        xc = x[:, c0:c0 + gw]
        hi = xc.astype(BF16)
        out = jnp.dot(hi, ones, preferred_element_type=F32)
        if passes == 2:
            lo = (xc - hi.astype(F32)).astype(BF16)
            out = out + jnp.dot(lo, ones, preferred_element_type=F32)
        cols.append(out)
    return jnp.concatenate(cols, axis=1)


def _prep_kernel(r_ref, k_ref, v_ref, l_ref, rp_ref, kp_ref, vp_ref, lp_ref, rn_ref, kn_ref, vn_ref, ln_ref,
                 mu_ref, dbase_ref, dup_ref, ibase_ref, iup_ref, gup_ref, kkw_ref, ka_ref, rk_ref, ones_ref,
                 ro_ref, vo_ref, kko_ref, go_ref, bvo_ref, lwo_ref, kdo_ref, bo_ref, *, tm, seq):
    i = pl.program_id(0)
    bpr = seq // tm
    has_prev = (i % bpr != 0).astype(F32)
    has_next = (i % bpr != bpr - 1).astype(F32)
    rowi = lax.broadcasted_iota(jnp.int32, (tm, 1), 0)
    cw = RW_WIDTH

    def shifted(main_ref, prev_ref, next_ref, c0, c1):
        z = main_ref[...]
        zp = jnp.where(rowi == 0, prev_ref[SUBLANES - 1:SUBLANES, :] * has_prev, pltpu.roll(z, 1, 0))
        zn = jnp.where(rowi == tm - 1, next_ref[0:1, :] * has_next, pltpu.roll(z, tm - 1, 0))
        return z + mu_ref[0:1, c0:c1] * (zp - z) + mu_ref[1:2, c0:c1] * (zn - z)

    r = shifted(r_ref, rp_ref, rn_ref, 0, cw)
    k = shifted(k_ref, kp_ref, kn_ref, cw, 2 * cw)
    v = shifted(v_ref, vp_ref, vn_ref, 2 * cw, 3 * cw)
    lat = shifted(l_ref, lp_ref, ln_ref, 3 * cw, 3 * cw + LAT_PAD)
    nd, ni = 2 * DECAY_LORA, 2 * ICLR_LORA
    wl = jnp.tanh(lat[:, :nd])
    wl_hi = wl.astype(BF16)
    wl_lo = (wl - wl_hi.astype(F32)).astype(BF16)
    w_raw = (dbase_ref[...] + jnp.dot(wl_hi, dup_ref[0], preferred_element_type=F32)
             + jnp.dot(wl_lo, dup_ref[0], preferred_element_type=F32)
             + jnp.dot(wl_hi, dup_ref[1], preferred_element_type=F32))
    lw = jax.nn.sigmoid(w_raw) * (-DECAY_SCALE)
    a = jax.nn.sigmoid(ibase_ref[...] + jnp.dot(lat[:, nd:nd + ni].astype(BF16), iup_ref[...],
                                                preferred_element_type=F32))
    g = jnp.dot(jax.nn.sigmoid(lat[:, nd + ni:]).astype(BF16), gup_ref[...], preferred_element_type=F32)
    ones = ones_ref[...]
    kkr = k * kkw_ref[...]
    kk = kkr / jnp.maximum(jnp.sqrt(_head_sum(kkr * kkr, ones, 2)), 1e-12)
    ka = ka_ref[...]
    k_fix = k * (1.0 - ka)
    k_var = k * ka
    kd0 = k_fix + k_var * a[:, :cw]
    kd1 = k_fix + k_var * a[:, cw:]
    bonus = _head_sum(r * (kd0 + kd1) * rk_ref[...], ones, 2)
    ro_ref[...] = r
    vo_ref[...] = v.astype(vo_ref.dtype)
    kko_ref[...] = kk
    go_ref[...] = g.astype(go_ref.dtype)
    bvo_ref[...] = (bonus * v).astype(bvo_ref.dtype)
    lwo_ref[0] = lw[:, :cw]
    lwo_ref[1] = lw[:, cw:]
    kdo_ref[0] = kd0
    kdo_ref[1] = kd1
    bo_ref[0] = kk * a[:, :cw]
    bo_ref[1] = kk * a[:, cw:]


def _prep(proj, mu, dbase, dup, ibase, iup, gup, kkw, ka, rk, ones, seq, tm=256):
    m = proj.shape[0]
    cw = RW_WIDTH
    bpr = seq // tm
    hb = tm // SUBLANES
    nhb = m // SUBLANES
    r0 = RW_OFF // cw
    l0 = LAT_OFF // LAT_PAD
    cols = [(cw, r0), (cw, r0 + 1), (cw, r0 + 2), (LAT_PAD, l0)]
    main = [pl.BlockSpec((tm, w), lambda i, c=c: (i, c)) for w, c in cols]
    prev = [pl.BlockSpec((SUBLANES, w), lambda i, c=c: (jnp.maximum(i * hb - 1, 0), c)) for w, c in cols]
    nxt = [pl.BlockSpec((SUBLANES, w), lambda i, c=c: (jnp.minimum((i + 1) * hb, nhb - 1), c)) for w, c in cols]
    params = [mu, dbase, dup, ibase, iup, gup, kkw, ka, rk, ones]
    pspecs = [pl.BlockSpec(t.shape, lambda i, n=t.ndim: (0,) * n) for t in params]
    oblk = pl.BlockSpec((tm, cw), lambda i: (i, 0))
    dblk = pl.BlockSpec((2, tm, cw), lambda i: (0, i, 0))
    one = jax.ShapeDtypeStruct((m, cw), F32)
    half = jax.ShapeDtypeStruct((m, cw), BF16)
    two = jax.ShapeDtypeStruct((2, m, cw), F32)
    return pl.pallas_call(
        functools.partial(_prep_kernel, tm=tm, seq=seq),
        name="rwkv_prep",
        grid=(m // tm,),
        in_specs=main + prev + nxt + pspecs,
        out_specs=[oblk] * 5 + [dblk] * 3,
        out_shape=[one, half, one, half, half] + [two] * 3,
        compiler_params=_cparams(("parallel",)),
    )(*([proj] * 12), *params)


def _mix_kernel(att_ref, yf_ref, yb_ref, bv_ref, g_ref, lnw_ref, lnb_ref, ones_ref, ga_ref, gr_ref, x_ref, gt_ref,
                gain_ref, wa_ref, wr_ref, wo_ref, o_ref):
    ones = ones_ref[...]
    y = yf_ref[...] + yb_ref[...]
    mu = _head_sum(y, ones, 1) * (1.0 / RW_HEAD)
    yc = y - mu
    var = _head_sum(yc * yc, ones, 1) * (1.0 / RW_HEAD)
    rw = ((yc * lax.rsqrt(var + GN_EPS)) * lnw_ref[...] + lnb_ref[...] + bv_ref[...]) * g_ref[...]
    pa = jnp.dot(att_ref[...], wa_ref[...], preferred_element_type=F32)
    pr = jnp.dot(rw.astype(BF16), wr_ref[...], preferred_element_type=F32)
    merged = jax.nn.sigmoid(ga_ref[...]) * pa + jax.nn.sigmoid(gr_ref[...]) * pr
    mix = jnp.dot(merged.astype(BF16), wo_ref[...], preferred_element_type=F32)
    o_ref[...] = x_ref[...] + gt_ref[0] * _rms(mix, gain_ref[...])


def _mix(att, yf, yb, bv, g, lnw, lnb, ones, proj, x, gt, gain, wa, wr, wo, seq, tm=256):
    m, d = x.shape
    cw = RW_WIDTH
    bpr = seq // tm
    gblk = GATE_OFF // d
    const = lambda i: (0, 0)
    resident = lambda t: pl.BlockSpec(t.shape, const, pipeline_mode=pl.Buffered(1))
    return pl.pallas_call(
        _mix_kernel,
        name="mix",
        grid=(m // tm,),
        in_specs=[pl.BlockSpec((tm, att.shape[1]), lambda i: (i, 0)),
                  pl.BlockSpec((tm, cw), lambda i: (i, 0)),
                  pl.BlockSpec((tm, cw), lambda i: (i, 0)),
                  pl.BlockSpec((tm, cw), lambda i: (i, 0)),
                  pl.BlockSpec((tm, cw), lambda i: (i, 0)),
                  pl.BlockSpec((1, cw), const),
                  pl.BlockSpec((1, cw), const),
                  resident(ones),
                  pl.BlockSpec((tm, d), lambda i: (i, gblk)),
                  pl.BlockSpec((tm, d), lambda i: (i, gblk + 1)),
                  pl.BlockSpec((tm, d), lambda i: (i, 0)),
                  pl.BlockSpec((1, 1, d), lambda i: (i // bpr, 0, 0)),
                  pl.BlockSpec((1, d), const),
                  resident(wa), resident(wr), resident(wo)],
        out_specs=pl.BlockSpec((tm, d), lambda i: (i, 0)),
        out_shape=jax.ShapeDtypeStruct((m, d), F32),
        compiler_params=_cparams(("parallel",)),
    )(att, yf, yb, bv, g, lnw, lnb, ones, proj, proj, x, gt, gain, wa, wr, wo)


def _ffn_kernel(x_ref, xp_ref, xn_ref, gpre_ref, sc_ref, sh_ref, wg_ref, wu_ref, cw_ref, cb_ref, wd_ref,
                gt_ref, gpost_ref, o_ref, h_ref, *, tm, seq):
    i = pl.program_id(0)
    j = pl.program_id(1)
    halo = SUBLANES
    bpr = seq // tm

    @pl.when(j == 0)
    def _():
        gain = gpre_ref[...]
        sc = 1.0 + sc_ref[0]
        sh = sh_ref[0]
        has_prev = (i % bpr != 0).astype(F32)
        has_next = (i % bpr != bpr - 1).astype(F32)
        h_ref[0:halo, :] = ((_rms(xp_ref[...], gain) * sc + sh) * has_prev).astype(BF16)
        h_ref[halo:halo + tm, :] = (_rms(x_ref[...], gain) * sc + sh).astype(BF16)
        h_ref[halo + tm:, :] = ((_rms(xn_ref[...], gain) * sc + sh) * has_next).astype(BF16)
        o_ref[...] = jnp.zeros_like(o_ref)

    ext = tm + 2 * halo
    gate = jnp.dot(h_ref[...], wg_ref[...], preferred_element_type=F32)
    prev = pltpu.roll(gate, 1, 0)[halo:halo + tm]
    nxt = pltpu.roll(gate, ext - 1, 0)[halo:halo + tm]
    u = cw_ref[0:1, :] * prev + cw_ref[1:2, :] * gate[halo:halo + tm] + cw_ref[2:3, :] * nxt + cb_ref[...]
    up = jnp.dot(h_ref[halo:halo + tm, :], wu_ref[...], preferred_element_type=F32)
    act = jax.nn.gelu(u, approximate=True) * up
    o_ref[...] += jnp.dot(act.astype(BF16), wd_ref[...], preferred_element_type=F32)

    @pl.when(j == pl.num_programs(1) - 1)
    def _():
        o_ref[...] = x_ref[...] + gt_ref[0] * _rms(o_ref[...], gpost_ref[...])


def _ffn(x, gpre, sc, sh, wg, wu, cw, cb, wd, gt, gpost, seq, tm=1024, tf=FF_TILE):
    m, d = x.shape
    f = wg.shape[1]
    bpr = seq // tm
    hb = tm // SUBLANES
    nhb = m // SUBLANES
    const = lambda i, j: (0, 0)
    bidx = lambda i, j: (i // bpr, 0, 0)
    return pl.pallas_call(
        functools.partial(_ffn_kernel, tm=tm, seq=seq),
        name="ffn",
        grid=(m // tm, f // tf),
        in_specs=[pl.BlockSpec((tm, d), lambda i, j: (i, 0), pipeline_mode=pl.Buffered(1)),
                  pl.BlockSpec((SUBLANES, d), lambda i, j: (jnp.maximum(i * hb - 1, 0), 0)),
                  pl.BlockSpec((SUBLANES, d), lambda i, j: (jnp.minimum((i + 1) * hb, nhb - 1), 0)),
                  pl.BlockSpec((1, d), const),
                  pl.BlockSpec((1, 1, d), bidx),
                  pl.BlockSpec((1, 1, d), bidx),
                  pl.BlockSpec((d, tf), lambda i, j: (0, j)),
                  pl.BlockSpec((d, tf), lambda i, j: (0, j)),
                  pl.BlockSpec((SUBLANES, tf), lambda i, j: (0, j)),
                  pl.BlockSpec((1, tf), lambda i, j: (0, j)),
                  pl.BlockSpec((tf, d), lambda i, j: (j, 0)),
                  pl.BlockSpec((1, 1, d), bidx),
                  pl.BlockSpec((1, d), const)],
        out_specs=pl.BlockSpec((tm, d), lambda i, j: (i, 0)),
        out_shape=jax.ShapeDtypeStruct((m, d), F32),
        scratch_shapes=[pltpu.VMEM((tm + 2 * SUBLANES, d), BF16)],
        compiler_params=_cparams(("parallel", "arbitrary")),
    )(x, x, x, gpre, sc, sh, wg, wu, cw, cb, wd, gt, gpost)


def _rope_tables(seq):
    half = ATT_HEAD_DIM // 2
    inv_freq = 1.0 / (ROPE_THETA ** (jnp.arange(half, dtype=F32) / half))
    ang = jnp.arange(seq, dtype=F32)[:, None] * inv_freq[None, :]
    cos = jnp.cos(ang)
    sin = jnp.sin(ang)
    return jnp.concatenate([cos, cos], axis=-1), jnp.concatenate([-sin, sin], axis=-1)


def _trunk(x, ada, p):
    bsz, seq, d = x.shape
    m = bsz * seq
    sh1, sc1, gt1, sh2, sc2, gt2 = [t[:, None, :] for t in jnp.split(ada, 6, axis=-1)]
    x2 = x.reshape(m, d)
    proj = _inproj(x2, p['ln_mix_pre'], sc1, sh1, p['w_in'], seq)
    proj3 = proj.reshape(bsz, seq, N_IN_PAD)
    cos, sin = _rope_tables(seq)
    att = _attention(proj3, cos, sin)
    r, v, kk, g, bv, lw, kd, b = _prep(proj, p['shift_mu'], p['decay_base'], p['decay_up'], p['iclr_base'],
                                       p['iclr_up'], p['gate_up'], p['k_k'], p['k_a'], p['r_k'], p['head_ones'], seq)
    per_seq = lambda t: t.reshape(t.shape[:-2] + (bsz, seq, RW_WIDTH))
    yf, yb = _rwkv_scan(per_seq(r), per_seq(v), per_seq(kk), per_seq(lw), per_seq(kd), per_seq(b))
    x1 = _mix(att.reshape(m, ATT_OUT), yf.reshape(m, RW_WIDTH), yb.reshape(m, RW_WIDTH), bv, g,
              p['lnx_w'], p['lnx_b'], p['head_ones'],
              proj, x2, gt1, p['ln_mix_post'], p['w_att_branch'], p['w_rwkv_branch'], p['w_out'], seq)
    y = _ffn(x1, p['ln_ffn_pre'], sc2, sh2, p['w_ffn_gate'], p['w_ffn_up'], p['ffn_conv_w'], p['ffn_conv_b'],
             p['w_ffn_down'], gt2, p['ln_ffn_post'], seq)
    return y.reshape(bsz, seq, d)


def kernel(x_prompt, x_sample, c_prompt, c_sample, ln_mix_pre, ln_mix_post, ln_ffn_pre, ln_ffn_post,
           w_ada, b_ada, w_in, shift_mu, decay_base, decay_up, iclr_base, iclr_up, gate_up,
           k_k, k_a, r_k, lnx_w, lnx_b, w_att_branch, w_rwkv_branch, w_out,
           w_ffn_gate, w_ffn_up, ffn_conv_w, ffn_conv_b, w_ffn_down):
    depth = w_in.shape[0]
    nb_p = c_prompt.shape[0]
    nb_s = c_sample.shape[0]
    c_all = jnp.concatenate([c_prompt, c_sample], axis=0)
    c_all = jnp.pad(c_all, ((0, (-c_all.shape[0]) % SUBLANES), (0, 0)))
    fpad = D_FF_PAD - D_FF
    xp, xs = x_prompt, x_sample
    lane_head = jnp.arange(MXU_DIM_V7X) // RW_HEAD
    head_ones = (lane_head[:, None] == lane_head[None, :]).astype(BF16)
    zdl = jnp.zeros((DECAY_LORA, RW_WIDTH), F32)
    zil = jnp.zeros((ICLR_LORA, RW_WIDTH), F32)
    for l in range(depth):
        att_end = 3 * ATT_WIDTH
        rw_end = att_end + RW_COLS
        w_in_b = w_in[l].astype(BF16)
        rw_cols = jnp.pad(w_in_b[:, att_end:rw_end], ((0, 0), (0, LAT_PAD - LAT_COLS)))
        w_in_l = jnp.concatenate([w_in_b[:, rw_end:], rw_cols, w_in_b[:, :att_end]], axis=1)
        dup = jnp.block([[decay_up[l][0], zdl], [zdl, decay_up[l][1]]])
        dup_hi = dup.astype(BF16)
        p = dict(
            ln_mix_pre=ln_mix_pre[l][None], ln_mix_post=ln_mix_post[l][None],
            ln_ffn_pre=ln_ffn_pre[l][None], ln_ffn_post=ln_ffn_post[l][None],
            w_in=w_in_l, head_ones=head_ones,
            shift_mu=jnp.pad(shift_mu[l], ((0, 0), (0, RW_COLS_PAD - RW_COLS))),
            decay_base=decay_base[l].reshape(1, 2 * RW_WIDTH),
            decay_up=jnp.stack([dup_hi, (dup - dup_hi.astype(F32)).astype(BF16)]),
            iclr_base=iclr_base[l].reshape(1, 2 * RW_WIDTH),
            iclr_up=jnp.block([[iclr_up[l][0], zil], [zil, iclr_up[l][1]]]).astype(BF16),
            gate_up=jnp.pad(gate_up[l], ((0, LAT_PAD - LAT_COLS), (0, 0))).astype(BF16),
            k_k=k_k[l][None], k_a=k_a[l][None], r_k=r_k[l].reshape(1, RW_WIDTH),
            lnx_w=lnx_w[l][None], lnx_b=lnx_b[l][None],
            w_att_branch=w_att_branch[l].astype(BF16), w_rwkv_branch=w_rwkv_branch[l].astype(BF16),
            w_out=w_out[l].astype(BF16),
            w_ffn_gate=jnp.pad(w_ffn_gate[l], ((0, 0), (0, fpad))).astype(BF16),
            w_ffn_up=jnp.pad(w_ffn_up[l], ((0, 0), (0, fpad))).astype(BF16),
            ffn_conv_w=jnp.pad(ffn_conv_w[l], ((0, SUBLANES - 3), (0, fpad))),
            ffn_conv_b=jnp.pad(ffn_conv_b[l], ((0, fpad),))[None],
            w_ffn_down=jnp.pad(w_ffn_down[l], ((0, fpad), (0, 0))).astype(BF16),
        )
        ada = _ada(c_all, w_ada[l], b_ada[l][None])
        xp = _trunk(xp, ada[:nb_p], p)
        xs = _trunk(xs, ada[nb_p:nb_p + nb_s], p)
    return (xp, xs)
```

```python
import functools
import math

import jax
import jax.numpy as jnp
from jax import lax
from jax.experimental import pallas as pl
from jax.experimental.pallas import tpu as pltpu

F32 = jnp.float32
BF16 = jnp.bfloat16

D_MODEL = 2048
ATT_GROUPS = ((128, 1), (512, 4), (2048, 16))
ATT_HEADS_PER_GROUP = 4
ATT_HEAD_DIM = 128
ATT_HEADS = ATT_HEADS_PER_GROUP * len(ATT_GROUPS)
ATT_WIDTH = ATT_HEADS * ATT_HEAD_DIM
ATT_OUT = ATT_HEADS_PER_GROUP * ATT_HEAD_DIM
ROPE_THETA = 10000.0
RW_HEAD = 64
RW_WIDTH = D_MODEL // 2
RW_HEADS = RW_WIDTH // RW_HEAD
DECAY_LORA = 64
ICLR_LORA = 64
GATE_LORA = 160
RW_COLS = 3 * RW_WIDTH + 2 * DECAY_LORA + 2 * ICLR_LORA + GATE_LORA
N_IN = 3 * ATT_WIDTH + RW_COLS + 2 * D_MODEL
D_FF = ((8 * D_MODEL // 3 + 127) // 128) * 128
RMS_EPS = 1e-6
GN_EPS = 64e-5
NEG_INF = -1e30
DECAY_SCALE = math.exp(-0.5)

LANES = 128
SUBLANES = 8
BF16_ROWS = 16
MXU_DIM_V7X = 256
VMEM_BYTES_V7X = 64 * 1024 * 1024
VMEM_COMPILER_RESERVE = 4 * 1024 * 1024
VMEM_LIMIT = VMEM_BYTES_V7X - VMEM_COMPILER_RESERVE

GATE_OFF = 0
RW_OFF = GATE_OFF + 2 * D_MODEL
LAT_OFF = RW_OFF + 3 * RW_WIDTH
LAT_COLS = RW_COLS - 3 * RW_WIDTH
LAT_PAD = 512
ATT_OFF = LAT_OFF + LAT_PAD
N_IN_PAD = ATT_OFF + 3 * ATT_WIDTH
RW_COLS_PAD = 3 * RW_WIDTH + LAT_PAD
FF_TILE = 512
D_FF_PAD = -(-D_FF // FF_TILE) * FF_TILE
RW_CHUNK = 64
RW_ROWS_PER_STEP = 2
ATT_SPAN = 64
ATT_QBLOCK = 128


def _cparams(sem):
    return pltpu.CompilerParams(dimension_semantics=sem, vmem_limit_bytes=VMEM_LIMIT)


def _rms(x, gain):
    return x * lax.rsqrt(jnp.mean(x * x, axis=-1, keepdims=True) + RMS_EPS) * gain


def _row_chunks(first, last, fn, unroll=4):
    if unroll is None:
        for c in range(first, last):
            fn(c * BF16_ROWS)
        return

    def body(c, carry):
        fn(pl.multiple_of(c * BF16_ROWS, BF16_ROWS))
        return carry

    lax.fori_loop(first, last, body, 0, unroll=min(unroll, last - first))


def _ada_kernel(c_ref, w_ref, b_ref, o_ref):
    c = c_ref[...]
    s = c * jax.nn.sigmoid(c)
    o_ref[...] = jnp.dot(s.astype(BF16), w_ref[...].astype(BF16),
                         preferred_element_type=F32) + b_ref[...]


def _ada(c, w, b, tn=1024):
    m, k = c.shape
    n = w.shape[1]
    return pl.pallas_call(
        _ada_kernel,
        name="ada",
        grid=(n // tn,),
        in_specs=[pl.BlockSpec((m, k), lambda j: (0, 0)),
                  pl.BlockSpec((k, tn), lambda j: (0, j)),
                  pl.BlockSpec((1, tn), lambda j: (0, j))],
        out_specs=pl.BlockSpec((m, tn), lambda j: (0, j)),
        out_shape=jax.ShapeDtypeStruct((m, n), F32),
        compiler_params=_cparams(("parallel",)),
    )(c, w, b)


def _inproj_kernel(x_ref, g_ref, sc_ref, sh_ref, w_ref, o_ref, h_ref, *, part, stride):
    i = pl.program_id(0)
    j = pl.program_id(1)

    def norm_rows(slot, start, nrows, unroll):
        def chunk(r0):
            rows = pl.ds(pl.multiple_of(start + r0, BF16_ROWS), BF16_ROWS)
            h = _rms(x_ref[rows, :], g_ref[...]) * (1.0 + sc_ref[0]) + sh_ref[0]
            h_ref[slot, rows, :] = h.astype(BF16)

        _row_chunks(0, nrows // BF16_ROWS, chunk, unroll)

    @pl.when((i == 0) & (j == 0))
    def _():
        norm_rows(0, 0, x_ref.shape[0], 4)

    o_ref[...] = jnp.dot(h_ref[i % 2], w_ref[...], preferred_element_type=F32)

    norm_rows((i + 1) % 2, pl.multiple_of(jnp.maximum(j - 1, 0) * stride, BF16_ROWS), part, None)


def _inproj(x, gain, sc, sh, w, seq, tm=1024, tn=1536):
    m, k = x.shape
    n = w.shape[1]
    bpr = seq // tm
    ni = m // tm
    nsl = n // tn - 1
    part = -(-tm // (nsl * BF16_ROWS)) * BF16_ROWS
    stride = (tm - part) // (nsl - 1)
    assert stride * (nsl - 1) + part == tm and stride % BF16_ROWS == 0 and stride <= part
    ahead = lambda i, j: jnp.minimum(i + jnp.minimum(j, 1), ni - 1)
    return pl.pallas_call(
        functools.partial(_inproj_kernel, part=part, stride=stride),
        name="inproj",
        grid=(ni, n // tn),
        in_specs=[pl.BlockSpec((tm, k), lambda i, j: (ahead(i, j), 0)),
                  pl.BlockSpec((1, k), lambda i, j: (0, 0)),
                  pl.BlockSpec((1, 1, k), lambda i, j: (ahead(i, j) // bpr, 0, 0)),
                  pl.BlockSpec((1, 1, k), lambda i, j: (ahead(i, j) // bpr, 0, 0)),
                  pl.BlockSpec((k, tn), lambda i, j: (0, j))],
        out_specs=pl.BlockSpec((tm, tn), lambda i, j: (i, j)),
        out_shape=jax.ShapeDtypeStruct((m, n), F32),
        scratch_shapes=[pltpu.VMEM((2, tm, k), BF16)],
        compiler_params=_cparams(("arbitrary", "arbitrary")),
    )(x, gain, sc, sh, w)


def _attn_kernel(q_ref, k_ref, v_ref, cos_ref, sin_ref, o_ref, qr_ref, kr_ref, og_ref, lse_ref, *, seq):
    gid = pl.program_id(2)
    rows = 512
    half = ATT_HEAD_DIM // 2

    def rot(i, _):
        sl = pl.ds(pl.multiple_of(i * rows, rows), rows)
        cos = cos_ref[sl, :]
        sin = sin_ref[sl, :]
        q = q_ref[0, sl, :]
        k = k_ref[0, sl, :]
        qr_ref[sl, :] = (q * cos + pltpu.roll(q, half, 1) * sin) * (ATT_HEAD_DIM ** -0.5)
        kr_ref[sl, :] = k * cos + pltpu.roll(k, half, 1) * sin
        return 0

    lax.fori_loop(0, seq // rows, rot, 0)

    def group(gi, dil):
        cls_len = seq // dil
        bq = min(ATT_QBLOCK, cls_len)
        if cls_len <= bq + 2 * ATT_SPAN:
            bq = cls_len
        bk = min(bq + 2 * ATT_SPAN, cls_len)
        nqb = cls_len // bq
        rel0 = (lax.broadcasted_iota(jnp.int32, (bq, bk), 0)
                - lax.broadcasted_iota(jnp.int32, (bq, bk), 1))
        ones = jnp.ones((bk, ATT_HEAD_DIM), BF16)

        def body(it, _):
            cls = it // nqb
            p0 = (it % nqb) * bq
            ks = jnp.clip(p0 - ATT_SPAN, 0, cls_len - bk)
            if dil == 1:
                rq = pl.ds(p0, bq)
                rk = pl.ds(ks, bk)
            else:
                rq = pl.ds(cls + dil * p0, bq, stride=dil)
                rk = pl.ds(cls + dil * ks, bk, stride=dil)
            qb = qr_ref[rq, :].astype(BF16)
            kb = kr_ref[rk, :].astype(BF16)
            vb = v_ref[0, rk, :].astype(BF16)
            s = lax.dot_general(qb, kb, (((1,), (1,)), ((), ())), preferred_element_type=F32)
            band = jnp.abs(rel0 + (p0 - ks)) <= ATT_SPAN
            s = jnp.where(band, s, NEG_INF)
            m = jnp.max(s, axis=-1, keepdims=True)
            p = jnp.exp(s - m)
            pv = jnp.dot(p.astype(BF16), jnp.concatenate([vb, ones], axis=1), preferred_element_type=F32)
            l = pv[:, ATT_HEAD_DIM:]
            og_ref[gi, rq, :] = pv[:, :ATT_HEAD_DIM] / l
            lse_ref[gi, rq, :] = m + jnp.log(l)
            return 0

        lax.fori_loop(0, dil * nqb, body, 0, unroll=min(8, dil * nqb))

    for gi, (_, dil) in enumerate(ATT_GROUPS):
        @pl.when(gid == gi)
        def _(gi=gi, dil=dil):
            group(gi, dil)

    @pl.when(gid == len(ATT_GROUPS) - 1)
    def _():
        def comb(i, _):
            sl = pl.ds(pl.multiple_of(i * rows, rows), rows)
            l0 = lse_ref[0, sl, :]
            l1 = lse_ref[1, sl, :]
            l2 = lse_ref[2, sl, :]
            mx = jnp.maximum(jnp.maximum(l0, l1), l2)
            w0 = jnp.exp(l0 - mx)
            w1 = jnp.exp(l1 - mx)
            w2 = jnp.exp(l2 - mx)
            num = w0 * og_ref[0, sl, :] + w1 * og_ref[1, sl, :] + w2 * og_ref[2, sl, :]
            o_ref[0, sl, :] = (num / (w0 + w1 + w2)).astype(o_ref.dtype)
            return 0

        lax.fori_loop(0, seq // rows, comb, 0)


def _attention(proj3, cos, sin):
    bsz, seq, _ = proj3.shape
    hpg = ATT_HEADS_PER_GROUP
    ng = len(ATT_GROUPS)
    blk = (1, seq, ATT_HEAD_DIM)
    q0 = ATT_OFF // ATT_HEAD_DIM
    return pl.pallas_call(
        functools.partial(_attn_kernel, seq=seq),
        name="attn",
        grid=(bsz, hpg, ng),
        in_specs=[pl.BlockSpec(blk, lambda b, s, g: (b, 0, q0 + g * hpg + s)),
                  pl.BlockSpec(blk, lambda b, s, g: (b, 0, q0 + ATT_HEADS + g * hpg + s)),
                  pl.BlockSpec(blk, lambda b, s, g: (b, 0, q0 + 2 * ATT_HEADS + g * hpg + s)),
                  pl.BlockSpec((seq, ATT_HEAD_DIM), lambda b, s, g: (0, 0)),
                  pl.BlockSpec((seq, ATT_HEAD_DIM), lambda b, s, g: (0, 0))],
        out_specs=pl.BlockSpec(blk, lambda b, s, g: (b, 0, s)),
        out_shape=jax.ShapeDtypeStruct((bsz, seq, ATT_OUT), BF16),
        scratch_shapes=[pltpu.VMEM((seq, ATT_HEAD_DIM), F32),
                        pltpu.VMEM((seq, ATT_HEAD_DIM), F32),
                        pltpu.VMEM((ng, seq, ATT_HEAD_DIM), F32),
                        pltpu.VMEM((ng, seq, ATT_HEAD_DIM), F32)],
        compiler_params=_cparams(("parallel", "parallel", "arbitrary")),
    )(proj3, proj3, proj3, cos, sin)


def _rwkv_kernel(rf_ref, vf_ref, kkf_ref, lwf_ref, kdf_ref, bf_ref, rb_ref, vb_ref, kkb_ref, lwb_ref, kdb_ref,
                 bb_ref, yf_ref, yb_ref, s_ref):
    cn = RW_CHUNK
    c2 = 2 * cn

    @pl.when(pl.program_id(1) == 0)
    def _():
        s_ref[...] = jnp.zeros_like(s_ref)

    ti = lax.broadcasted_iota(jnp.int32, (cn, cn), 0)
    si = lax.broadcasted_iota(jnp.int32, (cn, cn), 1)
    t2 = lax.broadcasted_iota(jnp.int32, (c2, c2), 0)
    s2 = lax.broadcasted_iota(jnp.int32, (c2, c2), 1)
    eye = (t2 == s2).astype(F32)
    same_blk = (t2 // cn) == (s2 // cn)
    top = t2 < cn
    lane = lax.broadcasted_iota(jnp.int32, (1, LANES), 1)
    first = lane < RW_HEAD
    same_head = ((lax.broadcasted_iota(jnp.int32, (LANES, LANES), 0) // RW_HEAD)
                 == (lax.broadcasted_iota(jnp.int32, (LANES, LANES), 1) // RW_HEAD))

    def stack(x):
        return jnp.concatenate([jnp.where(first, x, 0.0), jnp.where(first, 0.0, x)], axis=0)

    def mm(a, b):
        return jnp.dot(a.astype(BF16), b.astype(BF16), preferred_element_type=F32)

    def mm_nt(a, b):
        return lax.dot_general(a.astype(BF16), b.astype(BF16), (((1,), (1,)), ((), ())),
                               preferred_element_type=F32)

    lanes = [slice(hp * LANES, (hp + 1) * LANES) for hp in range(RW_WIDTH // LANES)]
    inst = []
    xq, vs, bk, decay_end, amat, strict, incl = [], [], [], [], [], [], []
    for z, (r_ref, v_ref, kk_ref, lw_ref, kd_ref, b_ref, y_ref) in enumerate(
            ((rf_ref, vf_ref, kkf_ref, lwf_ref, kdf_ref, bf_ref, yf_ref),
             (rb_ref, vb_ref, kkb_ref, lwb_ref, kdb_ref, bb_ref, yb_ref))):
        rev = z == 1
        tri = ((si >= ti) if rev else (si <= ti)).astype(BF16)
        order = (t2 % cn - s2 % cn) if rev else (s2 % cn - t2 % cn)
        strict_z = (order < 0) & same_blk
        incl_z = (order <= 0) & same_blk
        for row in range(r_ref.shape[0]):
            lw_all = lw_ref[0, row]
            lw_hi = lw_all.astype(BF16)
            lw_lo = (lw_all - lw_hi.astype(F32)).astype(BF16)
            g_all = (jnp.dot(tri, lw_hi, preferred_element_type=F32)
                     + jnp.dot(tri, lw_lo, preferred_element_type=F32))
            for hp, ls in enumerate(lanes):
                r = r_ref[row, :, ls]
                v = v_ref[row, :, ls]
                kk = kk_ref[row, :, ls]
                kd = kd_ref[0, row, :, ls]
                b = b_ref[0, row, :, ls]
                g = g_all[:, ls]
                g_end = g[0:1] if rev else g[cn - 1:cn]
                e_neg = jnp.exp(-g)
                e_end = jnp.exp(g_end - g)
                x = jnp.concatenate([stack(kk * jnp.exp(g - lw_all[:, ls])), stack(r * jnp.exp(g))],
                                    axis=0).astype(BF16)
                y = jnp.concatenate([b * e_neg, kd * e_neg], axis=0).astype(BF16)
                inst.append((z, row, hp, ls, v_ref, y_ref))
                strict.append(strict_z)
                incl.append(incl_z)
                xq.append(x)
                vs.append(stack(v).astype(BF16))
                bk.append(jnp.concatenate([b * e_end, kd * e_end], axis=0).astype(BF16))
                decay_end.append(jnp.exp(g_end))
                amat.append(lax.dot_general(x, y, (((1,), (1,)), ((), ())), preferred_element_type=F32))
    swapped = [pltpu.roll(am, cn, 1) for am in amat]
    tmat = [jnp.where(st, jnp.where(top, am[:c2], sw[:c2]), 0.0) for am, sw, st in zip(amat, swapped, strict)]
    rest = [jnp.concatenate([jnp.where(st, jnp.where(top, sw[:c2], am[:c2]), 0.0),
                             jnp.where(ic, jnp.where(top, sw[c2:], am[c2:]), 0.0)], axis=0).astype(BF16)
            for am, sw, st, ic in zip(amat, swapped, strict, incl)]
    arb = [jnp.where(ic, jnp.where(top, am[c2:], sw[c2:]), 0.0).astype(BF16)
           for am, sw, ic in zip(amat, swapped, incl)]
    ninv = [eye - t for t in tmat]
    pw = [t.astype(BF16) for t in tmat]
    pw = [jnp.dot(p, p, preferred_element_type=F32).astype(BF16) for p in pw]
    nsq = cn.bit_length() - 2
    for it in range(nsq):
        if it < nsq - 1:
            both = [jnp.dot(p, jnp.concatenate([p, n.astype(BF16)], axis=1), preferred_element_type=F32)
                    for p, n in zip(pw, ninv)]
            pw = [bo[:, :c2].astype(BF16) for bo in both]
            ninv = [n + bo[:, c2:] for n, bo in zip(ninv, both)]
        else:
            ninv = [n + jnp.dot(p, n.astype(BF16), preferred_element_type=F32) for p, n in zip(pw, ninv)]
    mv = [jnp.dot(m, v, preferred_element_type=F32) for m, v in zip(rest, vs)]
    s0 = [s_ref[z, row, hp] for z, row, hp, _, _, _ in inst]
    xs = [mm_nt(x, s) for x, s in zip(xq, s0)]
    us = [mm(n, -x[:c2] - m[:c2]) for n, x, m in zip(ninv, xs, mv)]
    ys = [x[c2:] + m[c2:] + mm(ab, u) for x, m, ab, u in zip(xs, mv, arb, us)]
    for n, (z, row, hp, ls, v_ref, y_ref) in enumerate(inst):
        y_ref[row, :, ls] = ys[n][:cn] + ys[n][cn:]
        uv = jnp.concatenate([us[n][:cn] + us[n][cn:], v_ref[row, :, ls].astype(F32)], axis=0)
        upd = mm(uv.T, bk[n])
        s_ref[z, row, hp] = jnp.where(same_head, s0[n] * decay_end[n] + upd, 0.0)


def _rwkv_scan(r, v, kk, lw, kd, b):
    bsz, seq, width = r.shape
    cn = RW_CHUNK
    nc = seq // cn
    rows = RW_ROWS_PER_STEP
    assert bsz % rows == 0
    fwd = pl.BlockSpec((rows, cn, width), lambda b, c: (b, c, 0))
    bwd = pl.BlockSpec((rows, cn, width), lambda b, c: (b, nc - 1 - c, 0))
    fwd_d = pl.BlockSpec((1, rows, cn, width), lambda b, c: (0, b, c, 0))
    bwd_d = pl.BlockSpec((1, rows, cn, width), lambda b, c: (1, b, nc - 1 - c, 0))
    out = jax.ShapeDtypeStruct((bsz, seq, width), F32)
    return pl.pallas_call(
        _rwkv_kernel,
        name="rwkv",
        grid=(bsz // rows, nc),
        in_specs=[fwd, fwd, fwd, fwd_d, fwd_d, fwd_d, bwd, bwd, bwd, bwd_d, bwd_d, bwd_d],
        out_specs=[fwd, bwd],
        out_shape=[out, out],
        scratch_shapes=[pltpu.VMEM((2, rows, width // LANES, LANES, LANES), F32)],
        compiler_params=_cparams(("parallel", "arbitrary")),
    )(r, v, kk, lw, kd, b, r, v, kk, lw, kd, b)


def _head_sum(x, ones, passes):
    gw = ones.shape[0]
    cols = []
    for c0 in range(0, x.shape[1], gw):
        xc = x[:, c0:c0 + gw]
        hi = xc.astype(BF16)
        out = jnp.dot(hi, ones, preferred_element_type=F32)
        if passes == 2:
            lo = (xc - hi.astype(F32)).astype(BF16)
            out = out + jnp.dot(lo, ones, preferred_element_type=F32)
        cols.append(out)
    return jnp.concatenate(cols, axis=1)


def _prep_kernel(r_ref, k_ref, v_ref, l_ref, rp_ref, kp_ref, vp_ref, lp_ref, rn_ref, kn_ref, vn_ref, ln_ref,
                 mu_ref, dbase_ref, dup_ref, ibase_ref, iup_ref, gup_ref, kkw_ref, ka_ref, rk_ref, ones_ref,
                 ro_ref, vo_ref, kko_ref, go_ref, bvo_ref, lwo_ref, kdo_ref, bo_ref, *, tm, seq):
    i = pl.program_id(0)
    bpr = seq // tm
    has_prev = (i % bpr != 0).astype(F32)
    has_next = (i % bpr != bpr - 1).astype(F32)
    rowi = lax.broadcasted_iota(jnp.int32, (tm, 1), 0)
    cw = RW_WIDTH

    def shifted(main_ref, prev_ref, next_ref, c0, c1):
        z = main_ref[...]
        zp = jnp.where(rowi == 0, prev_ref[SUBLANES - 1:SUBLANES, :] * has_prev, pltpu.roll(z, 1, 0))
        zn = jnp.where(rowi == tm - 1, next_ref[0:1, :] * has_next, pltpu.roll(z, tm - 1, 0))
        return z + mu_ref[0:1, c0:c1] * (zp - z) + mu_ref[1:2, c0:c1] * (zn - z)

    r = shifted(r_ref, rp_ref, rn_ref, 0, cw)
    k = shifted(k_ref, kp_ref, kn_ref, cw, 2 * cw)
    v = shifted(v_ref, vp_ref, vn_ref, 2 * cw, 3 * cw)
    lat = shifted(l_ref, lp_ref, ln_ref, 3 * cw, 3 * cw + LAT_PAD)
    nd, ni = 2 * DECAY_LORA, 2 * ICLR_LORA
    wl = jnp.tanh(lat[:, :nd])
    wl_hi = wl.astype(BF16)
    wl_lo = (wl - wl_hi.astype(F32)).astype(BF16)
    w_raw = (dbase_ref[...] + jnp.dot(wl_hi, dup_ref[0], preferred_element_type=F32)
             + jnp.dot(wl_lo, dup_ref[0], preferred_element_type=F32)
             + jnp.dot(wl_hi, dup_ref[1], preferred_element_type=F32))
    lw = jax.nn.sigmoid(w_raw) * (-DECAY_SCALE)
    a = jax.nn.sigmoid(ibase_ref[...] + jnp.dot(lat[:, nd:nd + ni].astype(BF16), iup_ref[...],
                                                preferred_element_type=F32))
    g = jnp.dot(jax.nn.sigmoid(lat[:, nd + ni:]).astype(BF16), gup_ref[...], preferred_element_type=F32)
    ones = ones_ref[...]
    kkr = k * kkw_ref[...]
    kk = kkr / jnp.maximum(jnp.sqrt(_head_sum(kkr * kkr, ones, 2)), 1e-12)
    ka = ka_ref[...]
    k_fix = k * (1.0 - ka)
    k_var = k * ka
    kd0 = k_fix + k_var * a[:, :cw]
    kd1 = k_fix + k_var * a[:, cw:]
    bonus = _head_sum(r * (kd0 + kd1) * rk_ref[...], ones, 2)
    ro_ref[...] = r
    vo_ref[...] = v.astype(vo_ref.dtype)
    kko_ref[...] = kk
    go_ref[...] = g.astype(go_ref.dtype)
    bvo_ref[...] = (bonus * v).astype(bvo_ref.dtype)
    lwo_ref[0] = lw[:, :cw]
    lwo_ref[1] = lw[:, cw:]
    kdo_ref[0] = kd0
    kdo_ref[1] = kd1
    bo_ref[0] = kk * a[:, :cw]
    bo_ref[1] = kk * a[:, cw:]


def _prep(proj, mu, dbase, dup, ibase, iup, gup, kkw, ka, rk, ones, seq, tm=256):
    m = proj.shape[0]
    cw = RW_WIDTH
    bpr = seq // tm
    hb = tm // SUBLANES
    nhb = m // SUBLANES
    r0 = RW_OFF // cw
    l0 = LAT_OFF // LAT_PAD
    cols = [(cw, r0), (cw, r0 + 1), (cw, r0 + 2), (LAT_PAD, l0)]
    main = [pl.BlockSpec((tm, w), lambda i, c=c: (i, c)) for w, c in cols]
    prev = [pl.BlockSpec((SUBLANES, w), lambda i, c=c: (jnp.maximum(i * hb - 1, 0), c)) for w, c in cols]
    nxt = [pl.BlockSpec((SUBLANES, w), lambda i, c=c: (jnp.minimum((i + 1) * hb, nhb - 1), c)) for w, c in cols]
    params = [mu, dbase, dup, ibase, iup, gup, kkw, ka, rk, ones]
    pspecs = [pl.BlockSpec(t.shape, lambda i, n=t.ndim: (0,) * n) for t in params]
    oblk = pl.BlockSpec((tm, cw), lambda i: (i, 0))
    dblk = pl.BlockSpec((2, tm, cw), lambda i: (0, i, 0))
    one = jax.ShapeDtypeStruct((m, cw), F32)
    half = jax.ShapeDtypeStruct((m, cw), BF16)
    two = jax.ShapeDtypeStruct((2, m, cw), F32)
    return pl.pallas_call(
        functools.partial(_prep_kernel, tm=tm, seq=seq),
        name="rwkv_prep",
        grid=(m // tm,),
        in_specs=main + prev + nxt + pspecs,
        out_specs=[oblk] * 5 + [dblk] * 3,
        out_shape=[one, half, one, half, half] + [two] * 3,
        compiler_params=_cparams(("parallel",)),
    )(*([proj] * 12), *params)


def _mix_kernel(att_ref, yf_ref, yb_ref, bv_ref, g_ref, lnw_ref, lnb_ref, ones_ref, ga_ref, gr_ref, x_ref, gt_ref,
                gain_ref, wa_ref, wr_ref, wo_ref, o_ref):
    ones = ones_ref[...]
    y = yf_ref[...] + yb_ref[...]
    mu = _head_sum(y, ones, 1) * (1.0 / RW_HEAD)
    yc = y - mu
    var = _head_sum(yc * yc, ones, 1) * (1.0 / RW_HEAD)
    rw = ((yc * lax.rsqrt(var + GN_EPS)) * lnw_ref[...] + lnb_ref[...] + bv_ref[...]) * g_ref[...]
    pa = jnp.dot(att_ref[...], wa_ref[...], preferred_element_type=F32)
    pr = jnp.dot(rw.astype(BF16), wr_ref[...], preferred_element_type=F32)
    merged = jax.nn.sigmoid(ga_ref[...]) * pa + jax.nn.sigmoid(gr_ref[...]) * pr
    mix = jnp.dot(merged.astype(BF16), wo_ref[...], preferred_element_type=F32)
    o_ref[...] = x_ref[...] + gt_ref[0] * _rms(mix, gain_ref[...])


def _mix(att, yf, yb, bv, g, lnw, lnb, ones, proj, x, gt, gain, wa, wr, wo, seq, tm=256):
    m, d = x.shape
    cw = RW_WIDTH
    bpr = seq // tm
    gblk = GATE_OFF // d
    const = lambda i: (0, 0)
    resident = lambda t: pl.BlockSpec(t.shape, const, pipeline_mode=pl.Buffered(1))
    return pl.pallas_call(
        _mix_kernel,
        name="mix",
        grid=(m // tm,),
        in_specs=[pl.BlockSpec((tm, att.shape[1]), lambda i: (i, 0)),
                  pl.BlockSpec((tm, cw), lambda i: (i, 0)),
                  pl.BlockSpec((tm, cw), lambda i: (i, 0)),
                  pl.BlockSpec((tm, cw), lambda i: (i, 0)),
                  pl.BlockSpec((tm, cw), lambda i: (i, 0)),
                  pl.BlockSpec((1, cw), const),
                  pl.BlockSpec((1, cw), const),
                  resident(ones),
                  pl.BlockSpec((tm, d), lambda i: (i, gblk)),
                  pl.BlockSpec((tm, d), lambda i: (i, gblk + 1)),
                  pl.BlockSpec((tm, d), lambda i: (i, 0)),
                  pl.BlockSpec((1, 1, d), lambda i: (i // bpr, 0, 0)),
                  pl.BlockSpec((1, d), const),
                  resident(wa), resident(wr), resident(wo)],
        out_specs=pl.BlockSpec((tm, d), lambda i: (i, 0)),
        out_shape=jax.ShapeDtypeStruct((m, d), F32),
        compiler_params=_cparams(("parallel",)),
    )(att, yf, yb, bv, g, lnw, lnb, ones, proj, proj, x, gt, gain, wa, wr, wo)


def _ffn_kernel(x_ref, xp_ref, xn_ref, gpre_ref, sc_ref, sh_ref, wg_ref, wu_ref, cw_ref, cb_ref, wd_ref,
                gt_ref, gpost_ref, o_ref, h_ref, *, tm, seq):
    i = pl.program_id(0)
    j = pl.program_id(1)
    halo = SUBLANES
    bpr = seq // tm

    @pl.when(j == 0)
    def _():
        gain = gpre_ref[...]
        sc = 1.0 + sc_ref[0]
        sh = sh_ref[0]
        has_prev = (i % bpr != 0).astype(F32)
        has_next = (i % bpr != bpr - 1).astype(F32)
        h_ref[0:halo, :] = ((_rms(xp_ref[...], gain) * sc + sh) * has_prev).astype(BF16)
        h_ref[halo:halo + tm, :] = (_rms(x_ref[...], gain) * sc + sh).astype(BF16)
        h_ref[halo + tm:, :] = ((_rms(xn_ref[...], gain) * sc + sh) * has_next).astype(BF16)
        o_ref[...] = jnp.zeros_like(o_ref)

    ext = tm + 2 * halo
    gate = jnp.dot(h_ref[...], wg_ref[...], preferred_element_type=F32)
    prev = pltpu.roll(gate, 1, 0)[halo:halo + tm]
    nxt = pltpu.roll(gate, ext - 1, 0)[halo:halo + tm]
    u = cw_ref[0:1, :] * prev + cw_ref[1:2, :] * gate[halo:halo + tm] + cw_ref[2:3, :] * nxt + cb_ref[...]
    up = jnp.dot(h_ref[halo:halo + tm, :], wu_ref[...], preferred_element_type=F32)
    act = jax.nn.gelu(u, approximate=True) * up
    o_ref[...] += jnp.dot(act.astype(BF16), wd_ref[...], preferred_element_type=F32)

    @pl.when(j == pl.num_programs(1) - 1)
    def _():
        o_ref[...] = x_ref[...] + gt_ref[0] * _rms(o_ref[...], gpost_ref[...])


def _ffn(x, gpre, sc, sh, wg, wu, cw, cb, wd, gt, gpost, seq, tm=1024, tf=FF_TILE):
    m, d = x.shape
    f = wg.shape[1]
    bpr = seq // tm
    hb = tm // SUBLANES
    nhb = m // SUBLANES
    const = lambda i, j: (0, 0)
    bidx = lambda i, j: (i // bpr, 0, 0)
    return pl.pallas_call(
        functools.partial(_ffn_kernel, tm=tm, seq=seq),
        name="ffn",
        grid=(m // tm, f // tf),
        in_specs=[pl.BlockSpec((tm, d), lambda i, j: (i, 0)),
                  pl.BlockSpec((SUBLANES, d), lambda i, j: (jnp.maximum(i * hb - 1, 0), 0)),
                  pl.BlockSpec((SUBLANES, d), lambda i, j: (jnp.minimum((i + 1) * hb, nhb - 1), 0)),
                  pl.BlockSpec((1, d), const),
                  pl.BlockSpec((1, 1, d), bidx),
                  pl.BlockSpec((1, 1, d), bidx),
                  pl.BlockSpec((d, tf), lambda i, j: (0, j)),
                  pl.BlockSpec((d, tf), lambda i, j: (0, j)),
                  pl.BlockSpec((SUBLANES, tf), lambda i, j: (0, j)),
                  pl.BlockSpec((1, tf), lambda i, j: (0, j)),
                  pl.BlockSpec((tf, d), lambda i, j: (j, 0)),
                  pl.BlockSpec((1, 1, d), bidx),
                  pl.BlockSpec((1, d), const)],
        out_specs=pl.BlockSpec((tm, d), lambda i, j: (i, 0)),
        out_shape=jax.ShapeDtypeStruct((m, d), F32),
        scratch_shapes=[pltpu.VMEM((tm + 2 * SUBLANES, d), BF16)],
        compiler_params=_cparams(("parallel", "arbitrary")),
    )(x, x, x, gpre, sc, sh, wg, wu, cw, cb, wd, gt, gpost)


def _rope_tables(seq):
    half = ATT_HEAD_DIM // 2
    inv_freq = 1.0 / (ROPE_THETA ** (jnp.arange(half, dtype=F32) / half))
    ang = jnp.arange(seq, dtype=F32)[:, None] * inv_freq[None, :]
    cos = jnp.cos(ang)
    sin = jnp.sin(ang)
    return jnp.concatenate([cos, cos], axis=-1), jnp.concatenate([-sin, sin], axis=-1)


def _trunk(x, ada, p):
    bsz, seq, d = x.shape
    m = bsz * seq
    sh1, sc1, gt1, sh2, sc2, gt2 = [t[:, None, :] for t in jnp.split(ada, 6, axis=-1)]
    x2 = x.reshape(m, d)
    proj = _inproj(x2, p['ln_mix_pre'], sc1, sh1, p['w_in'], seq)
    proj3 = proj.reshape(bsz, seq, N_IN_PAD)
    cos, sin = _rope_tables(seq)
    att = _attention(proj3, cos, sin)
    r, v, kk, g, bv, lw, kd, b = _prep(proj, p['shift_mu'], p['decay_base'], p['decay_up'], p['iclr_base'],
                                       p['iclr_up'], p['gate_up'], p['k_k'], p['k_a'], p['r_k'], p['head_ones'], seq)
    per_seq = lambda t: t.reshape(t.shape[:-2] + (bsz, seq, RW_WIDTH))
    yf, yb = _rwkv_scan(per_seq(r), per_seq(v), per_seq(kk), per_seq(lw), per_seq(kd), per_seq(b))
    x1 = _mix(att.reshape(m, ATT_OUT), yf.reshape(m, RW_WIDTH), yb.reshape(m, RW_WIDTH), bv, g,
              p['lnx_w'], p['lnx_b'], p['head_ones'],
              proj, x2, gt1, p['ln_mix_post'], p['w_att_branch'], p['w_rwkv_branch'], p['w_out'], seq)
    y = _ffn(x1, p['ln_ffn_pre'], sc2, sh2, p['w_ffn_gate'], p['w_ffn_up'], p['ffn_conv_w'], p['ffn_conv_b'],
             p['w_ffn_down'], gt2, p['ln_ffn_post'], seq)
    return y.reshape(bsz, seq, d)


def kernel(x_prompt, x_sample, c_prompt, c_sample, ln_mix_pre, ln_mix_post, ln_ffn_pre, ln_ffn_post,
           w_ada, b_ada, w_in, shift_mu, decay_base, decay_up, iclr_base, iclr_up, gate_up,
           k_k, k_a, r_k, lnx_w, lnx_b, w_att_branch, w_rwkv_branch, w_out,
           w_ffn_gate, w_ffn_up, ffn_conv_w, ffn_conv_b, w_ffn_down):
    depth = w_in.shape[0]
    nb_p = c_prompt.shape[0]
    nb_s = c_sample.shape[0]
    c_all = jnp.concatenate([c_prompt, c_sample], axis=0)
    c_all = jnp.pad(c_all, ((0, (-c_all.shape[0]) % SUBLANES), (0, 0)))
    fpad = D_FF_PAD - D_FF
    xp, xs = x_prompt, x_sample
    lane_head = jnp.arange(MXU_DIM_V7X) // RW_HEAD
    head_ones = (lane_head[:, None] == lane_head[None, :]).astype(BF16)
    zdl = jnp.zeros((DECAY_LORA, RW_WIDTH), F32)
    zil = jnp.zeros((ICLR_LORA, RW_WIDTH), F32)
    for l in range(depth):
        att_end = 3 * ATT_WIDTH
        rw_end = att_end + RW_COLS
        w_in_b = w_in[l].astype(BF16)
        rw_cols = jnp.pad(w_in_b[:, att_end:rw_end], ((0, 0), (0, LAT_PAD - LAT_COLS)))
        w_in_l = jnp.concatenate([w_in_b[:, rw_end:], rw_cols, w_in_b[:, :att_end]], axis=1)
        dup = jnp.block([[decay_up[l][0], zdl], [zdl, decay_up[l][1]]])
        dup_hi = dup.astype(BF16)
        p = dict(
            ln_mix_pre=ln_mix_pre[l][None], ln_mix_post=ln_mix_post[l][None],
            ln_ffn_pre=ln_ffn_pre[l][None], ln_ffn_post=ln_ffn_post[l][None],
            w_in=w_in_l, head_ones=head_ones,
            shift_mu=jnp.pad(shift_mu[l], ((0, 0), (0, RW_COLS_PAD - RW_COLS))),
            decay_base=decay_base[l].reshape(1, 2 * RW_WIDTH),
            decay_up=jnp.stack([dup_hi, (dup - dup_hi.astype(F32)).astype(BF16)]),
            iclr_base=iclr_base[l].reshape(1, 2 * RW_WIDTH),
            iclr_up=jnp.block([[iclr_up[l][0], zil], [zil, iclr_up[l][1]]]).astype(BF16),
            gate_up=jnp.pad(gate_up[l], ((0, LAT_PAD - LAT_COLS), (0, 0))).astype(BF16),
            k_k=k_k[l][None], k_a=k_a[l][None], r_k=r_k[l].reshape(1, RW_WIDTH),
            lnx_w=lnx_w[l][None], lnx_b=lnx_b[l][None],
            w_att_branch=w_att_branch[l].astype(BF16), w_rwkv_branch=w_rwkv_branch[l].astype(BF16),
            w_out=w_out[l].astype(BF16),
            w_ffn_gate=jnp.concatenate([w_ffn_gate[l].astype(BF16), jnp.zeros((D_MODEL, fpad), BF16)], axis=1),
            w_ffn_up=jnp.concatenate([w_ffn_up[l].astype(BF16), jnp.zeros((D_MODEL, fpad), BF16)], axis=1),
            ffn_conv_w=jnp.pad(ffn_conv_w[l], ((0, SUBLANES - 3), (0, fpad))),
            ffn_conv_b=jnp.pad(ffn_conv_b[l], ((0, fpad),))[None],
            w_ffn_down=jnp.concatenate([w_ffn_down[l].astype(BF16), jnp.zeros((fpad, D_MODEL), BF16)], axis=0),
        )
        ada = _ada(c_all, w_ada[l], b_ada[l][None])
        xp = _trunk(xp, ada[:nb_p], p)
        xs = _trunk(xs, ada[nb_p:nb_p + nb_s], p)
    return (xp, xs)
```

```python
import functools
import math

import jax
import jax.numpy as jnp
from jax import lax
from jax.experimental import pallas as pl
from jax.experimental.pallas import tpu as pltpu

F32 = jnp.float32
BF16 = jnp.bfloat16

D_MODEL = 2048
ATT_GROUPS = ((128, 1), (512, 4), (2048, 16))
ATT_HEADS_PER_GROUP = 4
ATT_HEAD_DIM = 128
ATT_HEADS = ATT_HEADS_PER_GROUP * len(ATT_GROUPS)
ATT_WIDTH = ATT_HEADS * ATT_HEAD_DIM
ATT_OUT = ATT_HEADS_PER_GROUP * ATT_HEAD_DIM
ROPE_THETA = 10000.0
RW_HEAD = 64
RW_WIDTH = D_MODEL // 2
RW_HEADS = RW_WIDTH // RW_HEAD
DECAY_LORA = 64
ICLR_LORA = 64
GATE_LORA = 160
RW_COLS = 3 * RW_WIDTH + 2 * DECAY_LORA + 2 * ICLR_LORA + GATE_LORA
N_IN = 3 * ATT_WIDTH + RW_COLS + 2 * D_MODEL
D_FF = ((8 * D_MODEL // 3 + 127) // 128) * 128
RMS_EPS = 1e-6
GN_EPS = 64e-5
NEG_INF = -1e30
DECAY_SCALE = math.exp(-0.5)

LANES = 128
SUBLANES = 8
BF16_ROWS = 16
MXU_DIM_V7X = 256
VMEM_BYTES_V7X = 64 * 1024 * 1024
VMEM_COMPILER_RESERVE = 4 * 1024 * 1024
VMEM_LIMIT = VMEM_BYTES_V7X - VMEM_COMPILER_RESERVE

GATE_OFF = 0
RW_OFF = GATE_OFF + 2 * D_MODEL
LAT_OFF = RW_OFF + 3 * RW_WIDTH
LAT_COLS = RW_COLS - 3 * RW_WIDTH
LAT_PAD = 512
ATT_OFF = LAT_OFF + LAT_PAD
N_IN_PAD = ATT_OFF + 3 * ATT_WIDTH
RW_COLS_PAD = 3 * RW_WIDTH + LAT_PAD
FF_TILE = 512
D_FF_PAD = -(-D_FF // FF_TILE) * FF_TILE
RW_CHUNK = 64
RW_ROWS_PER_STEP = 2
ATT_SPAN = 64
ATT_QBLOCK = 128


def _cparams(sem):
    return pltpu.CompilerParams(dimension_semantics=sem, vmem_limit_bytes=VMEM_LIMIT)


def _rms(x, gain):
    return x * lax.rsqrt(jnp.mean(x * x, axis=-1, keepdims=True) + RMS_EPS) * gain


def _row_chunks(first, last, fn, unroll=4):
    if unroll is None:
        for c in range(first, last):
            fn(c * BF16_ROWS)
        return

    def body(c, carry):
        fn(pl.multiple_of(c * BF16_ROWS, BF16_ROWS))
        return carry

    lax.fori_loop(first, last, body, 0, unroll=min(unroll, last - first))


def _ada_kernel(c_ref, w_ref, b_ref, o_ref):
    c = c_ref[...]
    s = c * jax.nn.sigmoid(c)
    o_ref[...] = jnp.dot(s.astype(BF16), w_ref[...].astype(BF16),
                         preferred_element_type=F32) + b_ref[...]


def _ada(c, w, b, tn=1024):
    m, k = c.shape
    n = w.shape[1]
    return pl.pallas_call(
        _ada_kernel,
        name="ada",
        grid=(n // tn,),
        in_specs=[pl.BlockSpec((m, k), lambda j: (0, 0)),
                  pl.BlockSpec((k, tn), lambda j: (0, j)),
                  pl.BlockSpec((1, tn), lambda j: (0, j))],
        out_specs=pl.BlockSpec((m, tn), lambda j: (0, j)),
        out_shape=jax.ShapeDtypeStruct((m, n), F32),
        compiler_params=_cparams(("parallel",)),
    )(c, w, b)


def _inproj_kernel(x_ref, g_ref, sc_ref, sh_ref, w_ref, o_ref, h_ref, *, part, stride):
    i = pl.program_id(0)
    j = pl.program_id(1)

    def norm_rows(slot, start, nrows, unroll):
        def chunk(r0):
            rows = pl.ds(pl.multiple_of(start + r0, BF16_ROWS), BF16_ROWS)
            h = _rms(x_ref[rows, :], g_ref[...]) * (1.0 + sc_ref[0]) + sh_ref[0]
            h_ref[slot, rows, :] = h.astype(BF16)

        _row_chunks(0, nrows // BF16_ROWS, chunk, unroll)

    @pl.when((i == 0) & (j == 0))
    def _():
        norm_rows(0, 0, x_ref.shape[0], 4)

    o_ref[...] = jnp.dot(h_ref[i % 2], w_ref[...], preferred_element_type=F32)

    norm_rows((i + 1) % 2, pl.multiple_of(jnp.maximum(j - 1, 0) * stride, BF16_ROWS), part, None)


def _inproj(x, gain, sc, sh, w, seq, tm=1024, tn=1536):
    m, k = x.shape
    n = w.shape[1]
    bpr = seq // tm
    ni = m // tm
    nsl = n // tn - 1
    part = -(-tm // (nsl * BF16_ROWS)) * BF16_ROWS
    stride = (tm - part) // (nsl - 1)
    assert stride * (nsl - 1) + part == tm and stride % BF16_ROWS == 0 and stride <= part
    ahead = lambda i, j: jnp.minimum(i + jnp.minimum(j, 1), ni - 1)
    return pl.pallas_call(
        functools.partial(_inproj_kernel, part=part, stride=stride),
        name="inproj",
        grid=(ni, n // tn),
        in_specs=[pl.BlockSpec((tm, k), lambda i, j: (ahead(i, j), 0)),
                  pl.BlockSpec((1, k), lambda i, j: (0, 0)),
                  pl.BlockSpec((1, 1, k), lambda i, j: (ahead(i, j) // bpr, 0, 0)),
                  pl.BlockSpec((1, 1, k), lambda i, j: (ahead(i, j) // bpr, 0, 0)),
                  pl.BlockSpec((k, tn), lambda i, j: (0, j))],
        out_specs=pl.BlockSpec((tm, tn), lambda i, j: (i, j)),
        out_shape=jax.ShapeDtypeStruct((m, n), F32),
        scratch_shapes=[pltpu.VMEM((2, tm, k), BF16)],
        compiler_params=_cparams(("arbitrary", "arbitrary")),
    )(x, gain, sc, sh, w)


def _attn_kernel(q_ref, k_ref, v_ref, cos_ref, sin_ref, o_ref, qr_ref, kr_ref, og_ref, lse_ref, *, seq):
    gid = pl.program_id(2)
    rows = 512
    half = ATT_HEAD_DIM // 2

    def rot(i, _):
        sl = pl.ds(pl.multiple_of(i * rows, rows), rows)
        cos = cos_ref[sl, :]
        sin = sin_ref[sl, :]
        q = q_ref[0, sl, :]
        k = k_ref[0, sl, :]
        qr_ref[sl, :] = (q * cos + pltpu.roll(q, half, 1) * sin) * (ATT_HEAD_DIM ** -0.5)
        kr_ref[sl, :] = k * cos + pltpu.roll(k, half, 1) * sin
        return 0

    lax.fori_loop(0, seq // rows, rot, 0)

    def group(gi, dil):
        cls_len = seq // dil
        bq = min(ATT_QBLOCK, cls_len)
        if cls_len <= bq + 2 * ATT_SPAN:
            bq = cls_len
        bk = min(bq + 2 * ATT_SPAN, cls_len)
        nqb = cls_len // bq
        rel0 = (lax.broadcasted_iota(jnp.int32, (bq, bk), 0)
                - lax.broadcasted_iota(jnp.int32, (bq, bk), 1))
        ones = jnp.ones((bk, ATT_HEAD_DIM), BF16)

        def body(it, _):
            cls = it // nqb
            p0 = (it % nqb) * bq
            ks = jnp.clip(p0 - ATT_SPAN, 0, cls_len - bk)
            if dil == 1:
                rq = pl.ds(p0, bq)
                rk = pl.ds(ks, bk)
            else:
                rq = pl.ds(cls + dil * p0, bq, stride=dil)
                rk = pl.ds(cls + dil * ks, bk, stride=dil)
            qb = qr_ref[rq, :].astype(BF16)
            kb = kr_ref[rk, :].astype(BF16)
            vb = v_ref[0, rk, :].astype(BF16)
            s = lax.dot_general(qb, kb, (((1,), (1,)), ((), ())), preferred_element_type=F32)
            band = jnp.abs(rel0 + (p0 - ks)) <= ATT_SPAN
            s = jnp.where(band, s, NEG_INF)
            m = jnp.max(s, axis=-1, keepdims=True)
            p = jnp.exp(s - m)
            pv = jnp.dot(p.astype(BF16), jnp.concatenate([vb, ones], axis=1), preferred_element_type=F32)
            l = pv[:, ATT_HEAD_DIM:]
            og_ref[gi, rq, :] = pv[:, :ATT_HEAD_DIM] / l
            lse_ref[gi, rq, :] = m + jnp.log(l)
            return 0

        lax.fori_loop(0, dil * nqb, body, 0, unroll=min(8, dil * nqb))

    for gi, (_, dil) in enumerate(ATT_GROUPS):
        @pl.when(gid == gi)
        def _(gi=gi, dil=dil):
            group(gi, dil)

    @pl.when(gid == len(ATT_GROUPS) - 1)
    def _():
        def comb(i, _):
            sl = pl.ds(pl.multiple_of(i * rows, rows), rows)
            l0 = lse_ref[0, sl, :]
            l1 = lse_ref[1, sl, :]
            l2 = lse_ref[2, sl, :]
            mx = jnp.maximum(jnp.maximum(l0, l1), l2)
            w0 = jnp.exp(l0 - mx)
            w1 = jnp.exp(l1 - mx)
            w2 = jnp.exp(l2 - mx)
            num = w0 * og_ref[0, sl, :] + w1 * og_ref[1, sl, :] + w2 * og_ref[2, sl, :]
            o_ref[0, sl, :] = (num / (w0 + w1 + w2)).astype(o_ref.dtype)
            return 0

        lax.fori_loop(0, seq // rows, comb, 0)


def _attention(proj3, cos, sin):
    bsz, seq, _ = proj3.shape
    hpg = ATT_HEADS_PER_GROUP
    ng = len(ATT_GROUPS)
    blk = (1, seq, ATT_HEAD_DIM)
    q0 = ATT_OFF // ATT_HEAD_DIM
    return pl.pallas_call(
        functools.partial(_attn_kernel, seq=seq),
        name="attn",
        grid=(bsz, hpg, ng),
        in_specs=[pl.BlockSpec(blk, lambda b, s, g: (b, 0, q0 + g * hpg + s)),
                  pl.BlockSpec(blk, lambda b, s, g: (b, 0, q0 + ATT_HEADS + g * hpg + s)),
                  pl.BlockSpec(blk, lambda b, s, g: (b, 0, q0 + 2 * ATT_HEADS + g * hpg + s)),
                  pl.BlockSpec((seq, ATT_HEAD_DIM), lambda b, s, g: (0, 0)),
                  pl.BlockSpec((seq, ATT_HEAD_DIM), lambda b, s, g: (0, 0))],
        out_specs=pl.BlockSpec(blk, lambda b, s, g: (b, 0, s)),
        out_shape=jax.ShapeDtypeStruct((bsz, seq, ATT_OUT), BF16),
        scratch_shapes=[pltpu.VMEM((seq, ATT_HEAD_DIM), F32),
                        pltpu.VMEM((seq, ATT_HEAD_DIM), F32),
                        pltpu.VMEM((ng, seq, ATT_HEAD_DIM), F32),
                        pltpu.VMEM((ng, seq, ATT_HEAD_DIM), F32)],
        compiler_params=_cparams(("parallel", "parallel", "arbitrary")),
    )(proj3, proj3, proj3, cos, sin)


def _rwkv_kernel(rf_ref, vf_ref, kkf_ref, lwf_ref, kdf_ref, bf_ref, rb_ref, vb_ref, kkb_ref, lwb_ref, kdb_ref,
                 bb_ref, yf_ref, yb_ref, s_ref):
    cn = RW_CHUNK
    c2 = 2 * cn

    @pl.when(pl.program_id(1) == 0)
    def _():
        s_ref[...] = jnp.zeros_like(s_ref)

    ti = lax.broadcasted_iota(jnp.int32, (cn, cn), 0)
    si = lax.broadcasted_iota(jnp.int32, (cn, cn), 1)
    t2 = lax.broadcasted_iota(jnp.int32, (c2, c2), 0)
    s2 = lax.broadcasted_iota(jnp.int32, (c2, c2), 1)
    eye = (t2 == s2).astype(F32)
    same_blk = (t2 // cn) == (s2 // cn)
    top = t2 < cn
    lane = lax.broadcasted_iota(jnp.int32, (1, LANES), 1)
    first = lane < RW_HEAD
    same_head = ((lax.broadcasted_iota(jnp.int32, (LANES, LANES), 0) // RW_HEAD)
                 == (lax.broadcasted_iota(jnp.int32, (LANES, LANES), 1) // RW_HEAD))

    def stack(x):
        return jnp.concatenate([jnp.where(first, x, 0.0), jnp.where(first, 0.0, x)], axis=0)

    def mm(a, b):
        return jnp.dot(a.astype(BF16), b.astype(BF16), preferred_element_type=F32)

    def mm_nt(a, b):
        return lax.dot_general(a.astype(BF16), b.astype(BF16), (((1,), (1,)), ((), ())),
                               preferred_element_type=F32)

    lanes = [slice(hp * LANES, (hp + 1) * LANES) for hp in range(RW_WIDTH // LANES)]
    inst = []
    xq, vs, bk, decay_end, amat, strict, incl = [], [], [], [], [], [], []
    for z, (r_ref, v_ref, kk_ref, lw_ref, kd_ref, b_ref, y_ref) in enumerate(
            ((rf_ref, vf_ref, kkf_ref, lwf_ref, kdf_ref, bf_ref, yf_ref),
             (rb_ref, vb_ref, kkb_ref, lwb_ref, kdb_ref, bb_ref, yb_ref))):
        rev = z == 1
        tri = ((si >= ti) if rev else (si <= ti)).astype(BF16)
        order = (t2 % cn - s2 % cn) if rev else (s2 % cn - t2 % cn)
        strict_z = (order < 0) & same_blk
        incl_z = (order <= 0) & same_blk
        for row in range(r_ref.shape[0]):
            lw_all = lw_ref[0, row]
            lw_hi = lw_all.astype(BF16)
            lw_lo = (lw_all - lw_hi.astype(F32)).astype(BF16)
            g_all = (jnp.dot(tri, lw_hi, preferred_element_type=F32)
                     + jnp.dot(tri, lw_lo, preferred_element_type=F32))
            for hp, ls in enumerate(lanes):
                r = r_ref[row, :, ls]
                v = v_ref[row, :, ls]
                kk = kk_ref[row, :, ls]
                kd = kd_ref[0, row, :, ls]
                b = b_ref[0, row, :, ls]
                g = g_all[:, ls]
                g_end = g[0:1] if rev else g[cn - 1:cn]
                e_neg = jnp.exp(-g)
                e_end = jnp.exp(g_end - g)
                x = jnp.concatenate([stack(kk * jnp.exp(g - lw_all[:, ls])), stack(r * jnp.exp(g))],
                                    axis=0).astype(BF16)
                y = jnp.concatenate([b * e_neg, kd * e_neg], axis=0).astype(BF16)
                inst.append((z, row, hp, ls, v_ref, y_ref))
                strict.append(strict_z)
                incl.append(incl_z)
                xq.append(x)
                vs.append(stack(v).astype(BF16))
                bk.append(jnp.concatenate([b * e_end, kd * e_end], axis=0).astype(BF16))
                decay_end.append(jnp.exp(g_end))
                amat.append(lax.dot_general(x, y, (((1,), (1,)), ((), ())), preferred_element_type=F32))
    swapped = [pltpu.roll(am, cn, 1) for am in amat]
    tmat = [jnp.where(st, jnp.where(top, am[:c2], sw[:c2]), 0.0) for am, sw, st in zip(amat, swapped, strict)]
    rest = [jnp.concatenate([jnp.where(st, jnp.where(top, sw[:c2], am[:c2]), 0.0),
                             jnp.where(ic, jnp.where(top, sw[c2:], am[c2:]), 0.0)], axis=0).astype(BF16)
            for am, sw, st, ic in zip(amat, swapped, strict, incl)]
    arb = [jnp.where(ic, jnp.where(top, am[c2:], sw[c2:]), 0.0).astype(BF16)
           for am, sw, ic in zip(amat, swapped, incl)]
    ninv = [eye - t for t in tmat]
    pw = [t.astype(BF16) for t in tmat]
    pw = [jnp.dot(p, p, preferred_element_type=F32).astype(BF16) for p in pw]
    nsq = cn.bit_length() - 2
    for it in range(nsq):
        if it < nsq - 1:
            both = [jnp.dot(p, jnp.concatenate([p, n.astype(BF16)], axis=1), preferred_element_type=F32)
                    for p, n in zip(pw, ninv)]
            pw = [bo[:, :c2].astype(BF16) for bo in both]
            ninv = [n + bo[:, c2:] for n, bo in zip(ninv, both)]
        else:
            ninv = [n + jnp.dot(p, n.astype(BF16), preferred_element_type=F32) for p, n in zip(pw, ninv)]
    mv = [jnp.dot(m, v, preferred_element_type=F32) for m, v in zip(rest, vs)]
    s0 = [s_ref[z, row, hp] for z, row, hp, _, _, _ in inst]
    xs = [mm_nt(x, s) for x, s in zip(xq, s0)]
    us = [mm(n, -x[:c2] - m[:c2]) for n, x, m in zip(ninv, xs, mv)]
    ys = [x[c2:] + m[c2:] + mm(ab, u) for x, m, ab, u in zip(xs, mv, arb, us)]
    for n, (z, row, hp, ls, v_ref, y_ref) in enumerate(inst):
        y_ref[row, :, ls] = ys[n][:cn] + ys[n][cn:]
        uv = jnp.concatenate([us[n][:cn] + us[n][cn:], v_ref[row, :, ls].astype(F32)], axis=0)
        upd = mm(uv.T, bk[n])
        s_ref[z, row, hp] = jnp.where(same_head, s0[n] * decay_end[n] + upd, 0.0)


def _rwkv_scan(r, v, kk, lw, kd, b):
    bsz, seq, width = r.shape
    cn = RW_CHUNK
    nc = seq // cn
    rows = RW_ROWS_PER_STEP
    assert bsz % rows == 0
    fwd = pl.BlockSpec((rows, cn, width), lambda b, c: (b, c, 0))
    bwd = pl.BlockSpec((rows, cn, width), lambda b, c: (b, nc - 1 - c, 0))
    fwd_d = pl.BlockSpec((1, rows, cn, width), lambda b, c: (0, b, c, 0))
    bwd_d = pl.BlockSpec((1, rows, cn, width), lambda b, c: (1, b, nc - 1 - c, 0))
    out = jax.ShapeDtypeStruct((bsz, seq, width), F32)
    return pl.pallas_call(
        _rwkv_kernel,
        name="rwkv",
        grid=(bsz // rows, nc),
        in_specs=[fwd, fwd, fwd, fwd_d, fwd_d, fwd_d, bwd, bwd, bwd, bwd_d, bwd_d, bwd_d],
        out_specs=[fwd, bwd],
        out_shape=[out, out],
        scratch_shapes=[pltpu.VMEM((2, rows, width // LANES, LANES, LANES), F32)],
        compiler_params=_cparams(("parallel", "arbitrary")),
    )(r, v, kk, lw, kd, b, r, v, kk, lw, kd, b)


def _head_sum(x, ones, passes):
    gw = ones.shape[0]
    cols = []
    for c0 in range(0, x.shape[1], gw):
        xc = x[:, c0:c0 + gw]
        hi = xc.astype(BF16)
        out = jnp.dot(hi, ones, preferred_element_type=F32)
        if passes == 2:
            lo = (xc - hi.astype(F32)).astype(BF16)
            out = out + jnp.dot(lo, ones, preferred_element_type=F32)
        cols.append(out)
    return jnp.concatenate(cols, axis=1)


def _prep_kernel(r_ref, k_ref, v_ref, l_ref, rp_ref, kp_ref, vp_ref, lp_ref, rn_ref, kn_ref, vn_ref, ln_ref,
                 mu_ref, dbase_ref, dup_ref, ibase_ref, iup_ref, gup_ref, kkw_ref, ka_ref, rk_ref, ones_ref,
                 ro_ref, vo_ref, kko_ref, go_ref, bvo_ref, lwo_ref, kdo_ref, bo_ref, *, tm, seq):
    i = pl.program_id(0)
    bpr = seq // tm
    has_prev = (i % bpr != 0).astype(F32)
    has_next = (i % bpr != bpr - 1).astype(F32)
    rowi = lax.broadcasted_iota(jnp.int32, (tm, 1), 0)
    cw = RW_WIDTH

    def shifted(main_ref, prev_ref, next_ref, c0, c1):
        z = main_ref[...]
        zp = jnp.where(rowi == 0, prev_ref[SUBLANES - 1:SUBLANES, :] * has_prev, pltpu.roll(z, 1, 0))
        zn = jnp.where(rowi == tm - 1, next_ref[0:1, :] * has_next, pltpu.roll(z, tm - 1, 0))
        return z + mu_ref[0:1, c0:c1] * (zp - z) + mu_ref[1:2, c0:c1] * (zn - z)

    r = shifted(r_ref, rp_ref, rn_ref, 0, cw)
    k = shifted(k_ref, kp_ref, kn_ref, cw, 2 * cw)
    v = shifted(v_ref, vp_ref, vn_ref, 2 * cw, 3 * cw)
    lat = shifted(l_ref, lp_ref, ln_ref, 3 * cw, 3 * cw + LAT_PAD)
    nd, ni = 2 * DECAY_LORA, 2 * ICLR_LORA
    wl = jnp.tanh(lat[:, :nd])
    wl_hi = wl.astype(BF16)
    wl_lo = (wl - wl_hi.astype(F32)).astype(BF16)
    w_raw = (dbase_ref[...] + jnp.dot(wl_hi, dup_ref[0], preferred_element_type=F32)
             + jnp.dot(wl_lo, dup_ref[0], preferred_element_type=F32)
             + jnp.dot(wl_hi, dup_ref[1], preferred_element_type=F32))
    lw = jax.nn.sigmoid(w_raw) * (-DECAY_SCALE)
    a = jax.nn.sigmoid(ibase_ref[...] + jnp.dot(lat[:, nd:nd + ni].astype(BF16), iup_ref[...],
                                                preferred_element_type=F32))
    g = jnp.dot(jax.nn.sigmoid(lat[:, nd + ni:]).astype(BF16), gup_ref[...], preferred_element_type=F32)
    ones = ones_ref[...]
    kkr = k * kkw_ref[...]
    kk = kkr / jnp.maximum(jnp.sqrt(_head_sum(kkr * kkr, ones, 2)), 1e-12)
    ka = ka_ref[...]
    k_fix = k * (1.0 - ka)
    k_var = k * ka
    kd0 = k_fix + k_var * a[:, :cw]
    kd1 = k_fix + k_var * a[:, cw:]
    bonus = _head_sum(r * (kd0 + kd1) * rk_ref[...], ones, 2)
    ro_ref[...] = r
    vo_ref[...] = v.astype(vo_ref.dtype)
    kko_ref[...] = kk
    go_ref[...] = g.astype(go_ref.dtype)
    bvo_ref[...] = (bonus * v).astype(bvo_ref.dtype)
    lwo_ref[0] = lw[:, :cw]
    lwo_ref[1] = lw[:, cw:]
    kdo_ref[0] = kd0
    kdo_ref[1] = kd1
    bo_ref[0] = kk * a[:, :cw]
    bo_ref[1] = kk * a[:, cw:]


def _prep(proj, mu, dbase, dup, ibase, iup, gup, kkw, ka, rk, ones, seq, tm=256):
    m = proj.shape[0]
    cw = RW_WIDTH
    bpr = seq // tm
    hb = tm // SUBLANES
    nhb = m // SUBLANES
    r0 = RW_OFF // cw
    l0 = LAT_OFF // LAT_PAD
    cols = [(cw, r0), (cw, r0 + 1), (cw, r0 + 2), (LAT_PAD, l0)]
    main = [pl.BlockSpec((tm, w), lambda i, c=c: (i, c)) for w, c in cols]
    prev = [pl.BlockSpec((SUBLANES, w), lambda i, c=c: (jnp.maximum(i * hb - 1, 0), c)) for w, c in cols]
    nxt = [pl.BlockSpec((SUBLANES, w), lambda i, c=c: (jnp.minimum((i + 1) * hb, nhb - 1), c)) for w, c in cols]
    params = [mu, dbase, dup, ibase, iup, gup, kkw, ka, rk, ones]
    pspecs = [pl.BlockSpec(t.shape, lambda i, n=t.ndim: (0,) * n) for t in params]
    oblk = pl.BlockSpec((tm, cw), lambda i: (i, 0))
    dblk = pl.BlockSpec((2, tm, cw), lambda i: (0, i, 0))
    one = jax.ShapeDtypeStruct((m, cw), F32)
    half = jax.ShapeDtypeStruct((m, cw), BF16)
    two = jax.ShapeDtypeStruct((2, m, cw), F32)
    return pl.pallas_call(
        functools.partial(_prep_kernel, tm=tm, seq=seq),
        name="rwkv_prep",
        grid=(m // tm,),
        in_specs=main + prev + nxt + pspecs,
        out_specs=[oblk] * 5 + [dblk] * 3,
        out_shape=[one, half, one, half, half] + [two] * 3,
        compiler_params=_cparams(("parallel",)),
    )(*([proj] * 12), *params)


def _mix_kernel(att_ref, yf_ref, yb_ref, bv_ref, g_ref, lnw_ref, lnb_ref, ones_ref, ga_ref, gr_ref, x_ref, gt_ref,
                gain_ref, wa_ref, wr_ref, wo_ref, o_ref):
    ones = ones_ref[...]
    y = yf_ref[...] + yb_ref[...]
    mu = _head_sum(y, ones, 1) * (1.0 / RW_HEAD)
    yc = y - mu
    var = _head_sum(yc * yc, ones, 1) * (1.0 / RW_HEAD)
    rw = ((yc * lax.rsqrt(var + GN_EPS)) * lnw_ref[...] + lnb_ref[...] + bv_ref[...]) * g_ref[...]
    pa = jnp.dot(att_ref[...], wa_ref[...], preferred_element_type=F32)
    pr = jnp.dot(rw.astype(BF16), wr_ref[...], preferred_element_type=F32)
    merged = jax.nn.sigmoid(ga_ref[...]) * pa + jax.nn.sigmoid(gr_ref[...]) * pr
    mix = jnp.dot(merged.astype(BF16), wo_ref[...], preferred_element_type=F32)
    o_ref[...] = x_ref[...] + gt_ref[0] * _rms(mix, gain_ref[...])


def _mix(att, yf, yb, bv, g, lnw, lnb, ones, proj, x, gt, gain, wa, wr, wo, seq, tm=256):
    m, d = x.shape
    cw = RW_WIDTH
    bpr = seq // tm
    gblk = GATE_OFF // d
    const = lambda i: (0, 0)
    resident = lambda t: pl.BlockSpec(t.shape, const, pipeline_mode=pl.Buffered(1))
    return pl.pallas_call(
        _mix_kernel,
        name="mix",
        grid=(m // tm,),
        in_specs=[pl.BlockSpec((tm, att.shape[1]), lambda i: (i, 0)),
                  pl.BlockSpec((tm, cw), lambda i: (i, 0)),
                  pl.BlockSpec((tm, cw), lambda i: (i, 0)),
                  pl.BlockSpec((tm, cw), lambda i: (i, 0)),
                  pl.BlockSpec((tm, cw), lambda i: (i, 0)),
                  pl.BlockSpec((1, cw), const),
                  pl.BlockSpec((1, cw), const),
                  resident(ones),
                  pl.BlockSpec((tm, d), lambda i: (i, gblk)),
                  pl.BlockSpec((tm, d), lambda i: (i, gblk + 1)),
                  pl.BlockSpec((tm, d), lambda i: (i, 0)),
                  pl.BlockSpec((1, 1, d), lambda i: (i // bpr, 0, 0)),
                  pl.BlockSpec((1, d), const),
                  resident(wa), resident(wr), resident(wo)],
        out_specs=pl.BlockSpec((tm, d), lambda i: (i, 0)),
        out_shape=jax.ShapeDtypeStruct((m, d), F32),
        compiler_params=_cparams(("parallel",)),
    )(att, yf, yb, bv, g, lnw, lnb, ones, proj, proj, x, gt, gain, wa, wr, wo)


def _ffn_kernel(x_ref, xp_ref, xn_ref, gpre_ref, sc_ref, sh_ref, wg_ref, wu_ref, cw_ref, cb_ref, wd_ref,
                gt_ref, gpost_ref, o_ref, h_ref, *, tm, seq):
    i = pl.program_id(0)
    j = pl.program_id(1)
    halo = SUBLANES
    bpr = seq // tm

    @pl.when(j == 0)
    def _():
        def modnorm(x):
            return _rms(x, gpre_ref[...]) * (1.0 + sc_ref[0]) + sh_ref[0]

        top = lax.broadcasted_iota(jnp.int32, (BF16_ROWS, 1), 0) < halo
        keep_first = jnp.where(top, (i % bpr != 0).astype(F32), 1.0)
        keep_last = jnp.where(top, 1.0, (i % bpr != bpr - 1).astype(F32))
        first = jnp.concatenate([xp_ref[...], x_ref[0:halo, :]], axis=0)
        last = jnp.concatenate([x_ref[tm - halo:tm, :], xn_ref[...]], axis=0)
        h_ref[0:BF16_ROWS, :] = (modnorm(first) * keep_first).astype(BF16)
        h_ref[tm:tm + BF16_ROWS, :] = (modnorm(last) * keep_last).astype(BF16)

        def middle(r0):
            h_ref[r0:r0 + BF16_ROWS, :] = modnorm(x_ref[r0 - halo:r0 - halo + BF16_ROWS, :]).astype(BF16)

        _row_chunks(1, tm // BF16_ROWS, middle, None)
        o_ref[...] = jnp.zeros_like(o_ref)

    ext = tm + 2 * halo
    gate = jnp.dot(h_ref[...], wg_ref[...], preferred_element_type=F32)
    prev = pltpu.roll(gate, 1, 0)[halo:halo + tm]
    nxt = pltpu.roll(gate, ext - 1, 0)[halo:halo + tm]
    u = cw_ref[0:1, :] * prev + cw_ref[1:2, :] * gate[halo:halo + tm] + cw_ref[2:3, :] * nxt + cb_ref[...]
    up = jnp.dot(h_ref[halo:halo + tm, :], wu_ref[...], preferred_element_type=F32)
    act = jax.nn.gelu(u, approximate=True) * up
    o_ref[...] += jnp.dot(act.astype(BF16), wd_ref[...], preferred_element_type=F32)

    @pl.when(j == pl.num_programs(1) - 1)
    def _():
        def finish(r0):
            rows = slice(r0, r0 + BF16_ROWS)
            o_ref[rows, :] = x_ref[rows, :] + gt_ref[0] * _rms(o_ref[rows, :], gpost_ref[...])

        _row_chunks(0, tm // BF16_ROWS, finish, None)


def _ffn(x, gpre, sc, sh, wg, wu, cw, cb, wd, gt, gpost, seq, tm=1024, tf=FF_TILE):
    m, d = x.shape
    f = wg.shape[1]
    bpr = seq // tm
    hb = tm // SUBLANES
    nhb = m // SUBLANES
    const = lambda i, j: (0, 0)
    bidx = lambda i, j: (i // bpr, 0, 0)
    return pl.pallas_call(
        functools.partial(_ffn_kernel, tm=tm, seq=seq),
        name="ffn",
        grid=(m // tm, f // tf),
        in_specs=[pl.BlockSpec((tm, d), lambda i, j: (i, 0)),
                  pl.BlockSpec((SUBLANES, d), lambda i, j: (jnp.maximum(i * hb - 1, 0), 0)),
                  pl.BlockSpec((SUBLANES, d), lambda i, j: (jnp.minimum((i + 1) * hb, nhb - 1), 0)),
                  pl.BlockSpec((1, d), const),
                  pl.BlockSpec((1, 1, d), bidx),
                  pl.BlockSpec((1, 1, d), bidx),
                  pl.BlockSpec((d, tf), lambda i, j: (0, j)),
                  pl.BlockSpec((d, tf), lambda i, j: (0, j)),
                  pl.BlockSpec((SUBLANES, tf), lambda i, j: (0, j)),
                  pl.BlockSpec((1, tf), lambda i, j: (0, j)),
                  pl.BlockSpec((tf, d), lambda i, j: (j, 0)),
                  pl.BlockSpec((1, 1, d), bidx),
                  pl.BlockSpec((1, d), const)],
        out_specs=pl.BlockSpec((tm, d), lambda i, j: (i, 0)),
        out_shape=jax.ShapeDtypeStruct((m, d), F32),
        scratch_shapes=[pltpu.VMEM((tm + 2 * SUBLANES, d), BF16)],
        compiler_params=_cparams(("parallel", "arbitrary")),
    )(x, x, x, gpre, sc, sh, wg, wu, cw, cb, wd, gt, gpost)


def _rope_tables(seq):
    half = ATT_HEAD_DIM // 2
    inv_freq = 1.0 / (ROPE_THETA ** (jnp.arange(half, dtype=F32) / half))
    ang = jnp.arange(seq, dtype=F32)[:, None] * inv_freq[None, :]
    cos = jnp.cos(ang)
    sin = jnp.sin(ang)
    return jnp.concatenate([cos, cos], axis=-1), jnp.concatenate([-sin, sin], axis=-1)


def _trunk(x, ada, p):
    bsz, seq, d = x.shape
    m = bsz * seq
    sh1, sc1, gt1, sh2, sc2, gt2 = [t[:, None, :] for t in jnp.split(ada, 6, axis=-1)]
    x2 = x.reshape(m, d)
    proj = _inproj(x2, p['ln_mix_pre'], sc1, sh1, p['w_in'], seq)
    proj3 = proj.reshape(bsz, seq, N_IN_PAD)
    cos, sin = _rope_tables(seq)
    att = _attention(proj3, cos, sin)
    r, v, kk, g, bv, lw, kd, b = _prep(proj, p['shift_mu'], p['decay_base'], p['decay_up'], p['iclr_base'],
                                       p['iclr_up'], p['gate_up'], p['k_k'], p['k_a'], p['r_k'], p['head_ones'], seq)
    per_seq = lambda t: t.reshape(t.shape[:-2] + (bsz, seq, RW_WIDTH))
    yf, yb = _rwkv_scan(per_seq(r), per_seq(v), per_seq(kk), per_seq(lw), per_seq(kd), per_seq(b))
    x1 = _mix(att.reshape(m, ATT_OUT), yf.reshape(m, RW_WIDTH), yb.reshape(m, RW_WIDTH), bv, g,
              p['lnx_w'], p['lnx_b'], p['head_ones'],
              proj, x2, gt1, p['ln_mix_post'], p['w_att_branch'], p['w_rwkv_branch'], p['w_out'], seq)
    y = _ffn(x1, p['ln_ffn_pre'], sc2, sh2, p['w_ffn_gate'], p['w_ffn_up'], p['ffn_conv_w'], p['ffn_conv_b'],
             p['w_ffn_down'], gt2, p['ln_ffn_post'], seq)
    return y.reshape(bsz, seq, d)


def kernel(x_prompt, x_sample, c_prompt, c_sample, ln_mix_pre, ln_mix_post, ln_ffn_pre, ln_ffn_post,
           w_ada, b_ada, w_in, shift_mu, decay_base, decay_up, iclr_base, iclr_up, gate_up,
           k_k, k_a, r_k, lnx_w, lnx_b, w_att_branch, w_rwkv_branch, w_out,
           w_ffn_gate, w_ffn_up, ffn_conv_w, ffn_conv_b, w_ffn_down):
    depth = w_in.shape[0]
    nb_p = c_prompt.shape[0]
    nb_s = c_sample.shape[0]
    c_all = jnp.concatenate([c_prompt, c_sample], axis=0)
    c_all = jnp.pad(c_all, ((0, (-c_all.shape[0]) % SUBLANES), (0, 0)))
    fpad = D_FF_PAD - D_FF
    xp, xs = x_prompt, x_sample
    lane_head = jnp.arange(MXU_DIM_V7X) // RW_HEAD
    head_ones = (lane_head[:, None] == lane_head[None, :]).astype(BF16)
    zdl = jnp.zeros((DECAY_LORA, RW_WIDTH), F32)
    zil = jnp.zeros((ICLR_LORA, RW_WIDTH), F32)
    for l in range(depth):
        att_end = 3 * ATT_WIDTH
        rw_end = att_end + RW_COLS
        w_in_b = w_in[l].astype(BF16)
        rw_cols = jnp.pad(w_in_b[:, att_end:rw_end], ((0, 0), (0, LAT_PAD - LAT_COLS)))
        w_in_l = jnp.concatenate([w_in_b[:, rw_end:], rw_cols, w_in_b[:, :att_end]], axis=1)
        dup = jnp.block([[decay_up[l][0], zdl], [zdl, decay_up[l][1]]])
        dup_hi = dup.astype(BF16)
        p = dict(
            ln_mix_pre=ln_mix_pre[l][None], ln_mix_post=ln_mix_post[l][None],
            ln_ffn_pre=ln_ffn_pre[l][None], ln_ffn_post=ln_ffn_post[l][None],
            w_in=w_in_l, head_ones=head_ones,
            shift_mu=jnp.pad(shift_mu[l], ((0, 0), (0, RW_COLS_PAD - RW_COLS))),
            decay_base=decay_base[l].reshape(1, 2 * RW_WIDTH),
            decay_up=jnp.stack([dup_hi, (dup - dup_hi.astype(F32)).astype(BF16)]),
            iclr_base=iclr_base[l].reshape(1, 2 * RW_WIDTH),
            iclr_up=jnp.block([[iclr_up[l][0], zil], [zil, iclr_up[l][1]]]).astype(BF16),
            gate_up=jnp.pad(gate_up[l], ((0, LAT_PAD - LAT_COLS), (0, 0))).astype(BF16),
            k_k=k_k[l][None], k_a=k_a[l][None], r_k=r_k[l].reshape(1, RW_WIDTH),
            lnx_w=lnx_w[l][None], lnx_b=lnx_b[l][None],
            w_att_branch=w_att_branch[l].astype(BF16), w_rwkv_branch=w_rwkv_branch[l].astype(BF16),
            w_out=w_out[l].astype(BF16),
            w_ffn_gate=jnp.concatenate([w_ffn_gate[l].astype(BF16), jnp.zeros((D_MODEL, fpad), BF16)], axis=1),
            w_ffn_up=jnp.concatenate([w_ffn_up[l].astype(BF16), jnp.zeros((D_MODEL, fpad), BF16)], axis=1),
            ffn_conv_w=jnp.pad(ffn_conv_w[l], ((0, SUBLANES - 3), (0, fpad))),
            ffn_conv_b=jnp.pad(ffn_conv_b[l], ((0, fpad),))[None],
            w_ffn_down=jnp.concatenate([w_ffn_down[l].astype(BF16), jnp.zeros((fpad, D_MODEL), BF16)], axis=0),
        )
        ada = _ada(c_all, w_ada[l], b_ada[l][None])
        xp = _trunk(xp, ada[:nb_p], p)
        xs = _trunk(xs, ada[nb_p:nb_p + nb_s], p)
    return (xp, xs)
```

```python
import functools
import math

import jax
import jax.numpy as jnp
from jax import lax
from jax.experimental import pallas as pl
from jax.experimental.pallas import tpu as pltpu

F32 = jnp.float32
BF16 = jnp.bfloat16

D_MODEL = 2048
ATT_GROUPS = ((128, 1), (512, 4), (2048, 16))
ATT_HEADS_PER_GROUP = 4
ATT_HEAD_DIM = 128
ATT_HEADS = ATT_HEADS_PER_GROUP * len(ATT_GROUPS)
ATT_WIDTH = ATT_HEADS * ATT_HEAD_DIM
ATT_OUT = ATT_HEADS_PER_GROUP * ATT_HEAD_DIM
ROPE_THETA = 10000.0
RW_HEAD = 64
RW_WIDTH = D_MODEL // 2
RW_HEADS = RW_WIDTH // RW_HEAD
DECAY_LORA = 64
ICLR_LORA = 64
GATE_LORA = 160
RW_COLS = 3 * RW_WIDTH + 2 * DECAY_LORA + 2 * ICLR_LORA + GATE_LORA
N_IN = 3 * ATT_WIDTH + RW_COLS + 2 * D_MODEL
D_FF = ((8 * D_MODEL // 3 + 127) // 128) * 128
RMS_EPS = 1e-6
GN_EPS = 64e-5
NEG_INF = -1e30
DECAY_SCALE = math.exp(-0.5)

LANES = 128
SUBLANES = 8
BF16_ROWS = 16
MXU_DIM_V7X = 256
VMEM_BYTES_V7X = 64 * 1024 * 1024
VMEM_COMPILER_RESERVE = 4 * 1024 * 1024
VMEM_LIMIT = VMEM_BYTES_V7X - VMEM_COMPILER_RESERVE

GATE_OFF = 0
RW_OFF = GATE_OFF + 2 * D_MODEL
LAT_OFF = RW_OFF + 3 * RW_WIDTH
LAT_COLS = RW_COLS - 3 * RW_WIDTH
LAT_PAD = 512
ATT_OFF = LAT_OFF + LAT_PAD
N_IN_PAD = ATT_OFF + 3 * ATT_WIDTH
RW_COLS_PAD = 3 * RW_WIDTH + LAT_PAD
FF_TILE = 512
D_FF_PAD = -(-D_FF // FF_TILE) * FF_TILE
RW_CHUNK = 64
RW_ROWS_PER_STEP = 2
ATT_SPAN = 64
ATT_QBLOCK = 128


def _cparams(sem):
    return pltpu.CompilerParams(dimension_semantics=sem, vmem_limit_bytes=VMEM_LIMIT)


def _rms(x, gain):
    return x * lax.rsqrt(jnp.mean(x * x, axis=-1, keepdims=True) + RMS_EPS) * gain


def _row_chunks(first, last, fn, unroll=4):
    if unroll is None:
        for c in range(first, last):
            fn(c * BF16_ROWS)
        return

    def body(c, carry):
        fn(pl.multiple_of(c * BF16_ROWS, BF16_ROWS))
        return carry

    lax.fori_loop(first, last, body, 0, unroll=min(unroll, last - first))


def _ada_kernel(c_ref, w_ref, b_ref, o_ref):
    c = c_ref[...]
    s = c * jax.nn.sigmoid(c)
    o_ref[...] = jnp.dot(s.astype(BF16), w_ref[...].astype(BF16),
                         preferred_element_type=F32) + b_ref[...]


def _ada(c, w, b, tn=1024):
    m, k = c.shape
    n = w.shape[1]
    return pl.pallas_call(
        _ada_kernel,
        name="ada",
        grid=(n // tn,),
        in_specs=[pl.BlockSpec((m, k), lambda j: (0, 0)),
                  pl.BlockSpec((k, tn), lambda j: (0, j)),
                  pl.BlockSpec((1, tn), lambda j: (0, j))],
        out_specs=pl.BlockSpec((m, tn), lambda j: (0, j)),
        out_shape=jax.ShapeDtypeStruct((m, n), F32),
        compiler_params=_cparams(("parallel",)),
    )(c, w, b)


def _inproj_kernel(x_ref, g_ref, sc_ref, sh_ref, w_ref, o_ref, h_ref, *, part):
    i = pl.program_id(0)
    j = pl.program_id(1)

    def norm_rows(slot, start, nrows, unroll):
        def chunk(r0):
            rows = pl.ds(pl.multiple_of(start + r0, BF16_ROWS), BF16_ROWS)
            h = _rms(x_ref[rows, :], g_ref[...]) * (1.0 + sc_ref[0]) + sh_ref[0]
            h_ref[slot, rows, :] = h.astype(BF16)

        _row_chunks(0, nrows // BF16_ROWS, chunk, unroll)

    @pl.when((i == 0) & (j == 0))
    def _():
        norm_rows(0, 0, x_ref.shape[0], 4)

    o_ref[...] = jnp.dot(h_ref[i % 2], w_ref[...], preferred_element_type=F32)

    start = jnp.minimum(jnp.maximum(j - 1, 0) * part, x_ref.shape[0] - part)
    norm_rows((i + 1) % 2, pl.multiple_of(start, BF16_ROWS), part, None)


def _inproj(x, gain, sc, sh, w, seq, tm=1024, tn=1536):
    m, k = x.shape
    n = w.shape[1]
    bpr = seq // tm
    ni = m // tm
    nsl = n // tn - 1
    part = -(-tm // (nsl * BF16_ROWS)) * BF16_ROWS
    ahead = lambda i, j: jnp.minimum(i + jnp.minimum(j, 1), ni - 1)
    return pl.pallas_call(
        functools.partial(_inproj_kernel, part=part),
        name="inproj",
        grid=(ni, n // tn),
        in_specs=[pl.BlockSpec((tm, k), lambda i, j: (ahead(i, j), 0)),
                  pl.BlockSpec((1, k), lambda i, j: (0, 0)),
                  pl.BlockSpec((1, 1, k), lambda i, j: (ahead(i, j) // bpr, 0, 0)),
                  pl.BlockSpec((1, 1, k), lambda i, j: (ahead(i, j) // bpr, 0, 0)),
                  pl.BlockSpec((k, tn), lambda i, j: (0, j))],
        out_specs=pl.BlockSpec((tm, tn), lambda i, j: (i, j)),
        out_shape=jax.ShapeDtypeStruct((m, n), F32),
        scratch_shapes=[pltpu.VMEM((2, tm, k), BF16)],
        compiler_params=_cparams(("arbitrary", "arbitrary")),
    )(x, gain, sc, sh, w)


def _attn_kernel(q_ref, k_ref, v_ref, cos_ref, sin_ref, o_ref, qr_ref, kr_ref, og_ref, lse_ref, *, seq):
    gid = pl.program_id(2)
    rows = 512
    half = ATT_HEAD_DIM // 2

    def rot(i, _):
        sl = pl.ds(pl.multiple_of(i * rows, rows), rows)
        cos = cos_ref[sl, :]
        sin = sin_ref[sl, :]
        q = q_ref[0, sl, :]
        k = k_ref[0, sl, :]
        qr_ref[sl, :] = (q * cos + pltpu.roll(q, half, 1) * sin) * (ATT_HEAD_DIM ** -0.5)
        kr_ref[sl, :] = k * cos + pltpu.roll(k, half, 1) * sin
        return 0

    lax.fori_loop(0, seq // rows, rot, 0)

    def group(gi, dil):
        cls_len = seq // dil
        bq = min(ATT_QBLOCK, cls_len)
        if cls_len <= bq + 2 * ATT_SPAN:
            bq = cls_len
        bk = min(bq + 2 * ATT_SPAN, cls_len)
        nqb = cls_len // bq
        rel0 = (lax.broadcasted_iota(jnp.int32, (bq, bk), 0)
                - lax.broadcasted_iota(jnp.int32, (bq, bk), 1))
        ones = jnp.ones((bk, ATT_HEAD_DIM), BF16)

        def body(it, _):
            cls = it // nqb
            p0 = (it % nqb) * bq
            ks = jnp.clip(p0 - ATT_SPAN, 0, cls_len - bk)
            if dil == 1:
                rq = pl.ds(p0, bq)
                rk = pl.ds(ks, bk)
            else:
                rq = pl.ds(cls + dil * p0, bq, stride=dil)
                rk = pl.ds(cls + dil * ks, bk, stride=dil)
            qb = qr_ref[rq, :].astype(BF16)
            kb = kr_ref[rk, :].astype(BF16)
            vb = v_ref[0, rk, :].astype(BF16)
            s = lax.dot_general(qb, kb, (((1,), (1,)), ((), ())), preferred_element_type=F32)
            band = jnp.abs(rel0 + (p0 - ks)) <= ATT_SPAN
            s = jnp.where(band, s, NEG_INF)
            m = jnp.max(s, axis=-1, keepdims=True)
            p = jnp.exp(s - m)
            pv = jnp.dot(p.astype(BF16), jnp.concatenate([vb, ones], axis=1), preferred_element_type=F32)
            l = pv[:, ATT_HEAD_DIM:]
            og_ref[gi, rq, :] = pv[:, :ATT_HEAD_DIM] / l
            lse_ref[gi, rq, :] = m + jnp.log(l)
            return 0

        lax.fori_loop(0, dil * nqb, body, 0, unroll=min(8, dil * nqb))

    for gi, (_, dil) in enumerate(ATT_GROUPS):
        @pl.when(gid == gi)
        def _(gi=gi, dil=dil):
            group(gi, dil)

    @pl.when(gid == len(ATT_GROUPS) - 1)
    def _():
        def comb(i, _):
            sl = pl.ds(pl.multiple_of(i * rows, rows), rows)
            l0 = lse_ref[0, sl, :]
            l1 = lse_ref[1, sl, :]
            l2 = lse_ref[2, sl, :]
            mx = jnp.maximum(jnp.maximum(l0, l1), l2)
            w0 = jnp.exp(l0 - mx)
            w1 = jnp.exp(l1 - mx)
            w2 = jnp.exp(l2 - mx)
            num = w0 * og_ref[0, sl, :] + w1 * og_ref[1, sl, :] + w2 * og_ref[2, sl, :]
            o_ref[0, sl, :] = (num / (w0 + w1 + w2)).astype(o_ref.dtype)
            return 0

        lax.fori_loop(0, seq // rows, comb, 0)


def _attention(proj3, cos, sin):
    bsz, seq, _ = proj3.shape
    hpg = ATT_HEADS_PER_GROUP
    ng = len(ATT_GROUPS)
    blk = (1, seq, ATT_HEAD_DIM)
    q0 = ATT_OFF // ATT_HEAD_DIM
    return pl.pallas_call(
        functools.partial(_attn_kernel, seq=seq),
        name="attn",
        grid=(bsz, hpg, ng),
        in_specs=[pl.BlockSpec(blk, lambda b, s, g: (b, 0, q0 + g * hpg + s)),
                  pl.BlockSpec(blk, lambda b, s, g: (b, 0, q0 + ATT_HEADS + g * hpg + s)),
                  pl.BlockSpec(blk, lambda b, s, g: (b, 0, q0 + 2 * ATT_HEADS + g * hpg + s)),
                  pl.BlockSpec((seq, ATT_HEAD_DIM), lambda b, s, g: (0, 0)),
                  pl.BlockSpec((seq, ATT_HEAD_DIM), lambda b, s, g: (0, 0))],
        out_specs=pl.BlockSpec(blk, lambda b, s, g: (b, 0, s)),
        out_shape=jax.ShapeDtypeStruct((bsz, seq, ATT_OUT), BF16),
        scratch_shapes=[pltpu.VMEM((seq, ATT_HEAD_DIM), F32),
                        pltpu.VMEM((seq, ATT_HEAD_DIM), F32),
                        pltpu.VMEM((ng, seq, ATT_HEAD_DIM), F32),
                        pltpu.VMEM((ng, seq, ATT_HEAD_DIM), F32)],
        compiler_params=_cparams(("parallel", "parallel", "arbitrary")),
    )(proj3, proj3, proj3, cos, sin)


def _rwkv_kernel(rf_ref, vf_ref, kkf_ref, lwf_ref, kdf_ref, bf_ref, rb_ref, vb_ref, kkb_ref, lwb_ref, kdb_ref,
                 bb_ref, yf_ref, yb_ref, s_ref):
    cn = RW_CHUNK
    c2 = 2 * cn

    @pl.when(pl.program_id(1) == 0)
    def _():
        s_ref[...] = jnp.zeros_like(s_ref)

    ti = lax.broadcasted_iota(jnp.int32, (cn, cn), 0)
    si = lax.broadcasted_iota(jnp.int32, (cn, cn), 1)
    t2 = lax.broadcasted_iota(jnp.int32, (c2, c2), 0)
    s2 = lax.broadcasted_iota(jnp.int32, (c2, c2), 1)
    eye = (t2 == s2).astype(F32)
    same_blk = (t2 // cn) == (s2 // cn)
    top = t2 < cn
    lane = lax.broadcasted_iota(jnp.int32, (1, LANES), 1)
    first = lane < RW_HEAD
    same_head = ((lax.broadcasted_iota(jnp.int32, (LANES, LANES), 0) // RW_HEAD)
                 == (lax.broadcasted_iota(jnp.int32, (LANES, LANES), 1) // RW_HEAD))

    def stack(x):
        return jnp.concatenate([jnp.where(first, x, 0.0), jnp.where(first, 0.0, x)], axis=0)

    def mm(a, b):
        return jnp.dot(a.astype(BF16), b.astype(BF16), preferred_element_type=F32)

    def mm_nt(a, b):
        return lax.dot_general(a.astype(BF16), b.astype(BF16), (((1,), (1,)), ((), ())),
                               preferred_element_type=F32)

    lanes = [slice(hp * LANES, (hp + 1) * LANES) for hp in range(RW_WIDTH // LANES)]
    inst = []
    xq, vs, bk, decay_end, amat, strict, incl = [], [], [], [], [], [], []
    for z, (r_ref, v_ref, kk_ref, lw_ref, kd_ref, b_ref, y_ref) in enumerate(
            ((rf_ref, vf_ref, kkf_ref, lwf_ref, kdf_ref, bf_ref, yf_ref),
             (rb_ref, vb_ref, kkb_ref, lwb_ref, kdb_ref, bb_ref, yb_ref))):
        rev = z == 1
        tri = ((si >= ti) if rev else (si <= ti)).astype(BF16)
        order = (t2 % cn - s2 % cn) if rev else (s2 % cn - t2 % cn)
        strict_z = (order < 0) & same_blk
        incl_z = (order <= 0) & same_blk
        for row in range(r_ref.shape[0]):
            lw_all = lw_ref[0, row]
            lw_hi = lw_all.astype(BF16)
            lw_lo = (lw_all - lw_hi.astype(F32)).astype(BF16)
            g_all = (jnp.dot(tri, lw_hi, preferred_element_type=F32)
                     + jnp.dot(tri, lw_lo, preferred_element_type=F32))
            for hp, ls in enumerate(lanes):
                r = r_ref[row, :, ls]
                v = v_ref[row, :, ls]
                kk = kk_ref[row, :, ls]
                kd = kd_ref[0, row, :, ls]
                b = b_ref[0, row, :, ls]
                g = g_all[:, ls]
                g_end = g[0:1] if rev else g[cn - 1:cn]
                e_neg = jnp.exp(-g)
                e_end = jnp.exp(g_end - g)
                x = jnp.concatenate([stack(kk * jnp.exp(g - lw_all[:, ls])), stack(r * jnp.exp(g))],
                                    axis=0).astype(BF16)
                y = jnp.concatenate([b * e_neg, kd * e_neg], axis=0).astype(BF16)
                inst.append((z, row, hp, ls, v_ref, y_ref))
                strict.append(strict_z)
                incl.append(incl_z)
                xq.append(x)
                vs.append(stack(v).astype(BF16))
                bk.append(jnp.concatenate([b * e_end, kd * e_end], axis=0).astype(BF16))
                decay_end.append(jnp.exp(g_end))
                amat.append(lax.dot_general(x, y, (((1,), (1,)), ((), ())), preferred_element_type=F32))
    swapped = [pltpu.roll(am, cn, 1) for am in amat]
    tmat = [jnp.where(st, jnp.where(top, am[:c2], sw[:c2]), 0.0) for am, sw, st in zip(amat, swapped, strict)]
    rest = [jnp.concatenate([jnp.where(st, jnp.where(top, sw[:c2], am[:c2]), 0.0),
                             jnp.where(ic, jnp.where(top, sw[c2:], am[c2:]), 0.0)], axis=0).astype(BF16)
            for am, sw, st, ic in zip(amat, swapped, strict, incl)]
    arb = [jnp.where(ic, jnp.where(top, am[c2:], sw[c2:]), 0.0).astype(BF16)
           for am, sw, ic in zip(amat, swapped, incl)]
    ninv = [eye - t for t in tmat]
    pw = [t.astype(BF16) for t in tmat]
    pw = [jnp.dot(p, p, preferred_element_type=F32).astype(BF16) for p in pw]
    nsq = cn.bit_length() - 2
    for it in range(nsq):
        if it < nsq - 1:
            both = [jnp.dot(p, jnp.concatenate([p, n.astype(BF16)], axis=1), preferred_element_type=F32)
                    for p, n in zip(pw, ninv)]
            pw = [bo[:, :c2].astype(BF16) for bo in both]
            ninv = [n + bo[:, c2:] for n, bo in zip(ninv, both)]
        else:
            ninv = [n + jnp.dot(p, n.astype(BF16), preferred_element_type=F32) for p, n in zip(pw, ninv)]
    mv = [jnp.dot(m, v, preferred_element_type=F32) for m, v in zip(rest, vs)]
    s0 = [s_ref[z, row, hp] for z, row, hp, _, _, _ in inst]
    xs = [mm_nt(x, s) for x, s in zip(xq, s0)]
    us = [mm(n, -x[:c2] - m[:c2]) for n, x, m in zip(ninv, xs, mv)]
    ys = [x[c2:] + m[c2:] + mm(ab, u) for x, m, ab, u in zip(xs, mv, arb, us)]
    for n, (z, row, hp, ls, v_ref, y_ref) in enumerate(inst):
        y_ref[row, :, ls] = ys[n][:cn] + ys[n][cn:]
        uv = jnp.concatenate([us[n][:cn] + us[n][cn:], v_ref[row, :, ls].astype(F32)], axis=0)
        upd = mm(uv.T, bk[n])
        s_ref[z, row, hp] = jnp.where(same_head, s0[n] * decay_end[n] + upd, 0.0)


def _rwkv_scan(r, v, kk, lw, kd, b):
    bsz, seq, width = r.shape
    cn = RW_CHUNK
    nc = seq // cn
    rows = RW_ROWS_PER_STEP
    assert bsz % rows == 0
    fwd = pl.BlockSpec((rows, cn, width), lambda b, c: (b, c, 0))
    bwd = pl.BlockSpec((rows, cn, width), lambda b, c: (b, nc - 1 - c, 0))
    fwd_d = pl.BlockSpec((1, rows, cn, width), lambda b, c: (0, b, c, 0))
    bwd_d = pl.BlockSpec((1, rows, cn, width), lambda b, c: (1, b, nc - 1 - c, 0))
    out = jax.ShapeDtypeStruct((bsz, seq, width), F32)
    return pl.pallas_call(
        _rwkv_kernel,
        name="rwkv",
        grid=(bsz // rows, nc),
        in_specs=[fwd, fwd, fwd, fwd_d, fwd_d, fwd_d, bwd, bwd, bwd, bwd_d, bwd_d, bwd_d],
        out_specs=[fwd, bwd],
        out_shape=[out, out],
        scratch_shapes=[pltpu.VMEM((2, rows, width // LANES, LANES, LANES), F32)],
        compiler_params=_cparams(("parallel", "arbitrary")),
    )(r, v, kk, lw, kd, b, r, v, kk, lw, kd, b)


def _head_sum(x, ones, passes):
    gw = ones.shape[0]
    cols = []
    for c0 in range(0, x.shape[1], gw):
        xc = x[:, c0:c0 + gw]
        hi = xc.astype(BF16)
        out = jnp.dot(hi, ones, preferred_element_type=F32)
        if passes == 2:
            lo = (xc - hi.astype(F32)).astype(BF16)
            out = out + jnp.dot(lo, ones, preferred_element_type=F32)
        cols.append(out)
    return jnp.concatenate(cols, axis=1)


def _prep_kernel(r_ref, k_ref, v_ref, l_ref, rp_ref, kp_ref, vp_ref, lp_ref, rn_ref, kn_ref, vn_ref, ln_ref,
                 mu_ref, dbase_ref, dup_ref, ibase_ref, iup_ref, gup_ref, kkw_ref, ka_ref, rk_ref, ones_ref,
                 ro_ref, vo_ref, kko_ref, go_ref, bvo_ref, lwo_ref, kdo_ref, bo_ref, *, tm, seq):
    i = pl.program_id(0)
    bpr = seq // tm
    has_prev = (i % bpr != 0).astype(F32)
    has_next = (i % bpr != bpr - 1).astype(F32)
    rowi = lax.broadcasted_iota(jnp.int32, (tm, 1), 0)
    cw = RW_WIDTH

    def shifted(main_ref, prev_ref, next_ref, c0, c1):
        z = main_ref[...]
        zp = jnp.where(rowi == 0, prev_ref[SUBLANES - 1:SUBLANES, :] * has_prev, pltpu.roll(z, 1, 0))
        zn = jnp.where(rowi == tm - 1, next_ref[0:1, :] * has_next, pltpu.roll(z, tm - 1, 0))
        return z + mu_ref[0:1, c0:c1] * (zp - z) + mu_ref[1:2, c0:c1] * (zn - z)

    r = shifted(r_ref, rp_ref, rn_ref, 0, cw)
    k = shifted(k_ref, kp_ref, kn_ref, cw, 2 * cw)
    v = shifted(v_ref, vp_ref, vn_ref, 2 * cw, 3 * cw)
    lat = shifted(l_ref, lp_ref, ln_ref, 3 * cw, 3 * cw + LAT_PAD)
    nd, ni = 2 * DECAY_LORA, 2 * ICLR_LORA
    wl = jnp.tanh(lat[:, :nd])
    wl_hi = wl.astype(BF16)
    wl_lo = (wl - wl_hi.astype(F32)).astype(BF16)
    w_raw = (dbase_ref[...] + jnp.dot(wl_hi, dup_ref[0], preferred_element_type=F32)
             + jnp.dot(wl_lo, dup_ref[0], preferred_element_type=F32)
             + jnp.dot(wl_hi, dup_ref[1], preferred_element_type=F32))
    lw = jax.nn.sigmoid(w_raw) * (-DECAY_SCALE)
    a = jax.nn.sigmoid(ibase_ref[...] + jnp.dot(lat[:, nd:nd + ni].astype(BF16), iup_ref[...],
                                                preferred_element_type=F32))
    g = jnp.dot(jax.nn.sigmoid(lat[:, nd + ni:]).astype(BF16), gup_ref[...], preferred_element_type=F32)
    ones = ones_ref[...]
    kkr = k * kkw_ref[...]
    kk = kkr / jnp.maximum(jnp.sqrt(_head_sum(kkr * kkr, ones, 2)), 1e-12)
    ka = ka_ref[...]
    k_fix = k * (1.0 - ka)
    k_var = k * ka
    kd0 = k_fix + k_var * a[:, :cw]
    kd1 = k_fix + k_var * a[:, cw:]
    bonus = _head_sum(r * (kd0 + kd1) * rk_ref[...], ones, 2)
    ro_ref[...] = r
    vo_ref[...] = v.astype(vo_ref.dtype)
    kko_ref[...] = kk
    go_ref[...] = g.astype(go_ref.dtype)
    bvo_ref[...] = (bonus * v).astype(bvo_ref.dtype)
    lwo_ref[0] = lw[:, :cw]
    lwo_ref[1] = lw[:, cw:]
    kdo_ref[0] = kd0
    kdo_ref[1] = kd1
    bo_ref[0] = kk * a[:, :cw]
    bo_ref[1] = kk * a[:, cw:]


def _prep(proj, mu, dbase, dup, ibase, iup, gup, kkw, ka, rk, ones, seq, tm=256):
    m = proj.shape[0]
    cw = RW_WIDTH
    bpr = seq // tm
    hb = tm // SUBLANES
    nhb = m // SUBLANES
    r0 = RW_OFF // cw
    l0 = LAT_OFF // LAT_PAD
    cols = [(cw, r0), (cw, r0 + 1), (cw, r0 + 2), (LAT_PAD, l0)]
    main = [pl.BlockSpec((tm, w), lambda i, c=c: (i, c)) for w, c in cols]
    prev = [pl.BlockSpec((SUBLANES, w), lambda i, c=c: (jnp.maximum(i * hb - 1, 0), c)) for w, c in cols]
    nxt = [pl.BlockSpec((SUBLANES, w), lambda i, c=c: (jnp.minimum((i + 1) * hb, nhb - 1), c)) for w, c in cols]
    params = [mu, dbase, dup, ibase, iup, gup, kkw, ka, rk, ones]
    pspecs = [pl.BlockSpec(t.shape, lambda i, n=t.ndim: (0,) * n) for t in params]
    oblk = pl.BlockSpec((tm, cw), lambda i: (i, 0))
    dblk = pl.BlockSpec((2, tm, cw), lambda i: (0, i, 0))
    one = jax.ShapeDtypeStruct((m, cw), F32)
    half = jax.ShapeDtypeStruct((m, cw), BF16)
    two = jax.ShapeDtypeStruct((2, m, cw), F32)
    return pl.pallas_call(
        functools.partial(_prep_kernel, tm=tm, seq=seq),
        name="rwkv_prep",
        grid=(m // tm,),
        in_specs=main + prev + nxt + pspecs,
        out_specs=[oblk] * 5 + [dblk] * 3,
        out_shape=[one, half, one, half, half] + [two] * 3,
        compiler_params=_cparams(("parallel",)),
    )(*([proj] * 12), *params)


def _mix_kernel(att_ref, yf_ref, yb_ref, bv_ref, g_ref, lnw_ref, lnb_ref, ones_ref, ga_ref, gr_ref, x_ref, gt_ref,
                gain_ref, wa_ref, wr_ref, wo_ref, o_ref):
    ones = ones_ref[...]
    y = yf_ref[...] + yb_ref[...]
    mu = _head_sum(y, ones, 1) * (1.0 / RW_HEAD)
    yc = y - mu
    var = _head_sum(yc * yc, ones, 1) * (1.0 / RW_HEAD)
    rw = ((yc * lax.rsqrt(var + GN_EPS)) * lnw_ref[...] + lnb_ref[...] + bv_ref[...]) * g_ref[...]
    pa = jnp.dot(att_ref[...], wa_ref[...], preferred_element_type=F32)
    pr = jnp.dot(rw.astype(BF16), wr_ref[...], preferred_element_type=F32)
    merged = jax.nn.sigmoid(ga_ref[...]) * pa + jax.nn.sigmoid(gr_ref[...]) * pr
    mix = jnp.dot(merged.astype(BF16), wo_ref[...], preferred_element_type=F32)
    o_ref[...] = x_ref[...] + gt_ref[0] * _rms(mix, gain_ref[...])


def _mix(att, yf, yb, bv, g, lnw, lnb, ones, proj, x, gt, gain, wa, wr, wo, seq, tm=256):
    m, d = x.shape
    cw = RW_WIDTH
    bpr = seq // tm
    gblk = GATE_OFF // d
    const = lambda i: (0, 0)
    resident = lambda t: pl.BlockSpec(t.shape, const, pipeline_mode=pl.Buffered(1))
    return pl.pallas_call(
        _mix_kernel,
        name="mix",
        grid=(m // tm,),
        in_specs=[pl.BlockSpec((tm, att.shape[1]), lambda i: (i, 0)),
                  pl.BlockSpec((tm, cw), lambda i: (i, 0)),
                  pl.BlockSpec((tm, cw), lambda i: (i, 0)),
                  pl.BlockSpec((tm, cw), lambda i: (i, 0)),
                  pl.BlockSpec((tm, cw), lambda i: (i, 0)),
                  pl.BlockSpec((1, cw), const),
                  pl.BlockSpec((1, cw), const),
                  resident(ones),
                  pl.BlockSpec((tm, d), lambda i: (i, gblk)),
                  pl.BlockSpec((tm, d), lambda i: (i, gblk + 1)),
                  pl.BlockSpec((tm, d), lambda i: (i, 0)),
                  pl.BlockSpec((1, 1, d), lambda i: (i // bpr, 0, 0)),
                  pl.BlockSpec((1, d), const),
                  resident(wa), resident(wr), resident(wo)],
        out_specs=pl.BlockSpec((tm, d), lambda i: (i, 0)),
        out_shape=jax.ShapeDtypeStruct((m, d), F32),
        compiler_params=_cparams(("parallel",)),
    )(att, yf, yb, bv, g, lnw, lnb, ones, proj, proj, x, gt, gain, wa, wr, wo)


def _ffn_kernel(x_ref, xp_ref, xn_ref, gpre_ref, sc_ref, sh_ref, wg_ref, wu_ref, cw_ref, cb_ref, wd_ref,
                gt_ref, gpost_ref, o_ref, h_ref, *, tm, seq):
    i = pl.program_id(0)
    j = pl.program_id(1)
    halo = SUBLANES
    bpr = seq // tm

    @pl.when(j == 0)
    def _():
        def modnorm(x):
            return _rms(x, gpre_ref[...]) * (1.0 + sc_ref[0]) + sh_ref[0]

        top = lax.broadcasted_iota(jnp.int32, (BF16_ROWS, 1), 0) < halo
        keep_first = jnp.where(top, (i % bpr != 0).astype(F32), 1.0)
        keep_last = jnp.where(top, 1.0, (i % bpr != bpr - 1).astype(F32))
        first = jnp.concatenate([xp_ref[...], x_ref[0:halo, :]], axis=0)
        last = jnp.concatenate([x_ref[tm - halo:tm, :], xn_ref[...]], axis=0)
        h_ref[0:BF16_ROWS, :] = (modnorm(first) * keep_first).astype(BF16)
        h_ref[tm:tm + BF16_ROWS, :] = (modnorm(last) * keep_last).astype(BF16)

        def middle(r0):
            h_ref[r0:r0 + BF16_ROWS, :] = modnorm(x_ref[r0 - halo:r0 - halo + BF16_ROWS, :]).astype(BF16)

        _row_chunks(1, tm // BF16_ROWS, middle, None)
        o_ref[...] = jnp.zeros_like(o_ref)

    ext = tm + 2 * halo
    gate = jnp.dot(h_ref[...], wg_ref[...], preferred_element_type=F32)
    prev = pltpu.roll(gate, 1, 0)[halo:halo + tm]
    nxt = pltpu.roll(gate, ext - 1, 0)[halo:halo + tm]
    u = cw_ref[0:1, :] * prev + cw_ref[1:2, :] * gate[halo:halo + tm] + cw_ref[2:3, :] * nxt + cb_ref[...]
    up = jnp.dot(h_ref[halo:halo + tm, :], wu_ref[...], preferred_element_type=F32)
    act = jax.nn.gelu(u, approximate=True) * up
    o_ref[...] += jnp.dot(act.astype(BF16), wd_ref[...], preferred_element_type=F32)

    @pl.when(j == pl.num_programs(1) - 1)
    def _():
        def finish(r0):
            rows = slice(r0, r0 + BF16_ROWS)
            o_ref[rows, :] = x_ref[rows, :] + gt_ref[0] * _rms(o_ref[rows, :], gpost_ref[...])

        _row_chunks(0, tm // BF16_ROWS, finish, None)


def _ffn(x, gpre, sc, sh, wg, wu, cw, cb, wd, gt, gpost, seq, tm=1024, tf=FF_TILE):
    m, d = x.shape
    f = wg.shape[1]
    bpr = seq // tm
    hb = tm // SUBLANES
    nhb = m // SUBLANES
    const = lambda i, j: (0, 0)
    bidx = lambda i, j: (i // bpr, 0, 0)
    return pl.pallas_call(
        functools.partial(_ffn_kernel, tm=tm, seq=seq),
        name="ffn",
        grid=(m // tm, f // tf),
        in_specs=[pl.BlockSpec((tm, d), lambda i, j: (i, 0)),
                  pl.BlockSpec((SUBLANES, d), lambda i, j: (jnp.maximum(i * hb - 1, 0), 0)),
                  pl.BlockSpec((SUBLANES, d), lambda i, j: (jnp.minimum((i + 1) * hb, nhb - 1), 0)),
                  pl.BlockSpec((1, d), const),
                  pl.BlockSpec((1, 1, d), bidx),
                  pl.BlockSpec((1, 1, d), bidx),
                  pl.BlockSpec((d, tf), lambda i, j: (0, j)),
                  pl.BlockSpec((d, tf), lambda i, j: (0, j)),
                  pl.BlockSpec((SUBLANES, tf), lambda i, j: (0, j)),
                  pl.BlockSpec((1, tf), lambda i, j: (0, j)),
                  pl.BlockSpec((tf, d), lambda i, j: (j, 0)),
                  pl.BlockSpec((1, 1, d), bidx),
                  pl.BlockSpec((1, d), const)],
        out_specs=pl.BlockSpec((tm, d), lambda i, j: (i, 0)),
        out_shape=jax.ShapeDtypeStruct((m, d), F32),
        scratch_shapes=[pltpu.VMEM((tm + 2 * SUBLANES, d), BF16)],
        compiler_params=_cparams(("parallel", "arbitrary")),
    )(x, x, x, gpre, sc, sh, wg, wu, cw, cb, wd, gt, gpost)


def _rope_tables(seq):
    half = ATT_HEAD_DIM // 2
    inv_freq = 1.0 / (ROPE_THETA ** (jnp.arange(half, dtype=F32) / half))
    ang = jnp.arange(seq, dtype=F32)[:, None] * inv_freq[None, :]
    cos = jnp.cos(ang)
    sin = jnp.sin(ang)
    return jnp.concatenate([cos, cos], axis=-1), jnp.concatenate([-sin, sin], axis=-1)


def _trunk(x, ada, p):
    bsz, seq, d = x.shape
    m = bsz * seq
    sh1, sc1, gt1, sh2, sc2, gt2 = [t[:, None, :] for t in jnp.split(ada, 6, axis=-1)]
    x2 = x.reshape(m, d)
    proj = _inproj(x2, p['ln_mix_pre'], sc1, sh1, p['w_in'], seq)
    proj3 = proj.reshape(bsz, seq, N_IN_PAD)
    cos, sin = _rope_tables(seq)
    att = _attention(proj3, cos, sin)
    r, v, kk, g, bv, lw, kd, b = _prep(proj, p['shift_mu'], p['decay_base'], p['decay_up'], p['iclr_base'],
                                       p['iclr_up'], p['gate_up'], p['k_k'], p['k_a'], p['r_k'], p['head_ones'], seq)
    per_seq = lambda t: t.reshape(t.shape[:-2] + (bsz, seq, RW_WIDTH))
    yf, yb = _rwkv_scan(per_seq(r), per_seq(v), per_seq(kk), per_seq(lw), per_seq(kd), per_seq(b))
    x1 = _mix(att.reshape(m, ATT_OUT), yf.reshape(m, RW_WIDTH), yb.reshape(m, RW_WIDTH), bv, g,
              p['lnx_w'], p['lnx_b'], p['head_ones'],
              proj, x2, gt1, p['ln_mix_post'], p['w_att_branch'], p['w_rwkv_branch'], p['w_out'], seq)
    y = _ffn(x1, p['ln_ffn_pre'], sc2, sh2, p['w_ffn_gate'], p['w_ffn_up'], p['ffn_conv_w'], p['ffn_conv_b'],
             p['w_ffn_down'], gt2, p['ln_ffn_post'], seq)
    return y.reshape(bsz, seq, d)


def kernel(x_prompt, x_sample, c_prompt, c_sample, ln_mix_pre, ln_mix_post, ln_ffn_pre, ln_ffn_post,
           w_ada, b_ada, w_in, shift_mu, decay_base, decay_up, iclr_base, iclr_up, gate_up,
           k_k, k_a, r_k, lnx_w, lnx_b, w_att_branch, w_rwkv_branch, w_out,
           w_ffn_gate, w_ffn_up, ffn_conv_w, ffn_conv_b, w_ffn_down):
    depth = w_in.shape[0]
    nb_p = c_prompt.shape[0]
    nb_s = c_sample.shape[0]
    c_all = jnp.concatenate([c_prompt, c_sample], axis=0)
    c_all = jnp.pad(c_all, ((0, (-c_all.shape[0]) % SUBLANES), (0, 0)))
    fpad = D_FF_PAD - D_FF
    xp, xs = x_prompt, x_sample
    lane_head = jnp.arange(MXU_DIM_V7X) // RW_HEAD
    head_ones = (lane_head[:, None] == lane_head[None, :]).astype(BF16)
    zdl = jnp.zeros((DECAY_LORA, RW_WIDTH), F32)
    zil = jnp.zeros((ICLR_LORA, RW_WIDTH), F32)
    for l in range(depth):
        att_end = 3 * ATT_WIDTH
        rw_end = att_end + RW_COLS
        w_in_b = w_in[l].astype(BF16)
        rw_cols = jnp.pad(w_in_b[:, att_end:rw_end], ((0, 0), (0, LAT_PAD - LAT_COLS)))
        w_in_l = jnp.concatenate([w_in_b[:, rw_end:], rw_cols, w_in_b[:, :att_end]], axis=1)
        dup = jnp.block([[decay_up[l][0], zdl], [zdl, decay_up[l][1]]])
        dup_hi = dup.astype(BF16)
        p = dict(
            ln_mix_pre=ln_mix_pre[l][None], ln_mix_post=ln_mix_post[l][None],
            ln_ffn_pre=ln_ffn_pre[l][None], ln_ffn_post=ln_ffn_post[l][None],
            w_in=w_in_l, head_ones=head_ones,
            shift_mu=jnp.pad(shift_mu[l], ((0, 0), (0, RW_COLS_PAD - RW_COLS))),
            decay_base=decay_base[l].reshape(1, 2 * RW_WIDTH),
            decay_up=jnp.stack([dup_hi, (dup - dup_hi.astype(F32)).astype(BF16)]),
            iclr_base=iclr_base[l].reshape(1, 2 * RW_WIDTH),
            iclr_up=jnp.block([[iclr_up[l][0], zil], [zil, iclr_up[l][1]]]).astype(BF16),
            gate_up=jnp.pad(gate_up[l], ((0, LAT_PAD - LAT_COLS), (0, 0))).astype(BF16),
            k_k=k_k[l][None], k_a=k_a[l][None], r_k=r_k[l].reshape(1, RW_WIDTH),
            lnx_w=lnx_w[l][None], lnx_b=lnx_b[l][None],
            w_att_branch=w_att_branch[l].astype(BF16), w_rwkv_branch=w_rwkv_branch[l].astype(BF16),
            w_out=w_out[l].astype(BF16),
            w_ffn_gate=jnp.zeros((D_MODEL, D_FF_PAD), BF16).at[:, :D_FF].set(w_ffn_gate[l].astype(BF16)),
            w_ffn_up=jnp.zeros((D_MODEL, D_FF_PAD), BF16).at[:, :D_FF].set(w_ffn_up[l].astype(BF16)),
            ffn_conv_w=jnp.pad(ffn_conv_w[l], ((0, SUBLANES - 3), (0, fpad))),
            ffn_conv_b=jnp.pad(ffn_conv_b[l], ((0, fpad),))[None],
            w_ffn_down=jnp.zeros((D_FF_PAD, D_MODEL), BF16).at[:D_FF, :].set(w_ffn_down[l].astype(BF16)),
        )
        ada = _ada(c_all, w_ada[l], b_ada[l][None])
        xp = _trunk(xp, ada[:nb_p], p)
        xs = _trunk(xs, ada[nb_p:nb_p + nb_s], p)
    return (xp, xs)
```

```python
import functools
import math

import jax
import jax.numpy as jnp
from jax import lax
from jax.experimental import pallas as pl
from jax.experimental.pallas import tpu as pltpu

F32 = jnp.float32
BF16 = jnp.bfloat16

D_MODEL = 2048
ATT_GROUPS = ((128, 1), (512, 4), (2048, 16))
ATT_HEADS_PER_GROUP = 4
ATT_HEAD_DIM = 128
ATT_HEADS = ATT_HEADS_PER_GROUP * len(ATT_GROUPS)
ATT_WIDTH = ATT_HEADS * ATT_HEAD_DIM
ATT_OUT = ATT_HEADS_PER_GROUP * ATT_HEAD_DIM
ROPE_THETA = 10000.0
RW_HEAD = 64
RW_WIDTH = D_MODEL // 2
RW_HEADS = RW_WIDTH // RW_HEAD
DECAY_LORA = 64
ICLR_LORA = 64
GATE_LORA = 160
RW_COLS = 3 * RW_WIDTH + 2 * DECAY_LORA + 2 * ICLR_LORA + GATE_LORA
N_IN = 3 * ATT_WIDTH + RW_COLS + 2 * D_MODEL
D_FF = ((8 * D_MODEL // 3 + 127) // 128) * 128
RMS_EPS = 1e-6
GN_EPS = 64e-5
NEG_INF = -1e30
DECAY_SCALE = math.exp(-0.5)

LANES = 128
SUBLANES = 8
BF16_ROWS = 16
MXU_DIM_V7X = 256
VMEM_BYTES_V7X = 64 * 1024 * 1024
VMEM_COMPILER_RESERVE = 4 * 1024 * 1024
VMEM_LIMIT = VMEM_BYTES_V7X - VMEM_COMPILER_RESERVE

GATE_OFF = 0
RW_OFF = GATE_OFF + 2 * D_MODEL
LAT_OFF = RW_OFF + 3 * RW_WIDTH
LAT_COLS = RW_COLS - 3 * RW_WIDTH
LAT_PAD = 512
ATT_OFF = LAT_OFF + LAT_PAD
N_IN_PAD = ATT_OFF + 3 * ATT_WIDTH
RW_COLS_PAD = 3 * RW_WIDTH + LAT_PAD
FF_TILE = 512
D_FF_PAD = -(-D_FF // FF_TILE) * FF_TILE
RW_CHUNK = 64
RW_ROWS_PER_STEP = 2
ATT_SPAN = 64
ATT_QBLOCK = 128


def _cparams(sem):
    return pltpu.CompilerParams(dimension_semantics=sem, vmem_limit_bytes=VMEM_LIMIT)


def _rms(x, gain):
    return x * lax.rsqrt(jnp.mean(x * x, axis=-1, keepdims=True) + RMS_EPS) * gain


def _row_chunks(first, last, fn, unroll=4):
    if unroll is None:
        for c in range(first, last):
            fn(c * BF16_ROWS)
        return

    def body(c, carry):
        fn(pl.multiple_of(c * BF16_ROWS, BF16_ROWS))
        return carry

    lax.fori_loop(first, last, body, 0, unroll=min(unroll, last - first))


def _ada_kernel(c_ref, w_ref, b_ref, o_ref):
    c = c_ref[...]
    s = c * jax.nn.sigmoid(c)
    o_ref[...] = jnp.dot(s.astype(BF16), w_ref[...].astype(BF16),
                         preferred_element_type=F32) + b_ref[...]


def _ada(c, w, b, tn=1024):
    m, k = c.shape
    n = w.shape[1]
    return pl.pallas_call(
        _ada_kernel,
        name="ada",
        grid=(n // tn,),
        in_specs=[pl.BlockSpec((m, k), lambda j: (0, 0)),
                  pl.BlockSpec((k, tn), lambda j: (0, j)),
                  pl.BlockSpec((1, tn), lambda j: (0, j))],
        out_specs=pl.BlockSpec((m, tn), lambda j: (0, j)),
        out_shape=jax.ShapeDtypeStruct((m, n), F32),
        compiler_params=_cparams(("parallel",)),
    )(c, w, b)


def _inproj_kernel(x_ref, g_ref, sc_ref, sh_ref, w_ref, o_ref, h_ref, *, part):
    i = pl.program_id(0)
    j = pl.program_id(1)

    def norm_rows(slot, start, nrows, unroll):
        def chunk(r0):
            rows = pl.ds(pl.multiple_of(start + r0, BF16_ROWS), BF16_ROWS)
            h = _rms(x_ref[rows, :], g_ref[...]) * (1.0 + sc_ref[0]) + sh_ref[0]
            h_ref[slot, rows, :] = h.astype(BF16)

        _row_chunks(0, nrows // BF16_ROWS, chunk, unroll)

    @pl.when((i == 0) & (j == 0))
    def _():
        norm_rows(0, 0, x_ref.shape[0], 4)

    o_ref[...] = jnp.dot(h_ref[i % 2], w_ref[...], preferred_element_type=F32)

    start = jnp.minimum(jnp.maximum(j - 1, 0) * part, x_ref.shape[0] - part)
    norm_rows((i + 1) % 2, pl.multiple_of(start, BF16_ROWS), part, None)


def _inproj(x, gain, sc, sh, w, seq, tm=1024, tn=1536):
    m, k = x.shape
    n = w.shape[1]
    bpr = seq // tm
    ni = m // tm
    nsl = n // tn - 1
    part = -(-tm // (nsl * BF16_ROWS)) * BF16_ROWS
    ahead = lambda i, j: jnp.minimum(i + jnp.minimum(j, 1), ni - 1)
    return pl.pallas_call(
        functools.partial(_inproj_kernel, part=part),
        name="inproj",
        grid=(ni, n // tn),
        in_specs=[pl.BlockSpec((tm, k), lambda i, j: (ahead(i, j), 0)),
                  pl.BlockSpec((1, k), lambda i, j: (0, 0)),
                  pl.BlockSpec((1, 1, k), lambda i, j: (ahead(i, j) // bpr, 0, 0)),
                  pl.BlockSpec((1, 1, k), lambda i, j: (ahead(i, j) // bpr, 0, 0)),
                  pl.BlockSpec((k, tn), lambda i, j: (0, j))],
        out_specs=pl.BlockSpec((tm, tn), lambda i, j: (i, j)),
        out_shape=jax.ShapeDtypeStruct((m, n), F32),
        scratch_shapes=[pltpu.VMEM((2, tm, k), BF16)],
        compiler_params=_cparams(("arbitrary", "arbitrary")),
    )(x, gain, sc, sh, w)


def _attn_kernel(q_ref, k_ref, v_ref, cos_ref, sin_ref, o_ref, qr_ref, kr_ref, og_ref, lse_ref, *, seq):
    gid = pl.program_id(2)
    rows = 512
    half = ATT_HEAD_DIM // 2

    def rot(i, _):
        sl = pl.ds(pl.multiple_of(i * rows, rows), rows)
        cos = cos_ref[sl, :]
        sin = sin_ref[sl, :]
        q = q_ref[0, sl, :]
        k = k_ref[0, sl, :]
        qr_ref[sl, :] = (q * cos + pltpu.roll(q, half, 1) * sin) * (ATT_HEAD_DIM ** -0.5)
        kr_ref[sl, :] = k * cos + pltpu.roll(k, half, 1) * sin
        return 0

    lax.fori_loop(0, seq // rows, rot, 0)

    def group(gi, dil):
        cls_len = seq // dil
        bq = min(ATT_QBLOCK, cls_len)
        if cls_len <= bq + 2 * ATT_SPAN:
            bq = cls_len
        bk = min(bq + 2 * ATT_SPAN, cls_len)
        nqb = cls_len // bq
        rel0 = (lax.broadcasted_iota(jnp.int32, (bq, bk), 0)
                - lax.broadcasted_iota(jnp.int32, (bq, bk), 1))
        ones = jnp.ones((bk, ATT_HEAD_DIM), BF16)

        def body(it, _):
            cls = it // nqb
            p0 = (it % nqb) * bq
            ks = jnp.clip(p0 - ATT_SPAN, 0, cls_len - bk)
            if dil == 1:
                rq = pl.ds(p0, bq)
                rk = pl.ds(ks, bk)
            else:
                rq = pl.ds(cls + dil * p0, bq, stride=dil)
                rk = pl.ds(cls + dil * ks, bk, stride=dil)
            qb = qr_ref[rq, :].astype(BF16)
            kb = kr_ref[rk, :].astype(BF16)
            vb = v_ref[0, rk, :].astype(BF16)
            s = lax.dot_general(qb, kb, (((1,), (1,)), ((), ())), preferred_element_type=F32)
            band = jnp.abs(rel0 + (p0 - ks)) <= ATT_SPAN
            s = jnp.where(band, s, NEG_INF)
            m = jnp.max(s, axis=-1, keepdims=True)
            p = jnp.exp(s - m)
            pv = jnp.dot(p.astype(BF16), jnp.concatenate([vb, ones], axis=1), preferred_element_type=F32)
            l = pv[:, ATT_HEAD_DIM:]
            og_ref[gi, rq, :] = pv[:, :ATT_HEAD_DIM] / l
            lse_ref[gi, rq, :] = m + jnp.log(l)
            return 0

        lax.fori_loop(0, dil * nqb, body, 0, unroll=min(8, dil * nqb))

    for gi, (_, dil) in enumerate(ATT_GROUPS):
        @pl.when(gid == gi)
        def _(gi=gi, dil=dil):
            group(gi, dil)

    @pl.when(gid == len(ATT_GROUPS) - 1)
    def _():
        def comb(i, _):
            sl = pl.ds(pl.multiple_of(i * rows, rows), rows)
            l0 = lse_ref[0, sl, :]
            l1 = lse_ref[1, sl, :]
            l2 = lse_ref[2, sl, :]
            mx = jnp.maximum(jnp.maximum(l0, l1), l2)
            w0 = jnp.exp(l0 - mx)
            w1 = jnp.exp(l1 - mx)
            w2 = jnp.exp(l2 - mx)
            num = w0 * og_ref[0, sl, :] + w1 * og_ref[1, sl, :] + w2 * og_ref[2, sl, :]
            o_ref[0, sl, :] = (num / (w0 + w1 + w2)).astype(o_ref.dtype)
            return 0

        lax.fori_loop(0, seq // rows, comb, 0)


def _attention(proj3, cos, sin):
    bsz, seq, _ = proj3.shape
    hpg = ATT_HEADS_PER_GROUP
    ng = len(ATT_GROUPS)
    blk = (1, seq, ATT_HEAD_DIM)
    q0 = ATT_OFF // ATT_HEAD_DIM
    return pl.pallas_call(
        functools.partial(_attn_kernel, seq=seq),
        name="attn",
        grid=(bsz, hpg, ng),
        in_specs=[pl.BlockSpec(blk, lambda b, s, g: (b, 0, q0 + g * hpg + s)),
                  pl.BlockSpec(blk, lambda b, s, g: (b, 0, q0 + ATT_HEADS + g * hpg + s)),
                  pl.BlockSpec(blk, lambda b, s, g: (b, 0, q0 + 2 * ATT_HEADS + g * hpg + s)),
                  pl.BlockSpec((seq, ATT_HEAD_DIM), lambda b, s, g: (0, 0)),
                  pl.BlockSpec((seq, ATT_HEAD_DIM), lambda b, s, g: (0, 0))],
        out_specs=pl.BlockSpec(blk, lambda b, s, g: (b, 0, s)),
        out_shape=jax.ShapeDtypeStruct((bsz, seq, ATT_OUT), BF16),
        scratch_shapes=[pltpu.VMEM((seq, ATT_HEAD_DIM), F32),
                        pltpu.VMEM((seq, ATT_HEAD_DIM), F32),
                        pltpu.VMEM((ng, seq, ATT_HEAD_DIM), F32),
                        pltpu.VMEM((ng, seq, ATT_HEAD_DIM), F32)],
        compiler_params=_cparams(("parallel", "parallel", "arbitrary")),
    )(proj3, proj3, proj3, cos, sin)


def _rwkv_kernel(rf_ref, vf_ref, kkf_ref, lwf_ref, kdf_ref, bf_ref, rb_ref, vb_ref, kkb_ref, lwb_ref, kdb_ref,
                 bb_ref, yf_ref, yb_ref, s_ref):
    cn = RW_CHUNK
    c2 = 2 * cn

    @pl.when(pl.program_id(1) == 0)
    def _():
        s_ref[...] = jnp.zeros_like(s_ref)

    ti = lax.broadcasted_iota(jnp.int32, (cn, cn), 0)
    si = lax.broadcasted_iota(jnp.int32, (cn, cn), 1)
    t2 = lax.broadcasted_iota(jnp.int32, (c2, c2), 0)
    s2 = lax.broadcasted_iota(jnp.int32, (c2, c2), 1)
    eye = (t2 == s2).astype(F32)
    same_blk = (t2 // cn) == (s2 // cn)
    top = t2 < cn
    lane = lax.broadcasted_iota(jnp.int32, (1, LANES), 1)
    first = lane < RW_HEAD
    same_head = ((lax.broadcasted_iota(jnp.int32, (LANES, LANES), 0) // RW_HEAD)
                 == (lax.broadcasted_iota(jnp.int32, (LANES, LANES), 1) // RW_HEAD))

    def stack(x):
        return jnp.concatenate([jnp.where(first, x, 0.0), jnp.where(first, 0.0, x)], axis=0)

    def mm(a, b):
        return jnp.dot(a.astype(BF16), b.astype(BF16), preferred_element_type=F32)

    def mm_nt(a, b):
        return lax.dot_general(a.astype(BF16), b.astype(BF16), (((1,), (1,)), ((), ())),
                               preferred_element_type=F32)

    lanes = [slice(hp * LANES, (hp + 1) * LANES) for hp in range(RW_WIDTH // LANES)]
    inst = []
    xq, vs, bk, decay_end, amat, strict, incl = [], [], [], [], [], [], []
    for z, (r_ref, v_ref, kk_ref, lw_ref, kd_ref, b_ref, y_ref) in enumerate(
            ((rf_ref, vf_ref, kkf_ref, lwf_ref, kdf_ref, bf_ref, yf_ref),
             (rb_ref, vb_ref, kkb_ref, lwb_ref, kdb_ref, bb_ref, yb_ref))):
        rev = z == 1
        tri = ((si >= ti) if rev else (si <= ti)).astype(BF16)
        order = (t2 % cn - s2 % cn) if rev else (s2 % cn - t2 % cn)
        strict_z = (order < 0) & same_blk
        incl_z = (order <= 0) & same_blk
        for row in range(r_ref.shape[0]):
            lw_all = lw_ref[0, row]
            lw_hi = lw_all.astype(BF16)
            lw_lo = (lw_all - lw_hi.astype(F32)).astype(BF16)
            g_all = (jnp.dot(tri, lw_hi, preferred_element_type=F32)
                     + jnp.dot(tri, lw_lo, preferred_element_type=F32))
            for hp, ls in enumerate(lanes):
                r = r_ref[row, :, ls]
                v = v_ref[row, :, ls]
                kk = kk_ref[row, :, ls]
                kd = kd_ref[0, row, :, ls]
                b = b_ref[0, row, :, ls]
                g = g_all[:, ls]
                g_end = g[0:1] if rev else g[cn - 1:cn]
                e_neg = jnp.exp(-g)
                e_end = jnp.exp(g_end - g)
                x = jnp.concatenate([stack(kk * jnp.exp(g - lw_all[:, ls])), stack(r * jnp.exp(g))],
                                    axis=0).astype(BF16)
                y = jnp.concatenate([b * e_neg, kd * e_neg], axis=0).astype(BF16)
                inst.append((z, row, hp, ls, v_ref, y_ref))
                strict.append(strict_z)
                incl.append(incl_z)
                xq.append(x)
                vs.append(stack(v).astype(BF16))
                bk.append(jnp.concatenate([b * e_end, kd * e_end], axis=0).astype(BF16))
                decay_end.append(jnp.exp(g_end))
                amat.append(lax.dot_general(x, y, (((1,), (1,)), ((), ())), preferred_element_type=F32))
    swapped = [pltpu.roll(am, cn, 1) for am in amat]
    tmat = [jnp.where(st, jnp.where(top, am[:c2], sw[:c2]), 0.0) for am, sw, st in zip(amat, swapped, strict)]
    rest = [jnp.concatenate([jnp.where(st, jnp.where(top, sw[:c2], am[:c2]), 0.0),
                             jnp.where(ic, jnp.where(top, sw[c2:], am[c2:]), 0.0)], axis=0).astype(BF16)
            for am, sw, st, ic in zip(amat, swapped, strict, incl)]
    arb = [jnp.where(ic, jnp.where(top, am[c2:], sw[c2:]), 0.0).astype(BF16)
           for am, sw, ic in zip(amat, swapped, incl)]
    ninv = [eye - t for t in tmat]
    pw = [t.astype(BF16) for t in tmat]
    pw = [jnp.dot(p, p, preferred_element_type=F32).astype(BF16) for p in pw]
    nsq = cn.bit_length() - 2
    for it in range(nsq):
        if it < nsq - 1:
            both = [jnp.dot(p, jnp.concatenate([p, n.astype(BF16)], axis=1), preferred_element_type=F32)
                    for p, n in zip(pw, ninv)]
            pw = [bo[:, :c2].astype(BF16) for bo in both]
            ninv = [n + bo[:, c2:] for n, bo in zip(ninv, both)]
        else:
            ninv = [n + jnp.dot(p, n.astype(BF16), preferred_element_type=F32) for p, n in zip(pw, ninv)]
    mv = [jnp.dot(m, v, preferred_element_type=F32) for m, v in zip(rest, vs)]
    s0 = [s_ref[z, row, hp] for z, row, hp, _, _, _ in inst]
    xs = [mm_nt(x, s) for x, s in zip(xq, s0)]
    us = [mm(n, -x[:c2] - m[:c2]) for n, x, m in zip(ninv, xs, mv)]
    ys = [x[c2:] + m[c2:] + mm(ab, u) for x, m, ab, u in zip(xs, mv, arb, us)]
    for n, (z, row, hp, ls, v_ref, y_ref) in enumerate(inst):
        y_ref[row, :, ls] = ys[n][:cn] + ys[n][cn:]
        uv = jnp.concatenate([us[n][:cn] + us[n][cn:], v_ref[row, :, ls].astype(F32)], axis=0)
        upd = mm(uv.T, bk[n])
        s_ref[z, row, hp] = jnp.where(same_head, s0[n] * decay_end[n] + upd, 0.0)


def _rwkv_scan(r, v, kk, lw, kd, b):
    bsz, seq, width = r.shape
    cn = RW_CHUNK
    nc = seq // cn
    rows = RW_ROWS_PER_STEP
    assert bsz % rows == 0
    fwd = pl.BlockSpec((rows, cn, width), lambda b, c: (b, c, 0))
    bwd = pl.BlockSpec((rows, cn, width), lambda b, c: (b, nc - 1 - c, 0))
    fwd_d = pl.BlockSpec((1, rows, cn, width), lambda b, c: (0, b, c, 0))
    bwd_d = pl.BlockSpec((1, rows, cn, width), lambda b, c: (1, b, nc - 1 - c, 0))
    out = jax.ShapeDtypeStruct((bsz, seq, width), F32)
    return pl.pallas_call(
        _rwkv_kernel,
        name="rwkv",
        grid=(bsz // rows, nc),
        in_specs=[fwd, fwd, fwd, fwd_d, fwd_d, fwd_d, bwd, bwd, bwd, bwd_d, bwd_d, bwd_d],
        out_specs=[fwd, bwd],
        out_shape=[out, out],
        scratch_shapes=[pltpu.VMEM((2, rows, width // LANES, LANES, LANES), F32)],
        compiler_params=_cparams(("parallel", "arbitrary")),
    )(r, v, kk, lw, kd, b, r, v, kk, lw, kd, b)


def _head_sum(x, ones, passes):
    gw = ones.shape[0]
    cols = []
    for c0 in range(0, x.shape[1], gw):
        xc = x[:, c0:c0 + gw]
        hi = xc.astype(BF16)
        out = jnp.dot(hi, ones, preferred_element_type=F32)
        if passes == 2:
            lo = (xc - hi.astype(F32)).astype(BF16)
            out = out + jnp.dot(lo, ones, preferred_element_type=F32)
        cols.append(out)
    return jnp.concatenate(cols, axis=1)


def _prep_kernel(r_ref, k_ref, v_ref, l_ref, rp_ref, kp_ref, vp_ref, lp_ref, rn_ref, kn_ref, vn_ref, ln_ref,
                 mu_ref, dbase_ref, dup_ref, ibase_ref, iup_ref, gup_ref, kkw_ref, ka_ref, rk_ref, ones_ref,
                 ro_ref, vo_ref, kko_ref, go_ref, bvo_ref, lwo_ref, kdo_ref, bo_ref, *, tm, seq):
    i = pl.program_id(0)
    bpr = seq // tm
    has_prev = (i % bpr != 0).astype(F32)
    has_next = (i % bpr != bpr - 1).astype(F32)
    rowi = lax.broadcasted_iota(jnp.int32, (tm, 1), 0)
    cw = RW_WIDTH

    def shifted(main_ref, prev_ref, next_ref, c0, c1):
        z = main_ref[...]
        zp = jnp.where(rowi == 0, prev_ref[SUBLANES - 1:SUBLANES, :] * has_prev, pltpu.roll(z, 1, 0))
        zn = jnp.where(rowi == tm - 1, next_ref[0:1, :] * has_next, pltpu.roll(z, tm - 1, 0))
        return z + mu_ref[0:1, c0:c1] * (zp - z) + mu_ref[1:2, c0:c1] * (zn - z)

    r = shifted(r_ref, rp_ref, rn_ref, 0, cw)
    k = shifted(k_ref, kp_ref, kn_ref, cw, 2 * cw)
    v = shifted(v_ref, vp_ref, vn_ref, 2 * cw, 3 * cw)
    lat = shifted(l_ref, lp_ref, ln_ref, 3 * cw, 3 * cw + LAT_PAD)
    nd, ni = 2 * DECAY_LORA, 2 * ICLR_LORA
    wl = jnp.tanh(lat[:, :nd])
    wl_hi = wl.astype(BF16)
    wl_lo = (wl - wl_hi.astype(F32)).astype(BF16)
    w_raw = (dbase_ref[...] + jnp.dot(wl_hi, dup_ref[0], preferred_element_type=F32)
             + jnp.dot(wl_lo, dup_ref[0], preferred_element_type=F32)
             + jnp.dot(wl_hi, dup_ref[1], preferred_element_type=F32))
    lw = jax.nn.sigmoid(w_raw) * (-DECAY_SCALE)
    a = jax.nn.sigmoid(ibase_ref[...] + jnp.dot(lat[:, nd:nd + ni].astype(BF16), iup_ref[...],
                                                preferred_element_type=F32))
    g = jnp.dot(jax.nn.sigmoid(lat[:, nd + ni:]).astype(BF16), gup_ref[...], preferred_element_type=F32)
    ones = ones_ref[...]
    kkr = k * kkw_ref[...]
    kk = kkr / jnp.maximum(jnp.sqrt(_head_sum(kkr * kkr, ones, 2)), 1e-12)
    ka = ka_ref[...]
    k_fix = k * (1.0 - ka)
    k_var = k * ka
    kd0 = k_fix + k_var * a[:, :cw]
    kd1 = k_fix + k_var * a[:, cw:]
    bonus = _head_sum(r * (kd0 + kd1) * rk_ref[...], ones, 2)
    ro_ref[...] = r
    vo_ref[...] = v.astype(vo_ref.dtype)
    kko_ref[...] = kk
    go_ref[...] = g.astype(go_ref.dtype)
    bvo_ref[...] = (bonus * v).astype(bvo_ref.dtype)
    lwo_ref[0] = lw[:, :cw]
    lwo_ref[1] = lw[:, cw:]
    kdo_ref[0] = kd0
    kdo_ref[1] = kd1
    bo_ref[0] = kk * a[:, :cw]
    bo_ref[1] = kk * a[:, cw:]


def _prep(proj, mu, dbase, dup, ibase, iup, gup, kkw, ka, rk, ones, seq, tm=256):
    m = proj.shape[0]
    cw = RW_WIDTH
    bpr = seq // tm
    hb = tm // SUBLANES
    nhb = m // SUBLANES
    r0 = RW_OFF // cw
    l0 = LAT_OFF // LAT_PAD
    cols = [(cw, r0), (cw, r0 + 1), (cw, r0 + 2), (LAT_PAD, l0)]
    main = [pl.BlockSpec((tm, w), lambda i, c=c: (i, c)) for w, c in cols]
    prev = [pl.BlockSpec((SUBLANES, w), lambda i, c=c: (jnp.maximum(i * hb - 1, 0), c)) for w, c in cols]
    nxt = [pl.BlockSpec((SUBLANES, w), lambda i, c=c: (jnp.minimum((i + 1) * hb, nhb - 1), c)) for w, c in cols]
    params = [mu, dbase, dup, ibase, iup, gup, kkw, ka, rk, ones]
    pspecs = [pl.BlockSpec(t.shape, lambda i, n=t.ndim: (0,) * n) for t in params]
    oblk = pl.BlockSpec((tm, cw), lambda i: (i, 0))
    dblk = pl.BlockSpec((2, tm, cw), lambda i: (0, i, 0))
    one = jax.ShapeDtypeStruct((m, cw), F32)
    half = jax.ShapeDtypeStruct((m, cw), BF16)
    two = jax.ShapeDtypeStruct((2, m, cw), F32)
    return pl.pallas_call(
        functools.partial(_prep_kernel, tm=tm, seq=seq),
        name="rwkv_prep",
        grid=(m // tm,),
        in_specs=main + prev + nxt + pspecs,
        out_specs=[oblk] * 5 + [dblk] * 3,
        out_shape=[one, half, one, half, half] + [two] * 3,
        compiler_params=_cparams(("parallel",)),
    )(*([proj] * 12), *params)


def _mix_kernel(att_ref, yf_ref, yb_ref, bv_ref, g_ref, lnw_ref, lnb_ref, ones_ref, ga_ref, gr_ref, x_ref, gt_ref,
                gain_ref, wa_ref, wr_ref, wo_ref, o_ref):
    ones = ones_ref[...]
    y = yf_ref[...] + yb_ref[...]
    mu = _head_sum(y, ones, 1) * (1.0 / RW_HEAD)
    yc = y - mu
    var = _head_sum(yc * yc, ones, 1) * (1.0 / RW_HEAD)
    rw = ((yc * lax.rsqrt(var + GN_EPS)) * lnw_ref[...] + lnb_ref[...] + bv_ref[...]) * g_ref[...]
    pa = jnp.dot(att_ref[...], wa_ref[...], preferred_element_type=F32)
    pr = jnp.dot(rw.astype(BF16), wr_ref[...], preferred_element_type=F32)
    merged = jax.nn.sigmoid(ga_ref[...]) * pa + jax.nn.sigmoid(gr_ref[...]) * pr
    mix = jnp.dot(merged.astype(BF16), wo_ref[...], preferred_element_type=F32)
    o_ref[...] = x_ref[...] + gt_ref[0] * _rms(mix, gain_ref[...])


def _mix(att, yf, yb, bv, g, lnw, lnb, ones, proj, x, gt, gain, wa, wr, wo, seq, tm=256):
    m, d = x.shape
    cw = RW_WIDTH
    bpr = seq // tm
    gblk = GATE_OFF // d
    const = lambda i: (0, 0)
    resident = lambda t: pl.BlockSpec(t.shape, const, pipeline_mode=pl.Buffered(1))
    return pl.pallas_call(
        _mix_kernel,
        name="mix",
        grid=(m // tm,),
        in_specs=[pl.BlockSpec((tm, att.shape[1]), lambda i: (i, 0)),
                  pl.BlockSpec((tm, cw), lambda i: (i, 0)),
                  pl.BlockSpec((tm, cw), lambda i: (i, 0)),
                  pl.BlockSpec((tm, cw), lambda i: (i, 0)),
                  pl.BlockSpec((tm, cw), lambda i: (i, 0)),
                  pl.BlockSpec((1, cw), const),
                  pl.BlockSpec((1, cw), const),
                  resident(ones),
                  pl.BlockSpec((tm, d), lambda i: (i, gblk)),
                  pl.BlockSpec((tm, d), lambda i: (i, gblk + 1)),
                  pl.BlockSpec((tm, d), lambda i: (i, 0)),
                  pl.BlockSpec((1, 1, d), lambda i: (i // bpr, 0, 0)),
                  pl.BlockSpec((1, d), const),
                  resident(wa), resident(wr), resident(wo)],
        out_specs=pl.BlockSpec((tm, d), lambda i: (i, 0)),
        out_shape=jax.ShapeDtypeStruct((m, d), F32),
        compiler_params=_cparams(("parallel",)),
    )(att, yf, yb, bv, g, lnw, lnb, ones, proj, proj, x, gt, gain, wa, wr, wo)


def _ffn_kernel(x_ref, xp_ref, xn_ref, gpre_ref, sc_ref, sh_ref, wg_ref, wu_ref, cw_ref, cb_ref, wd_ref,
                gt_ref, gpost_ref, o_ref, h_ref, *, tm, seq):
    i = pl.program_id(0)
    j = pl.program_id(1)
    halo = SUBLANES
    bpr = seq // tm

    @pl.when(j == 0)
    def _():
        def modnorm(x):
            return _rms(x, gpre_ref[...]) * (1.0 + sc_ref[0]) + sh_ref[0]

        top = lax.broadcasted_iota(jnp.int32, (BF16_ROWS, 1), 0) < halo
        keep_first = jnp.where(top, (i % bpr != 0).astype(F32), 1.0)
        keep_last = jnp.where(top, 1.0, (i % bpr != bpr - 1).astype(F32))
        first = jnp.concatenate([xp_ref[...], x_ref[0:halo, :]], axis=0)
        last = jnp.concatenate([x_ref[tm - halo:tm, :], xn_ref[...]], axis=0)
        h_ref[0:BF16_ROWS, :] = (modnorm(first) * keep_first).astype(BF16)
        h_ref[tm:tm + BF16_ROWS, :] = (modnorm(last) * keep_last).astype(BF16)

        def middle(r0):
            h_ref[r0:r0 + BF16_ROWS, :] = modnorm(x_ref[r0 - halo:r0 - halo + BF16_ROWS, :]).astype(BF16)

        _row_chunks(1, tm // BF16_ROWS, middle, None)
        o_ref[...] = jnp.zeros_like(o_ref)

    ext = tm + 2 * halo
    gate = jnp.dot(h_ref[...], wg_ref[...], preferred_element_type=F32)
    prev = pltpu.roll(gate, 1, 0)[halo:halo + tm]
    nxt = pltpu.roll(gate, ext - 1, 0)[halo:halo + tm]
    u = cw_ref[0:1, :] * prev + cw_ref[1:2, :] * gate[halo:halo + tm] + cw_ref[2:3, :] * nxt + cb_ref[...]
    up = jnp.dot(h_ref[halo:halo + tm, :], wu_ref[...], preferred_element_type=F32)
    act = jax.nn.gelu(u, approximate=True) * up
    o_ref[...] += jnp.dot(act.astype(BF16), wd_ref[...], preferred_element_type=F32)

    @pl.when(j == pl.num_programs(1) - 1)
    def _():
        def finish(r0):
            rows = slice(r0, r0 + BF16_ROWS)
            o_ref[rows, :] = x_ref[rows, :] + gt_ref[0] * _rms(o_ref[rows, :], gpost_ref[...])

        _row_chunks(0, tm // BF16_ROWS, finish, None)


def _ffn(x, gpre, sc, sh, wg, wu, cw, cb, wd, gt, gpost, seq, tm=1024, tf=FF_TILE):
    m, d = x.shape
    f = wg.shape[1]
    bpr = seq // tm
    hb = tm // SUBLANES
    nhb = m // SUBLANES
    const = lambda i, j: (0, 0)
    bidx = lambda i, j: (i // bpr, 0, 0)
    return pl.pallas_call(
        functools.partial(_ffn_kernel, tm=tm, seq=seq),
        name="ffn",
        grid=(m // tm, f // tf),
        in_specs=[pl.BlockSpec((tm, d), lambda i, j: (i, 0)),
                  pl.BlockSpec((SUBLANES, d), lambda i, j: (jnp.maximum(i * hb - 1, 0), 0)),
                  pl.BlockSpec((SUBLANES, d), lambda i, j: (jnp.minimum((i + 1) * hb, nhb - 1), 0)),
                  pl.BlockSpec((1, d), const),
                  pl.BlockSpec((1, 1, d), bidx),
                  pl.BlockSpec((1, 1, d), bidx),
                  pl.BlockSpec((d, tf), lambda i, j: (0, j)),
                  pl.BlockSpec((d, tf), lambda i, j: (0, j)),
                  pl.BlockSpec((SUBLANES, tf), lambda i, j: (0, j)),
                  pl.BlockSpec((1, tf), lambda i, j: (0, j)),
                  pl.BlockSpec((tf, d), lambda i, j: (j, 0)),
                  pl.BlockSpec((1, 1, d), bidx),
                  pl.BlockSpec((1, d), const)],
        out_specs=pl.BlockSpec((tm, d), lambda i, j: (i, 0)),
        out_shape=jax.ShapeDtypeStruct((m, d), F32),
        scratch_shapes=[pltpu.VMEM((tm + 2 * SUBLANES, d), BF16)],
        compiler_params=_cparams(("parallel", "arbitrary")),
    )(x, x, x, gpre, sc, sh, wg, wu, cw, cb, wd, gt, gpost)


def _rope_tables(seq):
    half = ATT_HEAD_DIM // 2
    inv_freq = 1.0 / (ROPE_THETA ** (jnp.arange(half, dtype=F32) / half))
    ang = jnp.arange(seq, dtype=F32)[:, None] * inv_freq[None, :]
    cos = jnp.cos(ang)
    sin = jnp.sin(ang)
    return jnp.concatenate([cos, cos], axis=-1), jnp.concatenate([-sin, sin], axis=-1)


def _trunk(x, ada, p):
    bsz, seq, d = x.shape
    m = bsz * seq
    sh1, sc1, gt1, sh2, sc2, gt2 = [t[:, None, :] for t in jnp.split(ada, 6, axis=-1)]
    x2 = x.reshape(m, d)
    proj = _inproj(x2, p['ln_mix_pre'], sc1, sh1, p['w_in'], seq)
    proj3 = proj.reshape(bsz, seq, N_IN_PAD)
    cos, sin = _rope_tables(seq)
    att = _attention(proj3, cos, sin)
    r, v, kk, g, bv, lw, kd, b = _prep(proj, p['shift_mu'], p['decay_base'], p['decay_up'], p['iclr_base'],
                                       p['iclr_up'], p['gate_up'], p['k_k'], p['k_a'], p['r_k'], p['head_ones'], seq)
    per_seq = lambda t: t.reshape(t.shape[:-2] + (bsz, seq, RW_WIDTH))
    yf, yb = _rwkv_scan(per_seq(r), per_seq(v), per_seq(kk), per_seq(lw), per_seq(kd), per_seq(b))
    x1 = _mix(att.reshape(m, ATT_OUT), yf.reshape(m, RW_WIDTH), yb.reshape(m, RW_WIDTH), bv, g,
              p['lnx_w'], p['lnx_b'], p['head_ones'],
              proj, x2, gt1, p['ln_mix_post'], p['w_att_branch'], p['w_rwkv_branch'], p['w_out'], seq)
    y = _ffn(x1, p['ln_ffn_pre'], sc2, sh2, p['w_ffn_gate'], p['w_ffn_up'], p['ffn_conv_w'], p['ffn_conv_b'],
             p['w_ffn_down'], gt2, p['ln_ffn_post'], seq)
    return y.reshape(bsz, seq, d)


def kernel(x_prompt, x_sample, c_prompt, c_sample, ln_mix_pre, ln_mix_post, ln_ffn_pre, ln_ffn_post,
           w_ada, b_ada, w_in, shift_mu, decay_base, decay_up, iclr_base, iclr_up, gate_up,
           k_k, k_a, r_k, lnx_w, lnx_b, w_att_branch, w_rwkv_branch, w_out,
           w_ffn_gate, w_ffn_up, ffn_conv_w, ffn_conv_b, w_ffn_down):
    depth = w_in.shape[0]
    nb_p = c_prompt.shape[0]
    nb_s = c_sample.shape[0]
    c_all = jnp.concatenate([c_prompt, c_sample], axis=0)
    c_all = jnp.pad(c_all, ((0, (-c_all.shape[0]) % SUBLANES), (0, 0)))
    fpad = D_FF_PAD - D_FF
    xp, xs = x_prompt, x_sample
    lane_head = jnp.arange(MXU_DIM_V7X) // RW_HEAD
    head_ones = (lane_head[:, None] == lane_head[None, :]).astype(BF16)
    zdl = jnp.zeros((DECAY_LORA, RW_WIDTH), F32)
    zil = jnp.zeros((ICLR_LORA, RW_WIDTH), F32)
    for l in range(depth):
        att_end = 3 * ATT_WIDTH
        rw_end = att_end + RW_COLS
        w_in_l = jnp.concatenate([w_in[l][:, rw_end:].astype(BF16), w_in[l][:, att_end:rw_end].astype(BF16),
                                  jnp.zeros((D_MODEL, LAT_PAD - LAT_COLS), BF16), w_in[l][:, :att_end].astype(BF16)],
                                 axis=1)
        dup = jnp.block([[decay_up[l][0], zdl], [zdl, decay_up[l][1]]])
        dup_hi = dup.astype(BF16)
        p = dict(
            ln_mix_pre=ln_mix_pre[l][None], ln_mix_post=ln_mix_post[l][None],
            ln_ffn_pre=ln_ffn_pre[l][None], ln_ffn_post=ln_ffn_post[l][None],
            w_in=w_in_l, head_ones=head_ones,
            shift_mu=jnp.pad(shift_mu[l], ((0, 0), (0, RW_COLS_PAD - RW_COLS))),
            decay_base=decay_base[l].reshape(1, 2 * RW_WIDTH),
            decay_up=jnp.stack([dup_hi, (dup - dup_hi.astype(F32)).astype(BF16)]),
            iclr_base=iclr_base[l].reshape(1, 2 * RW_WIDTH),
            iclr_up=jnp.block([[iclr_up[l][0], zil], [zil, iclr_up[l][1]]]).astype(BF16),
            gate_up=jnp.pad(gate_up[l], ((0, LAT_PAD - LAT_COLS), (0, 0))).astype(BF16),
            k_k=k_k[l][None], k_a=k_a[l][None], r_k=r_k[l].reshape(1, RW_WIDTH),
            lnx_w=lnx_w[l][None], lnx_b=lnx_b[l][None],
            w_att_branch=w_att_branch[l].astype(BF16), w_rwkv_branch=w_rwkv_branch[l].astype(BF16),
            w_out=w_out[l].astype(BF16),
            w_ffn_gate=jnp.zeros((D_MODEL, D_FF_PAD), BF16).at[:, :D_FF].set(w_ffn_gate[l].astype(BF16)),
            w_ffn_up=jnp.zeros((D_MODEL, D_FF_PAD), BF16).at[:, :D_FF].set(w_ffn_up[l].astype(BF16)),
            ffn_conv_w=jnp.pad(ffn_conv_w[l], ((0, SUBLANES - 3), (0, fpad))),
            ffn_conv_b=jnp.pad(ffn_conv_b[l], ((0, fpad),))[None],
            w_ffn_down=jnp.zeros((D_FF_PAD, D_MODEL), BF16).at[:D_FF, :].set(w_ffn_down[l].astype(BF16)),
        )
        ada = _ada(c_all, w_ada[l], b_ada[l][None])
        xp = _trunk(xp, ada[:nb_p], p)
        xs = _trunk(xs, ada[nb_p:nb_p + nb_s], p)
    return (xp, xs)
```
